```python
import math
import jax, jax.numpy as jnp
from jax import lax
import numpy as np

D_MODEL = 2048
BATCH = 2
SEQ = 16384
DEPTH = 1
DEC_BATCH = 8
DEC_SEQ = 64
PAST_LEN = 1024

CHUNK = 64
Q_BLOCK = 128
ATTN_WIDTH = D_MODEL // 2
CONV_WIDTH = D_MODEL - ATTN_WIDTH
N_HEADS = 8
HEAD_DIM = ATTN_WIDTH // (2 * N_HEADS)
V_DIM = 2 * HEAD_DIM
CONV_K = 3
CONV_GROUPS = 16
N_EXPERTS = 32
TOP_K = 4
D_FF = D_MODEL
SWIGLU_ALPHA = 1.702
SWIGLU_LIMIT = 7.0
NORM_EPS = 1e-6
IN_COLS = 3 * ATTN_WIDTH + 3 * CONV_WIDTH

kernel_name = "diffattn_shortconv_moe_streaming_encoder"


def rms_norm(x, g):
    xf = x.astype(jnp.float32)
    y = xf * lax.rsqrt(jnp.mean(xf * xf, axis=-1, keepdims=True) + NORM_EPS)
    return (y * g.astype(jnp.float32)).astype(x.dtype)


def alibi_slopes():
    return jnp.asarray(2.0 ** (-8.0 * np.arange(1, N_HEADS + 1) / N_HEADS), jnp.float32)


def diff_attend(q, k, v, pos_q, pos_k, visible, lam):
    s = jnp.einsum('bqhjd,bkhjd->bhjqk', q, k).astype(jnp.float32) * (HEAD_DIM ** -0.5)
    dist = jnp.abs(pos_q[:, None] - pos_k[None, :]).astype(jnp.float32)
    s = s - (alibi_slopes()[:, None, None] * dist)[None, :, None]
    if visible is not None:
        s = jnp.where(visible, s, -jnp.inf)
    p = jax.nn.softmax(s, axis=-1)
    a = p[:, :, 0] - lam * p[:, :, 1]
    return jnp.einsum('bhqk,bkhe->bqhe', a.astype(v.dtype), v)


def attn_prompt(q, k, v, lam):
    b, s = q.shape[0], q.shape[1]
    n_blk = s // Q_BLOCK
    qb = jnp.moveaxis(q.reshape(b, n_blk, Q_BLOCK, N_HEADS, 2, HEAD_DIM), 1, 0)
    pos_k = jnp.arange(s)

    def one(args):
        i, q_blk = args
        pos_q = i * Q_BLOCK + jnp.arange(Q_BLOCK)
        visible = (pos_k // CHUNK)[None, :] <= (pos_q // CHUNK)[:, None]
        return diff_attend(q_blk, k, v, pos_q, pos_k, visible, lam)

    o = lax.map(one, (jnp.arange(n_blk), qb))
    return jnp.moveaxis(o, 0, 1).reshape(b, s, N_HEADS, V_DIM)


def attn_sample(q, k_new, v_new, past_k, past_v, lam):
    t, p = q.shape[1], past_k.shape[1]
    k = jnp.concatenate([past_k, k_new], axis=1)
    v = jnp.concatenate([past_v, v_new], axis=1)
    pos_q = p + jnp.arange(t)
    pos_k = jnp.arange(p + t)
    return diff_attend(q, k, v, pos_q, pos_k, None, lam)


def short_conv(u_in, gate_b, gate_c, conv_w, conv_state):
    u = gate_c * u_in
    full = jnp.concatenate([conv_state, u], axis=1)
    t = u.shape[1]
    y = sum(conv_w[j] * full[:, j:j + t] for j in range(CONV_K))
    return gate_b * y, full[:, -(CONV_K - 1):]


def moe(h, w_router, b_router, w_gu, b_gu, w_down, b_down):
    t, d = h.shape
    logits = h.astype(jnp.float32) @ w_router.astype(jnp.float32) + b_router.astype(jnp.float32)
    top_v, top_e = lax.top_k(logits, TOP_K)
    gate = jax.nn.softmax(top_v, axis=-1)
    m = t * TOP_K
    rows = max(64, min(512, m // (4 * N_EXPERTS)))
    n_blocks = -(-m // rows) + N_EXPERTS
    flat_e = top_e.reshape(-1)
    counts = jnp.bincount(flat_e, length=N_EXPERTS)
    padded = ((counts + rows - 1) // rows) * rows
    cum_end = jnp.cumsum(padded)
    pstart = cum_end - padded
    start = jnp.cumsum(counts) - counts
    order = jnp.argsort(flat_e)
    sorted_e = flat_e[order]
    dest = pstart[sorted_e] + (jnp.arange(m) - start[sorted_e])
    buf_tok = jnp.full((n_blocks * rows,), t, jnp.int32).at[dest].set((order // TOP_K).astype(jnp.int32))
    buf_gate = jnp.zeros((n_blocks * rows,), jnp.float32).at[dest].set(gate.reshape(-1)[order])
    block_e = jnp.minimum(jnp.searchsorted(cum_end, jnp.arange(n_blocks) * rows, side='right'), N_EXPERTS - 1)
    h_pad = jnp.concatenate([h, jnp.zeros((1, d), h.dtype)], axis=0)
    xb = h_pad[buf_tok].reshape(n_blocks, rows, d)

    def expert_block(args):
        x_blk, e = args
        gu = x_blk @ w_gu[e] + b_gu[e]
        g, up = gu[:, :D_FF], gu[:, D_FF:]
        g = jnp.minimum(g, SWIGLU_LIMIT)
        up = jnp.clip(up, -SWIGLU_LIMIT, SWIGLU_LIMIT)
        act = (up + 1) * g * jax.nn.sigmoid(SWIGLU_ALPHA * g)
        return act @ w_down[e] + b_down[e]

    yb = lax.map(expert_block, (xb, block_e)).reshape(n_blocks * rows, d)
    y = jnp.zeros((t + 1, d), h.dtype).at[buf_tok].add(yb * buf_gate[:, None].astype(yb.dtype))
    return y[:t]


def trunk_layer(x, c, past_k, past_v, conv_state, layer_idx, g_mix, g_ffn, w_ada, b_ada, w_in,
                lambda_q1, lambda_k1, lambda_q2, lambda_k2, subln_g, conv_w, w_o,
                w_router, b_router, w_gu, b_gu, w_down, b_down):
    b, t, d = x.shape
    ada = jax.nn.silu(c) @ w_ada + b_ada
    shift1, scale1, gate1, shift2, scale2, gate2 = jnp.split(ada, 6, axis=-1)

    h = rms_norm(x, g_mix) * (1 + scale1[:, None]) + shift1[:, None]
    z = h @ w_in
    a_w, c_w = ATTN_WIDTH, CONV_WIDTH
    q, k, v, u, gb, gc = jnp.split(z, [a_w, 2 * a_w, 3 * a_w, 3 * a_w + c_w, 3 * a_w + 2 * c_w], axis=-1)
    q = q.reshape(b, t, N_HEADS, 2, HEAD_DIM)
    k = k.reshape(b, t, N_HEADS, 2, HEAD_DIM)
    v = v.reshape(b, t, N_HEADS, V_DIM)

    lam_init = 0.8 - 0.6 * math.exp(-0.3 * layer_idx)
    lam = (jnp.exp(jnp.sum(lambda_q1.astype(jnp.float32) * lambda_k1.astype(jnp.float32)))
           - jnp.exp(jnp.sum(lambda_q2.astype(jnp.float32) * lambda_k2.astype(jnp.float32))) + lam_init)
    if past_k is None:
        o = attn_prompt(q, k, v, lam)
    else:
        o = attn_sample(q, k, v, past_k, past_v, lam)
    attn_out = (rms_norm(o, subln_g) * (1 - lam_init)).reshape(b, t, ATTN_WIDTH)

    conv_out, new_conv = short_conv(u, gb, gc, conv_w, conv_state)
    mix = jnp.concatenate([attn_out, conv_out], axis=-1) @ w_o
    x = x + gate1[:, None] * mix

    h2 = rms_norm(x, g_ffn) * (1 + scale2[:, None]) + shift2[:, None]
    ff = moe(h2.reshape(b * t, d), w_router, b_router, w_gu, b_gu, w_down, b_down).reshape(b, t, d)
    x = x + gate2[:, None] * ff
    return x, k, v, new_conv


def setup_inputs(seed: int = 0) -> dict:
    key = jax.random.key(seed)
    ks = jax.random.split(key, 32)
    f32 = jnp.float32
    nrm = lambda k, shape, s: jax.random.normal(k, shape, f32) * s
    D = D_MODEL
    return {
        "x_prompt": nrm(ks[0], (BATCH, SEQ, D), 1.0),
        "x_sample": nrm(ks[1], (DEC_BATCH, DEC_SEQ, D), 1.0),
        "cache_k": nrm(ks[2], (DEPTH, DEC_BATCH, PAST_LEN, N_HEADS, 2, HEAD_DIM), 1.0),
        "cache_v": nrm(ks[3], (DEPTH, DEC_BATCH, PAST_LEN, N_HEADS, V_DIM), 1.0),
        "state_conv": nrm(ks[4], (DEPTH, DEC_BATCH, CONV_K - 1, CONV_WIDTH), 1.0),
        "c_prompt": nrm(ks[5], (BATCH, D), 1.0),
        "c_sample": nrm(ks[6], (DEC_BATCH, D), 1.0),
        "g_mix": 1.0 + nrm(ks[7], (DEPTH, D), 0.02),
        "g_ffn": 1.0 + nrm(ks[8], (DEPTH, D), 0.02),
        "w_ada": nrm(ks[9], (DEPTH, D, 6 * D), 0.5 * D ** -0.5),
        "b_ada": nrm(ks[10], (DEPTH, 6 * D), 0.01),
        "w_in": nrm(ks[11], (DEPTH, D, IN_COLS), D ** -0.5),
        "lambda_q1": nrm(ks[12], (DEPTH, HEAD_DIM), 0.1),
        "lambda_k1": nrm(ks[13], (DEPTH, HEAD_DIM), 0.1),
        "lambda_q2": nrm(ks[14], (DEPTH, HEAD_DIM), 0.1),
        "lambda_k2": nrm(ks[15], (DEPTH, HEAD_DIM), 0.1),
        "subln_g": 1.0 + nrm(ks[16], (DEPTH, V_DIM), 0.02),
        "conv_w": nrm(ks[17], (DEPTH, CONV_K, CONV_WIDTH), CONV_K ** -0.5),
        "w_o": nrm(ks[18], (DEPTH, ATTN_WIDTH + CONV_WIDTH, D), (ATTN_WIDTH + CONV_WIDTH) ** -0.5),
        "w_router": nrm(ks[19], (DEPTH, D, N_EXPERTS), D ** -0.5),
        "b_router": nrm(ks[20], (DEPTH, N_EXPERTS), 0.01),
        "w_gu": nrm(ks[21], (DEPTH, N_EXPERTS, D, 2 * D_FF), D ** -0.5),
        "b_gu": nrm(ks[22], (DEPTH, N_EXPERTS, 2 * D_FF), 0.01),
        "w_down": nrm(ks[23], (DEPTH, N_EXPERTS, D_FF, D), D_FF ** -0.5),
        "b_down": nrm(ks[24], (DEPTH, N_EXPERTS, D), 0.01),
        "g_final": 1.0 + nrm(ks[25], (D,), 0.02),
    }


def reference(x_prompt, x_sample, cache_k, cache_v, state_conv, c_prompt, c_sample,
              g_mix, g_ffn, w_ada, b_ada, w_in, lambda_q1, lambda_k1, lambda_q2, lambda_k2,
              subln_g, conv_w, w_o, w_router, b_router, w_gu, b_gu, w_down, b_down, g_final):
    xp, xs = x_prompt, x_sample
    kp, vp, cp, ksl, vsl, csl = [], [], [], [], [], []
    for l in range(DEPTH):
        lp = (g_mix[l], g_ffn[l], w_ada[l], b_ada[l], w_in[l], lambda_q1[l], lambda_k1[l],
              lambda_q2[l], lambda_k2[l], subln_g[l], conv_w[l], w_o[l], w_router[l], b_router[l],
              w_gu[l], b_gu[l], w_down[l], b_down[l])
        zero_state = jnp.zeros((xp.shape[0], CONV_K - 1, CONV_WIDTH), xp.dtype)
        xp, k1, v1, s1 = trunk_layer(xp, c_prompt, None, None, zero_state, l, *lp)
        xs, k2, v2, s2 = trunk_layer(xs, c_sample, cache_k[l], cache_v[l], state_conv[l], l, *lp)
        kp.append(k1); vp.append(v1); cp.append(s1)
        ksl.append(k2); vsl.append(v2); csl.append(s2)
    y_prompt = rms_norm(xp, g_final)
    y_sample = rms_norm(xs, g_final)
    return (y_prompt, y_sample, jnp.stack(kp), jnp.stack(vp), jnp.stack(cp), jnp.stack(ksl), jnp.stack(vsl), jnp.stack(csl))
```

```python
import functools
import math

import jax
import jax.numpy as jnp
import numpy as np
from jax import lax
from jax.experimental import pallas as pl
from jax.experimental.pallas import tpu as pltpu

_F32 = jnp.float32
_BF16 = jnp.bfloat16

CHUNK = 64
N_HEADS = 8
TOP_K = 4
NORM_EPS = 1e-6
SWIGLU_ALPHA = 1.702
SWIGLU_LIMIT = 7.0
MASKED_SCORE = -1e30
ROUTER_LANES = 128

V7X_VMEM_LIMIT_BYTES = 56 * 1024 * 1024


def _params(semantics, vmem_bytes=V7X_VMEM_LIMIT_BYTES):
    return pltpu.CompilerParams(dimension_semantics=semantics, vmem_limit_bytes=vmem_bytes)


def _rms(x):
    return x * lax.rsqrt(jnp.mean(x * x, axis=-1, keepdims=True) + NORM_EPS)


def _ada_kernel(c_ref, w_ref, b_ref, o_ref):
    c = c_ref[...]
    s = (c * jax.nn.sigmoid(c)).astype(_BF16)
    o_ref[...] = jnp.dot(s, w_ref[...].astype(_BF16), preferred_element_type=_F32) + b_ref[...]


def _ada(c_all, w_ada, b_ada, tn=1024):
    rows, d = c_all.shape
    n = w_ada.shape[1]
    return pl.pallas_call(
        _ada_kernel,
        grid=(n // tn,),
        in_specs=[pl.BlockSpec((rows, d), lambda j: (0, 0)),
                  pl.BlockSpec((d, tn), lambda j: (0, j)),
                  pl.BlockSpec((1, tn), lambda j: (0, j))],
        out_specs=pl.BlockSpec((rows, tn), lambda j: (0, j)),
        out_shape=jax.ShapeDtypeStruct((rows, n), _F32),
        compiler_params=_params(("arbitrary",)),
        name="ada",
    )(c_all, w_ada, b_ada.reshape(1, n))


def _inproj_kernel(x_ref, shift_ref, scale_ref, g_ref, w_ref, cw_ref, st_ref,
                   q_ref, k_ref, v_ref, kb_ref, vt_ref, co_ref, cs_ref,
                   ug_scr, carry_scr, *, width, seg, tiles_per_batch, use_state, q_scale):
    i = pl.program_id(0)
    tm = x_ref.shape[0]
    h = _rms(x_ref[...]) * g_ref[...]
    h = (h * (1.0 + scale_ref[0]) + shift_ref[0]).astype(_BF16)

    def proj(j):
        return jnp.dot(h, w_ref[:, j * width:(j + 1) * width], preferred_element_type=_F32)

    q_ref[...] = (proj(0) * q_scale).astype(_BF16)
    zk = proj(1)
    k_ref[...] = zk
    kb_ref[...] = zk.astype(_BF16)
    zv = proj(2)
    v_ref[...] = zv
    vt_ref[...] = zv.T.astype(_BF16)

    ug = proj(3) * proj(5)
    gb = proj(4)
    w0 = cw_ref[0:1, :]
    w1 = cw_ref[1:2, :]
    w2 = cw_ref[2:3, :]
    if not use_state:
        @pl.when((i % tiles_per_batch) == 0)
        def _():
            carry_scr[...] = jnp.zeros_like(carry_scr)

    for s in range(tm // seg):
        lo = s * seg
        bnd = st_ref[s] if use_state else carry_scr[...]
        ug_s = ug[lo:lo + seg]
        ug_scr[6:8, :] = bnd
        ug_scr[8:8 + seg, :] = ug_s
        y = w2 * ug_s + w1 * ug_scr[7:7 + seg, :] + w0 * ug_scr[6:6 + seg, :]
        co_ref[lo:lo + seg, :] = (gb[lo:lo + seg] * y).astype(_BF16)
        last2 = ug_scr[6 + seg:8 + seg, :]
        cs_ref[s] = last2
        carry_scr[...] = last2


def _inproj(x2d, shift, scale, g_mix, w_in_bf, conv_w, state, *, tm, seg, tiles_per_batch,
            use_state, q_scale):
    r, d = x2d.shape
    width = w_in_bf.shape[1] // 6
    n_tiles = r // tm
    n_seg = tm // seg
    mod_rows = shift.shape[1]
    if use_state:
        mod_map = lambda i: (i, 0, 0)
        st_spec = pl.BlockSpec((n_seg, 2, width), lambda i: (i, 0, 0))
        cs_spec = pl.BlockSpec((n_seg, 2, width), lambda i: (i, 0, 0))
        n_state = n_tiles * n_seg
    else:
        mod_map = lambda i: (i // tiles_per_batch, 0, 0)
        st_spec = pl.BlockSpec((1, 2, width), lambda i: (0, 0, 0))
        cs_spec = pl.BlockSpec((1, 2, width), lambda i: (i // tiles_per_batch, 0, 0))
        n_state = n_tiles // tiles_per_batch
    kern = functools.partial(_inproj_kernel, width=width, seg=seg, tiles_per_batch=tiles_per_batch,
                             use_state=use_state, q_scale=q_scale)
    row_spec = lambda: pl.BlockSpec((tm, width), lambda i: (i, 0))
    return pl.pallas_call(
        kern,
        grid=(n_tiles,),
        in_specs=[pl.BlockSpec((tm, d), lambda i: (i, 0)),
                  pl.BlockSpec((1, mod_rows, d), mod_map),
                  pl.BlockSpec((1, mod_rows, d), mod_map),
                  pl.BlockSpec((1, d), lambda i: (0, 0)),
                  pl.BlockSpec((d, 6 * width), lambda i: (0, 0), pipeline_mode=pl.Buffered(1)),
                  pl.BlockSpec((3, width), lambda i: (0, 0)),
                  st_spec],
        out_specs=[row_spec(), row_spec(), row_spec(), row_spec(),
                   pl.BlockSpec((width, tm), lambda i: (0, i)),
                   row_spec(), cs_spec],
        out_shape=[jax.ShapeDtypeStruct((r, width), _BF16),
                   jax.ShapeDtypeStruct((r, width), _F32),
                   jax.ShapeDtypeStruct((r, width), _F32),
                   jax.ShapeDtypeStruct((r, width), _BF16),
                   jax.ShapeDtypeStruct((width, r), _BF16),
                   jax.ShapeDtypeStruct((r, width), _BF16),
                   jax.ShapeDtypeStruct((n_state, 2, width), _F32)],
        scratch_shapes=[pltpu.VMEM((seg + 8, width), _F32), pltpu.VMEM((2, width), _F32)],
        compiler_params=_params(("arbitrary",)),
        name="inproj",
    )(x2d, shift, scale, g_mix.reshape(1, d), w_in_bf, conv_w, state)


def _attn_prompt_kernel(scal_ref, q_ref, k_ref, vt_ref, g_ref, o_ref, acc_scr, *, tq, hd, out_scale):
    hh = pl.program_id(1)
    qi = pl.program_id(2)
    lam = scal_ref[0]
    slope = scal_ref[1 + hh]
    tk = tq

    qt = q_ref[...].astype(_F32).T.astype(_BF16)
    row = lax.broadcasted_iota(jnp.int32, (2 * hd, 2 * tq), 0)
    col = lax.broadcasted_iota(jnp.int32, (2 * hd, 2 * tq), 1)
    rhs = jnp.where((row < hd) == (col < tq), jnp.concatenate([qt, qt], axis=1), 0).astype(_BF16)

    jj = lax.broadcasted_iota(jnp.int32, (tk, tq), 0)
    ii = lax.broadcasted_iota(jnp.int32, (tk, tq), 1)
    jf = jj.astype(_F32)
    iif = ii.astype(_F32)
    off_bias = slope * jf
    visible = (jj // CHUNK) <= (ii // CHUNK)
    diag_bias = jnp.where(visible, slope * iif - slope * jnp.abs(iif - jf), MASKED_SCORE)
    off_bias2 = jnp.concatenate([off_bias, off_bias], axis=1)
    diag_bias2 = jnp.concatenate([diag_bias, diag_bias], axis=1)

    def block(kv, bias2, c):
        start = pl.multiple_of(kv * tk, tk)
        k = k_ref[pl.ds(start, tk), :]
        s = jnp.dot(k, rhs, preferred_element_type=_F32) + bias2
        vt = vt_ref[:, pl.ds(start, tk)]
        return s, vt

    s, vt = block(qi, diag_bias2, 0.0)
    m0 = jnp.max(s, axis=0, keepdims=True)
    p = jnp.exp(s - m0)
    l0 = jnp.sum(p, axis=0, keepdims=True)
    acc_scr[...] = jnp.dot(vt, p.astype(_BF16), preferred_element_type=_F32)

    def body(kv, carry):
        m, l = carry
        c = -slope * ((qi - kv) * tq).astype(_F32)
        s, vt = block(kv, off_bias2, c)
        m_new = jnp.maximum(m, jnp.max(s, axis=0, keepdims=True) + c)
        p = jnp.exp(s - (m_new - c))
        alpha = jnp.exp(m - m_new)
        l = alpha * l + jnp.sum(p, axis=0, keepdims=True)
        acc_scr[...] = acc_scr[...] * alpha + jnp.dot(vt, p.astype(_BF16), preferred_element_type=_F32)
        return m_new, l

    _, l = lax.fori_loop(0, qi, body, (m0, l0))

    acc = acc_scr[...] / l
    o = acc[:, :tq] - lam * acc[:, tq:]
    o = o * lax.rsqrt(jnp.mean(o * o, axis=0, keepdims=True) + NORM_EPS)
    o = o * g_ref[...] * out_scale
    o_ref[...] = o.T.astype(_BF16)


def _attn_prompt(scal, q, kb, vt, subln_col, *, batch, seq, tq, out_scale):
    width = q.shape[1]
    hw = width // N_HEADS
    nq = seq // tq
    kern = functools.partial(_attn_prompt_kernel, tq=tq, hd=hw // 2, out_scale=out_scale)
    grid_spec = pltpu.PrefetchScalarGridSpec(
        num_scalar_prefetch=0,
        grid=(batch, N_HEADS, nq),
        in_specs=[pl.BlockSpec(memory_space=pltpu.SMEM),
                  pl.BlockSpec((tq, hw), lambda b, h, i: (b * nq + i, h)),
                  pl.BlockSpec((seq, hw), lambda b, h, i: (b, h)),
                  pl.BlockSpec((hw, seq), lambda b, h, i: (h, b)),
                  pl.BlockSpec((hw, 1), lambda b, h, i: (0, 0))],
        out_specs=pl.BlockSpec((tq, hw), lambda b, h, i: (b * nq + i, h)),
        scratch_shapes=[pltpu.VMEM((hw, 2 * tq), _F32)],
    )
    return pl.pallas_call(
        kern,
        grid_spec=grid_spec,
        out_shape=jax.ShapeDtypeStruct((batch * seq, width), _BF16),
        compiler_params=_params(("arbitrary", "arbitrary", "arbitrary")),
        name="attn_prompt",
    )(scal, q, kb, vt, subln_col)


def _attn_sample_kernel(scal_ref, q_ref, kn_ref, vn_ref, kc_ref, vc_ref, g_ref, o_ref, *, hd, out_scale):
    hh = pl.program_id(1)
    lam = scal_ref[0]
    slope = scal_ref[1 + hh]
    t = q_ref.shape[0]
    past = kc_ref.shape[1]
    q = q_ref[...]
    kc = kc_ref[0].astype(_BF16)
    kn = kn_ref[...].astype(_BF16)
    vc = vc_ref[0].astype(_BF16)
    vn = vn_ref[...].astype(_BF16)
    qpos = lax.broadcasted_iota(jnp.int32, (t, past), 0).astype(_F32) + float(past)
    kpos = lax.broadcasted_iota(jnp.int32, (t, past), 1).astype(_F32)
    bias_c = -slope * jnp.abs(qpos - kpos)
    tq_i = lax.broadcasted_iota(jnp.int32, (t, t), 0).astype(_F32)
    tk_i = lax.broadcasted_iota(jnp.int32, (t, t), 1).astype(_F32)
    bias_n = -slope * jnp.abs(tq_i - tk_i)
    nt = (((1,), (1,)), ((), ()))
    outs = []
    for j in range(2):
        sl = slice(j * hd, (j + 1) * hd)
        sc = lax.dot_general(q[:, sl], kc[:, sl], nt, preferred_element_type=_F32) + bias_c
        sn = lax.dot_general(q[:, sl], kn[:, sl], nt, preferred_element_type=_F32) + bias_n
        m = jnp.maximum(jnp.max(sc, axis=-1, keepdims=True), jnp.max(sn, axis=-1, keepdims=True))
        pc = jnp.exp(sc - m)
        pn = jnp.exp(sn - m)
        l = jnp.sum(pc, axis=-1, keepdims=True) + jnp.sum(pn, axis=-1, keepdims=True)
        o = (jnp.dot(pc.astype(_BF16), vc, preferred_element_type=_F32)
             + jnp.dot(pn.astype(_BF16), vn, preferred_element_type=_F32))
        outs.append(o / l)
    o = outs[0] - lam * outs[1]
    o = _rms(o) * g_ref[...] * out_scale
    o_ref[...] = o.astype(_BF16)


def _attn_sample(scal, q, k_new, v_new, cache_k, cache_v, subln_row, *, batch, t, out_scale):
    width = q.shape[1]
    hw = width // N_HEADS
    past = cache_k.shape[1]
    kern = functools.partial(_attn_sample_kernel, hd=hw // 2, out_scale=out_scale)
    new_spec = lambda: pl.BlockSpec((t, hw), lambda b, h: (b, h))
    cache_spec = lambda: pl.BlockSpec((1, past, hw), lambda b, h: (b, 0, h))
    return pl.pallas_call(
        kern,
        grid=(batch, N_HEADS),
        in_specs=[pl.BlockSpec(memory_space=pltpu.SMEM), new_spec(), new_spec(), new_spec(),
                  cache_spec(), cache_spec(), pl.BlockSpec((1, hw), lambda b, h: (0, 0))],
        out_specs=new_spec(),
        out_shape=jax.ShapeDtypeStruct((batch * t, width), _BF16),
        compiler_params=_params(("arbitrary", "arbitrary")),
        name="attn_sample",
    )(scal, q, k_new, v_new, cache_k, cache_v, subln_row)


def _outproj_kernel(a_ref, c_ref, x_ref, gate_ref, shift_ref, scale_ref, g_ref, wo_ref, wr_ref, br_ref,
                    x1_ref, h2_ref, lg_ref, *, half):
    mix = (jnp.dot(a_ref[...], wo_ref[0:half, :], preferred_element_type=_F32)
           + jnp.dot(c_ref[...], wo_ref[half:2 * half, :], preferred_element_type=_F32))
    x1 = x_ref[...] + gate_ref[0] * mix
    x1_ref[...] = x1
    h2 = _rms(x1) * g_ref[...]
    h2 = h2 * (1.0 + scale_ref[0]) + shift_ref[0]
    h2_ref[...] = h2.astype(_BF16)
    lg_ref[...] = jnp.dot(h2, wr_ref[...], preferred_element_type=_F32,
                          precision=lax.Precision.HIGHEST) + br_ref[...]


def _outproj(attn, conv, x2d, gate, shift, scale, g_ffn, w_o_bf, w_router_pad, b_router_pad, *, tm,
             tiles_per_batch, per_row):
    r, d = x2d.shape
    half = attn.shape[1]
    mod_rows = gate.shape[1]
    mod_map = (lambda i: (i, 0, 0)) if per_row else (lambda i: (i // tiles_per_batch, 0, 0))
    mod_spec = lambda: pl.BlockSpec((1, mod_rows, d), mod_map)
    kern = functools.partial(_outproj_kernel, half=half)
    return pl.pallas_call(
        kern,
        grid=(r // tm,),
        in_specs=[pl.BlockSpec((tm, half), lambda i: (i, 0)),
                  pl.BlockSpec((tm, half), lambda i: (i, 0)),
                  pl.BlockSpec((tm, d), lambda i: (i, 0)),
                  mod_spec(), mod_spec(), mod_spec(),
                  pl.BlockSpec((1, d), lambda i: (0, 0)),
                  pl.BlockSpec((2 * half, d), lambda i: (0, 0), pipeline_mode=pl.Buffered(1)),
                  pl.BlockSpec((d, ROUTER_LANES), lambda i: (0, 0)),
                  pl.BlockSpec((1, ROUTER_LANES), lambda i: (0, 0))],
        out_specs=[pl.BlockSpec((tm, d), lambda i: (i, 0)),
                   pl.BlockSpec((tm, d), lambda i: (i, 0)),
                   pl.BlockSpec((tm, ROUTER_LANES), lambda i: (i, 0))],
        out_shape=[jax.ShapeDtypeStruct((r, d), _F32),
                   jax.ShapeDtypeStruct((r, d), _BF16),
                   jax.ShapeDtypeStruct((r, ROUTER_LANES), _F32)],
        compiler_params=_params(("arbitrary",)),
        name="outproj",
    )(attn, conv, x2d, gate, shift, scale, g_ffn.reshape(1, d), w_o_bf, w_router_pad, b_router_pad)


def _moe_kernel(be_ref, nv_ref, x_ref, wg_ref, wu_ref, bg_ref, bu_ref, wd_ref, bd_ref, o_ref, acc_scr):
    i = pl.program_id(0)
    j = pl.program_id(1)
    nj = pl.num_programs(1)
    valid = i < nv_ref[0]

    @pl.when(valid)
    def _():
        x = x_ref[...]
        g = jnp.dot(x, wg_ref[0], preferred_element_type=_F32) + bg_ref[0]
        u = jnp.dot(x, wu_ref[0], preferred_element_type=_F32) + bu_ref[0]
        g = jnp.minimum(g, SWIGLU_LIMIT)
        u = jnp.clip(u, -SWIGLU_LIMIT, SWIGLU_LIMIT)
        act = ((u + 1.0) * g * jax.nn.sigmoid(SWIGLU_ALPHA * g)).astype(_BF16)
        part = jnp.dot(act, wd_ref[0], preferred_element_type=_F32)

        @pl.when(j == 0)
        def _():
            acc_scr[...] = part + bd_ref[0]

        @pl.when(j > 0)
        def _():
            acc_scr[...] = acc_scr[...] + part

        @pl.when(j == nj - 1)
        def _():
            o_ref[...] = acc_scr[...]

    @pl.when(jnp.logical_not(valid))
    def _():
        o_ref[...] = jnp.zeros_like(o_ref)


def _moe(block_e, n_valid, xs, w_gu_bf, b_gu, w_down_bf, b_down, *, tm, tf):
    rows, d = xs.shape
    n_exp, _, two_ff = w_gu_bf.shape
    d_ff = two_ff // 2
    n_ff = d_ff // tf
    n_blocks = rows // tm

    def ff_idx(i, j, nv):
        return jnp.where(i < nv[0], j, n_ff - 1)

    grid_spec = pltpu.PrefetchScalarGridSpec(
        num_scalar_prefetch=2,
        grid=(n_blocks, n_ff),
        in_specs=[pl.BlockSpec((tm, d), lambda i, j, be, nv: (i, 0)),
                  pl.BlockSpec((1, d, tf), lambda i, j, be, nv: (be[i], 0, ff_idx(i, j, nv))),
                  pl.BlockSpec((1, d, tf), lambda i, j, be, nv: (be[i], 0, n_ff + ff_idx(i, j, nv))),
                  pl.BlockSpec((1, 1, tf), lambda i, j, be, nv: (be[i], 0, ff_idx(i, j, nv))),
                  pl.BlockSpec((1, 1, tf), lambda i, j, be, nv: (be[i], 0, n_ff + ff_idx(i, j, nv))),
                  pl.BlockSpec((1, tf, d), lambda i, j, be, nv: (be[i], ff_idx(i, j, nv), 0)),
                  pl.BlockSpec((1, 1, d), lambda i, j, be, nv: (be[i], 0, 0))],
        out_specs=pl.BlockSpec((tm, d), lambda i, j, be, nv: (i, 0)),
        scratch_shapes=[pltpu.VMEM((tm, d), _F32)],
    )
    b_gu3 = b_gu.reshape(n_exp, 1, two_ff)
    return pl.pallas_call(
        _moe_kernel,
        grid_spec=grid_spec,
        out_shape=jax.ShapeDtypeStruct((rows, d), _F32),
        compiler_params=_params(("arbitrary", "arbitrary")),
        name="moe",
    )(block_e, n_valid, xs, w_gu_bf, w_gu_bf, b_gu3, b_gu3, w_down_bf, b_down.reshape(n_exp, 1, d))


def _combine_kernel(x1_ref, yk_ref, gk_ref, gate_ref, g_ref, y_ref):
    d = x1_ref.shape[1]
    gk = gk_ref[...]
    ff = gk[:, 0:1] * yk_ref[:, 0:d]
    for k in range(1, TOP_K):
        ff = ff + gk[:, k:k + 1] * yk_ref[:, k * d:(k + 1) * d]
    x2 = x1_ref[...] + gate_ref[0] * ff
    y_ref[...] = _rms(x2) * g_ref[...]


def _combine(x1, yk, gk, gate, g_final, *, tm, tiles_per_batch, per_row):
    r, d = x1.shape
    mod_rows = gate.shape[1]
    mod_map = (lambda i: (i, 0, 0)) if per_row else (lambda i: (i // tiles_per_batch, 0, 0))
    return pl.pallas_call(
        _combine_kernel,
        grid=(r // tm,),
        in_specs=[pl.BlockSpec((tm, d), lambda i: (i, 0)),
                  pl.BlockSpec((tm, TOP_K * d), lambda i: (i, 0)),
                  pl.BlockSpec((tm, TOP_K), lambda i: (i, 0)),
                  pl.BlockSpec((1, mod_rows, d), mod_map),
                  pl.BlockSpec((1, d), lambda i: (0, 0))],
        out_specs=pl.BlockSpec((tm, d), lambda i: (i, 0)),
        out_shape=jax.ShapeDtypeStruct((r, d), _F32),
        compiler_params=_params(("arbitrary",)),
        name="combine",
    )(x1, yk, gk, gate, g_final.reshape(1, d))


def _route(logits, n_exp, tm):
    t = logits.shape[0]
    top_v, top_e = lax.top_k(logits, TOP_K)
    gate = jax.nn.softmax(top_v, axis=-1)
    m = t * TOP_K
    n_blocks = -(-m // tm) + n_exp
    flat_e = top_e.reshape(-1).astype(jnp.int32)
    onehot = (flat_e[:, None] == jnp.arange(n_exp, dtype=jnp.int32)[None, :]).astype(jnp.int32)
    rank = jnp.take_along_axis(jnp.cumsum(onehot, axis=0), flat_e[:, None], axis=1)[:, 0] - 1
    counts = jnp.sum(onehot, axis=0)
    padded = ((counts + tm - 1) // tm) * tm
    cum_end = jnp.cumsum(padded)
    pstart = cum_end - padded
    dest = (pstart[flat_e] + rank).astype(jnp.int32)
    buf_tok = jnp.zeros((n_blocks * tm,), jnp.int32).at[dest].set(jnp.arange(m, dtype=jnp.int32) // TOP_K)
    n_valid = (cum_end[-1] // tm).astype(jnp.int32).reshape(1)
    blk_start = jnp.arange(n_blocks, dtype=jnp.int32) * tm
    block_e = jnp.minimum(jnp.searchsorted(cum_end, blk_start, side='right'), n_exp - 1).astype(jnp.int32)
    last_e = block_e[jnp.maximum(n_valid[0] - 1, 0)]
    block_e = jnp.where(jnp.arange(n_blocks) < n_valid[0], block_e, last_e)
    return gate, dest.reshape(t, TOP_K), buf_tok, block_e, n_valid


def _pick_tile(n, pref):
    t = min(n, pref)
    while n % t:
        t //= 2
    return t


def kernel(x_prompt, x_sample, cache_k, cache_v, state_conv, c_prompt, c_sample, g_mix, g_ffn, w_ada, b_ada, w_in, lambda_q1, lambda_k1, lambda_q2, lambda_k2, subln_g, conv_w, w_o, w_router, b_router, w_gu, b_gu, w_down, b_down, g_final):
    depth = g_mix.shape[0]
    assert depth == 1
    bp, sp, d = x_prompt.shape
    bs, ts, _ = x_sample.shape
    past = cache_k.shape[2]
    aw = d // 2
    hw = aw // N_HEADS
    hd = hw // 2
    n_exp = w_router.shape[-1]
    layer = 0

    n_c = bp + bs
    c_rows = -(-n_c // 8) * 8
    c_all = jnp.concatenate([c_prompt, c_sample, jnp.zeros((c_rows - n_c, d), _F32)], axis=0)
    ada = _ada(c_all, w_ada[layer], b_ada[layer])
    shift1, scale1, gate1, shift2, scale2, gate2 = [ada[:, i * d:(i + 1) * d] for i in range(6)]

    lam_init = 0.8 - 0.6 * math.exp(-0.3 * layer)
    lam = (jnp.exp(jnp.sum(lambda_q1[layer] * lambda_k1[layer]))
           - jnp.exp(jnp.sum(lambda_q2[layer] * lambda_k2[layer])) + lam_init)
    slopes = jnp.asarray(2.0 ** (-8.0 * np.arange(1, N_HEADS + 1) / N_HEADS), _F32)
    scal = jnp.concatenate([lam.reshape(1).astype(_F32), slopes])
    out_scale = 1.0 - lam_init
    q_scale = hd ** -0.5

    w_in_bf = w_in[layer].astype(_BF16)
    w_o_bf = w_o[layer].astype(_BF16)
    w_gu_bf = w_gu[layer].astype(_BF16)
    w_down_bf = w_down[layer].astype(_BF16)
    w_router_pad = jnp.zeros((d, ROUTER_LANES), _F32).at[:, :n_exp].set(w_router[layer])
    b_router_pad = jnp.zeros((1, ROUTER_LANES), _F32).at[0, :n_exp].set(b_router[layer])

    rp = bp * sp
    tm_p = _pick_tile(sp, 512)
    tpb = sp // tm_p
    xp2 = x_prompt.reshape(rp, d)
    mod_p = lambda a: a[:bp].reshape(bp, 1, d)
    qp, kp, vp, kbp, vtp, cop, csp = _inproj(
        xp2, mod_p(shift1), mod_p(scale1), g_mix[layer], w_in_bf, conv_w[layer],
        jnp.zeros((1, 2, aw), _F32), tm=tm_p, seg=tm_p, tiles_per_batch=tpb, use_state=False,
        q_scale=q_scale)
    tq = _pick_tile(sp, 512)
    ap = _attn_prompt(scal, qp, kbp, vtp, subln_g[layer].reshape(hw, 1), batch=bp, seq=sp, tq=tq,
                      out_scale=out_scale)
    x1p, h2p, lgp = _outproj(ap, cop, xp2, mod_p(gate1), mod_p(shift2), mod_p(scale2), g_ffn[layer],
                             w_o_bf, w_router_pad, b_router_pad, tm=tm_p, tiles_per_batch=tpb,
                             per_row=False)

    rs = bs * ts
    mod_s = lambda a: jnp.repeat(a[bp:bp + bs], ts, axis=0).reshape(1, rs, d)
    xs2 = x_sample.reshape(rs, d)
    qs, ks, vs, _, _, cos, css = _inproj(
        xs2, mod_s(shift1), mod_s(scale1), g_mix[layer], w_in_bf, conv_w[layer],
        state_conv[layer], tm=rs, seg=ts, tiles_per_batch=1, use_state=True, q_scale=q_scale)
    a_s = _attn_sample(scal, qs, ks, vs, cache_k[layer].reshape(bs, past, aw),
                       cache_v[layer].reshape(bs, past, aw), subln_g[layer].reshape(1, hw),
                       batch=bs, t=ts, out_scale=out_scale)
    x1s, h2s, lgs = _outproj(a_s, cos, xs2, mod_s(gate1), mod_s(shift2), mod_s(scale2), g_ffn[layer],
                             w_o_bf, w_router_pad, b_router_pad, tm=rs, tiles_per_batch=1, per_row=True)

    tm_e = 512
    h2_all = jnp.concatenate([h2p, h2s], axis=0)
    logits = jnp.concatenate([lgp, lgs], axis=0)[:, :n_exp]
    gate, dest, buf_tok, block_e, n_valid = _route(logits, n_exp, tm_e)
    xs_sorted = jnp.take(h2_all, buf_tok, axis=0)
    d_ff = w_down_bf.shape[1]
    ys = _moe(block_e, n_valid, xs_sorted, w_gu_bf, b_gu[layer], w_down_bf, b_down[layer],
              tm=tm_e, tf=_pick_tile(d_ff, 1024))
    yk = jnp.take(ys, dest.reshape(-1), axis=0).reshape(rp + rs, TOP_K * d)

    tm_c = _pick_tile(sp, 256)
    y_p = _combine(x1p, yk[:rp], gate[:rp], mod_p(gate2), g_final, tm=tm_c, tiles_per_batch=sp // tm_c,
                   per_row=False)
    y_s = _combine(x1s, yk[rp:], gate[rp:], mod_s(gate2), g_final, tm=rs, tiles_per_batch=1, per_row=True)

    return (y_p.reshape(bp, sp, d), y_s.reshape(bs, ts, d),
            kp.reshape(1, bp, sp, N_HEADS, 2, hd), vp.reshape(1, bp, sp, N_HEADS, hw),
            csp.reshape(1, bp, 2, aw),
            ks.reshape(1, bs, ts, N_HEADS, 2, hd), vs.reshape(1, bs, ts, N_HEADS, hw),
            css.reshape(1, bs, 2, aw))
```

```python
import functools
import math

import jax
import jax.numpy as jnp
import numpy as np
from jax import lax
from jax.experimental import pallas as pl
from jax.experimental.pallas import tpu as pltpu

_F32 = jnp.float32
_BF16 = jnp.bfloat16

CHUNK = 64
N_HEADS = 8
TOP_K = 4
NORM_EPS = 1e-6
SWIGLU_ALPHA = 1.702
SWIGLU_LIMIT = 7.0
MASKED_SCORE = -1e30
ROUTER_LANES = 128

V7X_VMEM_LIMIT_BYTES = 56 * 1024 * 1024


def _params(semantics, vmem_bytes=V7X_VMEM_LIMIT_BYTES):
    return pltpu.CompilerParams(dimension_semantics=semantics, vmem_limit_bytes=vmem_bytes)


def _rms(x):
    return x * lax.rsqrt(jnp.mean(x * x, axis=-1, keepdims=True) + NORM_EPS)


def _ada_kernel(c_ref, w_ref, b_ref, o_ref):
    c = c_ref[...]
    s = (c * jax.nn.sigmoid(c)).astype(_BF16)
    o_ref[...] = jnp.dot(s, w_ref[...].astype(_BF16), preferred_element_type=_F32) + b_ref[...]


def _ada(c_all, w_ada, b_ada, tn=1024):
    rows, d = c_all.shape
    n = w_ada.shape[1]
    return pl.pallas_call(
        _ada_kernel,
        grid=(n // tn,),
        in_specs=[pl.BlockSpec((rows, d), lambda j: (0, 0)),
                  pl.BlockSpec((d, tn), lambda j: (0, j)),
                  pl.BlockSpec((1, tn), lambda j: (0, j))],
        out_specs=pl.BlockSpec((rows, tn), lambda j: (0, j)),
        out_shape=jax.ShapeDtypeStruct((rows, n), _F32),
        compiler_params=_params(("arbitrary",)),
        name="ada",
    )(c_all, w_ada, b_ada.reshape(1, n))


def _inproj_kernel(x_ref, shift_ref, scale_ref, g_ref, w_ref, cw_ref, st_ref,
                   q_ref, k_ref, v_ref, kb_ref, vt_ref, co_ref, cs_ref,
                   ug_scr, carry_scr, *, width, seg, tiles_per_batch, use_state, q_scale):
    i = pl.program_id(0)
    tm = x_ref.shape[0]
    h = _rms(x_ref[...]) * g_ref[...]
    h = (h * (1.0 + scale_ref[0]) + shift_ref[0]).astype(_BF16)

    def proj(j):
        return jnp.dot(h, w_ref[:, j * width:(j + 1) * width], preferred_element_type=_F32)

    q_ref[...] = (proj(0) * q_scale).astype(_BF16)
    zk = proj(1)
    k_ref[...] = zk
    kb_ref[...] = zk.astype(_BF16)
    zv = proj(2)
    v_ref[...] = zv
    vt_ref[...] = zv.T.astype(_BF16)

    ug = proj(3) * proj(5)
    gb = proj(4)
    w0 = cw_ref[0:1, :]
    w1 = cw_ref[1:2, :]
    w2 = cw_ref[2:3, :]
    if not use_state:
        @pl.when((i % tiles_per_batch) == 0)
        def _():
            carry_scr[...] = jnp.zeros_like(carry_scr)

    for s in range(tm // seg):
        lo = s * seg
        bnd = st_ref[s] if use_state else carry_scr[...]
        ug_s = ug[lo:lo + seg]
        ug_scr[6:8, :] = bnd
        ug_scr[8:8 + seg, :] = ug_s
        y = w2 * ug_s + w1 * ug_scr[7:7 + seg, :] + w0 * ug_scr[6:6 + seg, :]
        co_ref[lo:lo + seg, :] = (gb[lo:lo + seg] * y).astype(_BF16)
        last2 = ug_scr[6 + seg:8 + seg, :]
        cs_ref[s] = last2
        carry_scr[...] = last2


def _inproj(x2d, shift, scale, g_mix, w_in_bf, conv_w, state, *, tm, seg, tiles_per_batch,
            use_state, q_scale):
    r, d = x2d.shape
    width = w_in_bf.shape[1] // 6
    n_tiles = r // tm
    n_seg = tm // seg
    mod_rows = shift.shape[1]
    if use_state:
        mod_map = lambda i: (i, 0, 0)
        st_spec = pl.BlockSpec((n_seg, 2, width), lambda i: (i, 0, 0))
        cs_spec = pl.BlockSpec((n_seg, 2, width), lambda i: (i, 0, 0))
        n_state = n_tiles * n_seg
    else:
        mod_map = lambda i: (i // tiles_per_batch, 0, 0)
        st_spec = pl.BlockSpec((1, 2, width), lambda i: (0, 0, 0))
        cs_spec = pl.BlockSpec((1, 2, width), lambda i: (i // tiles_per_batch, 0, 0))
        n_state = n_tiles // tiles_per_batch
    kern = functools.partial(_inproj_kernel, width=width, seg=seg, tiles_per_batch=tiles_per_batch,
                             use_state=use_state, q_scale=q_scale)
    row_spec = lambda: pl.BlockSpec((tm, width), lambda i: (i, 0))
    return pl.pallas_call(
        kern,
        grid=(n_tiles,),
        in_specs=[pl.BlockSpec((tm, d), lambda i: (i, 0)),
                  pl.BlockSpec((1, mod_rows, d), mod_map),
                  pl.BlockSpec((1, mod_rows, d), mod_map),
                  pl.BlockSpec((1, d), lambda i: (0, 0)),
                  pl.BlockSpec((d, 6 * width), lambda i: (0, 0), pipeline_mode=pl.Buffered(1)),
                  pl.BlockSpec((3, width), lambda i: (0, 0)),
                  st_spec],
        out_specs=[row_spec(), row_spec(), row_spec(), row_spec(),
                   pl.BlockSpec((width, tm), lambda i: (0, i)),
                   row_spec(), cs_spec],
        out_shape=[jax.ShapeDtypeStruct((r, width), _BF16),
                   jax.ShapeDtypeStruct((r, width), _F32),
                   jax.ShapeDtypeStruct((r, width), _F32),
                   jax.ShapeDtypeStruct((r, width), _BF16),
                   jax.ShapeDtypeStruct((width, r), _BF16),
                   jax.ShapeDtypeStruct((r, width), _BF16),
                   jax.ShapeDtypeStruct((n_state, 2, width), _F32)],
        scratch_shapes=[pltpu.VMEM((seg + 8, width), _F32), pltpu.VMEM((2, width), _F32)],
        compiler_params=_params(("arbitrary",)),
        name="inproj",
    )(x2d, shift, scale, g_mix.reshape(1, d), w_in_bf, conv_w, state)


def _attn_prompt_kernel(scal_ref, q_ref, k_ref, vt_ref, g_ref, o_ref, acc_scr, *, tq, hd, out_scale):
    hh = pl.program_id(1)
    qi = pl.program_id(2)
    lam = scal_ref[0]
    slope = scal_ref[1 + hh]
    tk = tq

    qt = q_ref[...].astype(_F32).T.astype(_BF16)
    row = lax.broadcasted_iota(jnp.int32, (2 * hd, 2 * tq), 0)
    col = lax.broadcasted_iota(jnp.int32, (2 * hd, 2 * tq), 1)
    rhs = jnp.where((row < hd) == (col < tq), jnp.concatenate([qt, qt], axis=1), 0).astype(_BF16)

    jj = lax.broadcasted_iota(jnp.int32, (tk, tq), 0)
    ii = lax.broadcasted_iota(jnp.int32, (tk, tq), 1)
    jf = jj.astype(_F32)
    iif = ii.astype(_F32)
    off_bias = slope * jf
    visible = (jj // CHUNK) <= (ii // CHUNK)
    diag_bias = jnp.where(visible, slope * iif - slope * jnp.abs(iif - jf), MASKED_SCORE)
    off_bias2 = jnp.concatenate([off_bias, off_bias], axis=1)
    diag_bias2 = jnp.concatenate([diag_bias, diag_bias], axis=1)

    def block(kv, bias2, c):
        start = pl.multiple_of(kv * tk, tk)
        k = k_ref[pl.ds(start, tk), :]
        s = jnp.dot(k, rhs, preferred_element_type=_F32) + bias2
        vt = vt_ref[:, pl.ds(start, tk)]
        return s, vt

    s, vt = block(qi, diag_bias2, 0.0)
    m0 = jnp.max(s, axis=0, keepdims=True)
    p = jnp.exp(s - m0)
    l0 = jnp.sum(p, axis=0, keepdims=True)
    acc_scr[...] = jnp.dot(vt, p.astype(_BF16), preferred_element_type=_F32)

    def body(kv, carry):
        m, l = carry
        c = -slope * ((qi - kv) * tq).astype(_F32)
        s, vt = block(kv, off_bias2, c)
        m_new = jnp.maximum(m, jnp.max(s, axis=0, keepdims=True) + c)
        p = jnp.exp(s - (m_new - c))
        alpha = jnp.exp(m - m_new)
        l = alpha * l + jnp.sum(p, axis=0, keepdims=True)
        acc_scr[...] = acc_scr[...] * alpha + jnp.dot(vt, p.astype(_BF16), preferred_element_type=_F32)
        return m_new, l

    _, l = lax.fori_loop(0, qi, body, (m0, l0))

    acc = acc_scr[...] / l
    o = acc[:, :tq] - lam * acc[:, tq:]
    o = o * lax.rsqrt(jnp.mean(o * o, axis=0, keepdims=True) + NORM_EPS)
    o = o * g_ref[...] * out_scale
    o_ref[...] = o.T.astype(_BF16)


def _attn_prompt(scal, q, kb, vt, subln_col, *, batch, seq, tq, out_scale):
    width = q.shape[1]
    hw = width // N_HEADS
    nq = seq // tq
    kern = functools.partial(_attn_prompt_kernel, tq=tq, hd=hw // 2, out_scale=out_scale)
    grid_spec = pltpu.PrefetchScalarGridSpec(
        num_scalar_prefetch=0,
        grid=(batch, N_HEADS, nq),
        in_specs=[pl.BlockSpec(memory_space=pltpu.SMEM),
                  pl.BlockSpec((tq, hw), lambda b, h, i: (b * nq + i, h)),
                  pl.BlockSpec((seq, hw), lambda b, h, i: (b, h)),
                  pl.BlockSpec((hw, seq), lambda b, h, i: (h, b)),
                  pl.BlockSpec((hw, 1), lambda b, h, i: (0, 0))],
        out_specs=pl.BlockSpec((tq, hw), lambda b, h, i: (b * nq + i, h)),
        scratch_shapes=[pltpu.VMEM((hw, 2 * tq), _F32)],
    )
    return pl.pallas_call(
        kern,
        grid_spec=grid_spec,
        out_shape=jax.ShapeDtypeStruct((batch * seq, width), _BF16),
        compiler_params=_params(("arbitrary", "arbitrary", "arbitrary")),
        name="attn_prompt",
    )(scal, q, kb, vt, subln_col)


def _attn_sample_kernel(scal_ref, q_ref, kn_ref, vn_ref, kc_ref, vc_ref, g_ref, o_ref, *, hd, out_scale):
    hh = pl.program_id(1)
    lam = scal_ref[0]
    slope = scal_ref[1 + hh]
    t = q_ref.shape[0]
    past = kc_ref.shape[1]
    q = q_ref[...]
    kc = kc_ref[0].astype(_BF16)
    kn = kn_ref[...].astype(_BF16)
    vc = vc_ref[0].astype(_BF16)
    vn = vn_ref[...].astype(_BF16)
    qpos = lax.broadcasted_iota(jnp.int32, (t, past), 0).astype(_F32) + float(past)
    kpos = lax.broadcasted_iota(jnp.int32, (t, past), 1).astype(_F32)
    bias_c = -slope * jnp.abs(qpos - kpos)
    tq_i = lax.broadcasted_iota(jnp.int32, (t, t), 0).astype(_F32)
    tk_i = lax.broadcasted_iota(jnp.int32, (t, t), 1).astype(_F32)
    bias_n = -slope * jnp.abs(tq_i - tk_i)
    nt = (((1,), (1,)), ((), ()))
    outs = []
    for j in range(2):
        sl = slice(j * hd, (j + 1) * hd)
        sc = lax.dot_general(q[:, sl], kc[:, sl], nt, preferred_element_type=_F32) + bias_c
        sn = lax.dot_general(q[:, sl], kn[:, sl], nt, preferred_element_type=_F32) + bias_n
        m = jnp.maximum(jnp.max(sc, axis=-1, keepdims=True), jnp.max(sn, axis=-1, keepdims=True))
        pc = jnp.exp(sc - m)
        pn = jnp.exp(sn - m)
        l = jnp.sum(pc, axis=-1, keepdims=True) + jnp.sum(pn, axis=-1, keepdims=True)
        o = (jnp.dot(pc.astype(_BF16), vc, preferred_element_type=_F32)
             + jnp.dot(pn.astype(_BF16), vn, preferred_element_type=_F32))
        outs.append(o / l)
    o = outs[0] - lam * outs[1]
    o = _rms(o) * g_ref[...] * out_scale
    o_ref[...] = o.astype(_BF16)


def _attn_sample(scal, q, k_new, v_new, cache_k, cache_v, subln_row, *, batch, t, out_scale):
    width = q.shape[1]
    hw = width // N_HEADS
    past = cache_k.shape[1]
    kern = functools.partial(_attn_sample_kernel, hd=hw // 2, out_scale=out_scale)
    new_spec = lambda: pl.BlockSpec((t, hw), lambda b, h: (b, h))
    cache_spec = lambda: pl.BlockSpec((1, past, hw), lambda b, h: (b, 0, h))
    return pl.pallas_call(
        kern,
        grid=(batch, N_HEADS),
        in_specs=[pl.BlockSpec(memory_space=pltpu.SMEM), new_spec(), new_spec(), new_spec(),
                  cache_spec(), cache_spec(), pl.BlockSpec((1, hw), lambda b, h: (0, 0))],
        out_specs=new_spec(),
        out_shape=jax.ShapeDtypeStruct((batch * t, width), _BF16),
        compiler_params=_params(("arbitrary", "arbitrary")),
        name="attn_sample",
    )(scal, q, k_new, v_new, cache_k, cache_v, subln_row)


def _outproj_kernel(a_ref, c_ref, x_ref, gate_ref, shift_ref, scale_ref, g_ref, wo_ref, wr_ref, br_ref,
                    x1_ref, h2_ref, lg_ref, *, half):
    mix = (jnp.dot(a_ref[...], wo_ref[0:half, :], preferred_element_type=_F32)
           + jnp.dot(c_ref[...], wo_ref[half:2 * half, :], preferred_element_type=_F32))
    x1 = x_ref[...] + gate_ref[0] * mix
    x1_ref[...] = x1
    h2 = _rms(x1) * g_ref[...]
    h2 = h2 * (1.0 + scale_ref[0]) + shift_ref[0]
    h2_ref[...] = h2.astype(_BF16)
    lg_ref[...] = jnp.dot(h2, wr_ref[...], preferred_element_type=_F32,
                          precision=lax.Precision.HIGHEST) + br_ref[...]


def _outproj(attn, conv, x2d, gate, shift, scale, g_ffn, w_o_bf, w_router_pad, b_router_pad, *, tm,
             tiles_per_batch, per_row):
    r, d = x2d.shape
    half = attn.shape[1]
    mod_rows = gate.shape[1]
    mod_map = (lambda i: (i, 0, 0)) if per_row else (lambda i: (i // tiles_per_batch, 0, 0))
    mod_spec = lambda: pl.BlockSpec((1, mod_rows, d), mod_map)
    kern = functools.partial(_outproj_kernel, half=half)
    return pl.pallas_call(
        kern,
        grid=(r // tm,),
        in_specs=[pl.BlockSpec((tm, half), lambda i: (i, 0)),
                  pl.BlockSpec((tm, half), lambda i: (i, 0)),
                  pl.BlockSpec((tm, d), lambda i: (i, 0)),
                  mod_spec(), mod_spec(), mod_spec(),
                  pl.BlockSpec((1, d), lambda i: (0, 0)),
                  pl.BlockSpec((2 * half, d), lambda i: (0, 0), pipeline_mode=pl.Buffered(1)),
                  pl.BlockSpec((d, ROUTER_LANES), lambda i: (0, 0)),
                  pl.BlockSpec((1, ROUTER_LANES), lambda i: (0, 0))],
        out_specs=[pl.BlockSpec((tm, d), lambda i: (i, 0)),
                   pl.BlockSpec((tm, d), lambda i: (i, 0)),
                   pl.BlockSpec((tm, ROUTER_LANES), lambda i: (i, 0))],
        out_shape=[jax.ShapeDtypeStruct((r, d), _F32),
                   jax.ShapeDtypeStruct((r, d), _BF16),
                   jax.ShapeDtypeStruct((r, ROUTER_LANES), _F32)],
        compiler_params=_params(("arbitrary",)),
        name="outproj",
    )(attn, conv, x2d, gate, shift, scale, g_ffn.reshape(1, d), w_o_bf, w_router_pad, b_router_pad)


def _moe_kernel(be_ref, nv_ref, x_ref, wg_ref, wu_ref, bg_ref, bu_ref, wd_ref, bd_ref, o_ref):
    i = pl.program_id(0)
    j = pl.program_id(1)
    valid = i < nv_ref[0]

    @pl.when(valid)
    def _():
        x = x_ref[...]
        g = jnp.dot(x, wg_ref[0].astype(_BF16), preferred_element_type=_F32) + bg_ref[0]
        u = jnp.dot(x, wu_ref[0].astype(_BF16), preferred_element_type=_F32) + bu_ref[0]
        g = jnp.minimum(g, SWIGLU_LIMIT)
        u = jnp.clip(u, -SWIGLU_LIMIT, SWIGLU_LIMIT)
        act = ((u + 1.0) * g * jax.nn.sigmoid(SWIGLU_ALPHA * g)).astype(_BF16)
        part = jnp.dot(act, wd_ref[0].astype(_BF16), preferred_element_type=_F32)

        @pl.when(j == 0)
        def _():
            o_ref[...] = part + bd_ref[0]

        @pl.when(j > 0)
        def _():
            o_ref[...] = o_ref[...] + part

    @pl.when(jnp.logical_not(valid))
    def _():
        o_ref[...] = jnp.zeros_like(o_ref)


def _moe(block_e, n_valid, xs, w_gu, b_gu, w_down, b_down, *, tm, tf):
    rows, d = xs.shape
    n_exp, _, two_ff = w_gu.shape
    d_ff = two_ff // 2
    n_ff = d_ff // tf
    n_blocks = rows // tm

    def ff_idx(i, j, nv):
        return jnp.where(i < nv[0], j, n_ff - 1)

    grid_spec = pltpu.PrefetchScalarGridSpec(
        num_scalar_prefetch=2,
        grid=(n_blocks, n_ff),
        in_specs=[pl.BlockSpec((tm, d), lambda i, j, be, nv: (i, 0)),
                  pl.BlockSpec((1, d, tf), lambda i, j, be, nv: (be[i], 0, ff_idx(i, j, nv))),
                  pl.BlockSpec((1, d, tf), lambda i, j, be, nv: (be[i], 0, n_ff + ff_idx(i, j, nv))),
                  pl.BlockSpec((1, 1, tf), lambda i, j, be, nv: (be[i], 0, ff_idx(i, j, nv))),
                  pl.BlockSpec((1, 1, tf), lambda i, j, be, nv: (be[i], 0, n_ff + ff_idx(i, j, nv))),
                  pl.BlockSpec((1, tf, d), lambda i, j, be, nv: (be[i], ff_idx(i, j, nv), 0)),
                  pl.BlockSpec((1, 1, d), lambda i, j, be, nv: (be[i], 0, 0))],
        out_specs=pl.BlockSpec((tm, d), lambda i, j, be, nv: (i, 0)),
    )
    b_gu3 = b_gu.reshape(n_exp, 1, two_ff)
    return pl.pallas_call(
        _moe_kernel,
        grid_spec=grid_spec,
        out_shape=jax.ShapeDtypeStruct((rows, d), _F32),
        compiler_params=_params(("arbitrary", "arbitrary")),
        name="moe",
    )(block_e, n_valid, xs, w_gu, w_gu, b_gu3, b_gu3, w_down, b_down.reshape(n_exp, 1, d))


def _combine_kernel(x1_ref, y0_ref, y1_ref, y2_ref, y3_ref, gk_ref, gate_ref, g_ref, y_ref):
    gk = gk_ref[...]
    ff = gk[:, 0:1] * y0_ref[...]
    for k, yk_ref in enumerate((y1_ref, y2_ref, y3_ref), start=1):
        ff = ff + gk[:, k:k + 1] * yk_ref[...]
    x2 = x1_ref[...] + gate_ref[0] * ff
    y_ref[...] = _rms(x2) * g_ref[...]


def _combine(x1, yks, gk, gate, g_final, *, tm, tiles_per_batch, per_row):
    r, d = x1.shape
    mod_rows = gate.shape[1]
    mod_map = (lambda i: (i, 0, 0)) if per_row else (lambda i: (i // tiles_per_batch, 0, 0))
    return pl.pallas_call(
        _combine_kernel,
        grid=(r // tm,),
        in_specs=[pl.BlockSpec((tm, d), lambda i: (i, 0))] * (1 + TOP_K) + [
                  pl.BlockSpec((tm, TOP_K), lambda i: (i, 0)),
                  pl.BlockSpec((1, mod_rows, d), mod_map),
                  pl.BlockSpec((1, d), lambda i: (0, 0))],
        out_specs=pl.BlockSpec((tm, d), lambda i: (i, 0)),
        out_shape=jax.ShapeDtypeStruct((r, d), _F32),
        compiler_params=_params(("arbitrary",)),
        name="combine",
    )(x1, *yks, gk, gate, g_final.reshape(1, d))


def _route(logits, n_exp, tm):
    t = logits.shape[0]
    top_v, top_e = lax.top_k(logits, TOP_K)
    gate = jax.nn.softmax(top_v, axis=-1)
    m = t * TOP_K
    n_blocks = -(-m // tm) + n_exp
    flat_e = top_e.reshape(-1).astype(jnp.int32)
    onehot = (flat_e[:, None] == jnp.arange(n_exp, dtype=jnp.int32)[None, :]).astype(jnp.int32)
    rank = jnp.take_along_axis(jnp.cumsum(onehot, axis=0), flat_e[:, None], axis=1)[:, 0] - 1
    counts = jnp.sum(onehot, axis=0)
    padded = ((counts + tm - 1) // tm) * tm
    cum_end = jnp.cumsum(padded)
    pstart = cum_end - padded
    dest = (pstart[flat_e] + rank).astype(jnp.int32)
    buf_tok = jnp.zeros((n_blocks * tm,), jnp.int32).at[dest].set(jnp.arange(m, dtype=jnp.int32) // TOP_K)
    n_valid = (cum_end[-1] // tm).astype(jnp.int32).reshape(1)
    blk_start = jnp.arange(n_blocks, dtype=jnp.int32) * tm
    block_e = jnp.minimum(jnp.searchsorted(cum_end, blk_start, side='right'), n_exp - 1).astype(jnp.int32)
    last_e = block_e[jnp.maximum(n_valid[0] - 1, 0)]
    block_e = jnp.where(jnp.arange(n_blocks) < n_valid[0], block_e, last_e)
    return gate, dest.reshape(t, TOP_K), buf_tok, block_e, n_valid


def _pick_tile(n, pref):
    t = min(n, pref)
    while n % t:
        t //= 2
    return t


def kernel(x_prompt, x_sample, cache_k, cache_v, state_conv, c_prompt, c_sample, g_mix, g_ffn, w_ada, b_ada, w_in, lambda_q1, lambda_k1, lambda_q2, lambda_k2, subln_g, conv_w, w_o, w_router, b_router, w_gu, b_gu, w_down, b_down, g_final):
    depth = g_mix.shape[0]
    assert depth == 1
    bp, sp, d = x_prompt.shape
    bs, ts, _ = x_sample.shape
    past = cache_k.shape[2]
    aw = d // 2
    hw = aw // N_HEADS
    hd = hw // 2
    n_exp = w_router.shape[-1]
    layer = 0

    n_c = bp + bs
    c_rows = -(-n_c // 8) * 8
    c_all = jnp.concatenate([c_prompt, c_sample, jnp.zeros((c_rows - n_c, d), _F32)], axis=0)
    ada = _ada(c_all, w_ada[layer], b_ada[layer])
    shift1, scale1, gate1, shift2, scale2, gate2 = [ada[:, i * d:(i + 1) * d] for i in range(6)]

    lam_init = 0.8 - 0.6 * math.exp(-0.3 * layer)
    lam = (jnp.exp(jnp.sum(lambda_q1[layer] * lambda_k1[layer]))
           - jnp.exp(jnp.sum(lambda_q2[layer] * lambda_k2[layer])) + lam_init)
    slopes = jnp.asarray(2.0 ** (-8.0 * np.arange(1, N_HEADS + 1) / N_HEADS), _F32)
    scal = jnp.concatenate([lam.reshape(1).astype(_F32), slopes])
    out_scale = 1.0 - lam_init
    q_scale = hd ** -0.5

    w_in_bf = w_in[layer].astype(_BF16)
    w_o_bf = w_o[layer].astype(_BF16)
    w_router_pad = jnp.zeros((d, ROUTER_LANES), _F32).at[:, :n_exp].set(w_router[layer])
    b_router_pad = jnp.zeros((1, ROUTER_LANES), _F32).at[0, :n_exp].set(b_router[layer])

    rp = bp * sp
    tm_p = _pick_tile(sp, 512)
    tpb = sp // tm_p
    xp2 = x_prompt.reshape(rp, d)
    mod_p = lambda a: a[:bp].reshape(bp, 1, d)
    qp, kp, vp, kbp, vtp, cop, csp = _inproj(
        xp2, mod_p(shift1), mod_p(scale1), g_mix[layer], w_in_bf, conv_w[layer],
        jnp.zeros((1, 2, aw), _F32), tm=tm_p, seg=tm_p, tiles_per_batch=tpb, use_state=False,
        q_scale=q_scale)
    tq = _pick_tile(sp, 512)
    ap = _attn_prompt(scal, qp, kbp, vtp, subln_g[layer].reshape(hw, 1), batch=bp, seq=sp, tq=tq,
                      out_scale=out_scale)
    x1p, h2p, lgp = _outproj(ap, cop, xp2, mod_p(gate1), mod_p(shift2), mod_p(scale2), g_ffn[layer],
                             w_o_bf, w_router_pad, b_router_pad, tm=tm_p, tiles_per_batch=tpb,
                             per_row=False)

    rs = bs * ts
    mod_s = lambda a: jnp.repeat(a[bp:bp + bs], ts, axis=0).reshape(1, rs, d)
    xs2 = x_sample.reshape(rs, d)
    qs, ks, vs, _, _, cos, css = _inproj(
        xs2, mod_s(shift1), mod_s(scale1), g_mix[layer], w_in_bf, conv_w[layer],
        state_conv[layer], tm=rs, seg=ts, tiles_per_batch=1, use_state=True, q_scale=q_scale)
    a_s = _attn_sample(scal, qs, ks, vs, cache_k[layer].reshape(bs, past, aw),
                       cache_v[layer].reshape(bs, past, aw), subln_g[layer].reshape(1, hw),
                       batch=bs, t=ts, out_scale=out_scale)
    x1s, h2s, lgs = _outproj(a_s, cos, xs2, mod_s(gate1), mod_s(shift2), mod_s(scale2), g_ffn[layer],
                             w_o_bf, w_router_pad, b_router_pad, tm=rs, tiles_per_batch=1, per_row=True)

    tm_e = 512
    h2_all = jnp.concatenate([h2p, h2s], axis=0)
    logits = jnp.concatenate([lgp, lgs], axis=0)[:, :n_exp]
    gate, dest, buf_tok, block_e, n_valid = _route(logits, n_exp, tm_e)
    take_rows = lambda a, idx: a.at[idx].get(mode="promise_in_bounds")
    xs_sorted = take_rows(h2_all, buf_tok)
    d_ff = w_down.shape[2]
    ys = _moe(block_e, n_valid, xs_sorted, w_gu[layer], b_gu[layer], w_down[layer], b_down[layer],
              tm=tm_e, tf=_pick_tile(d_ff, 512))
    yk_p = [take_rows(ys, dest[:rp, k]) for k in range(TOP_K)]
    yk_s = [take_rows(ys, dest[rp:, k]) for k in range(TOP_K)]

    tm_c = _pick_tile(sp, 256)
    y_p = _combine(x1p, yk_p, gate[:rp], mod_p(gate2), g_final, tm=tm_c, tiles_per_batch=sp // tm_c,
                   per_row=False)
    y_s = _combine(x1s, yk_s, gate[rp:], mod_s(gate2), g_final, tm=rs, tiles_per_batch=1, per_row=True)

    return (y_p.reshape(bp, sp, d), y_s.reshape(bs, ts, d),
            kp.reshape(1, bp, sp, N_HEADS, 2, hd), vp.reshape(1, bp, sp, N_HEADS, hw),
            csp.reshape(1, bp, 2, aw),
            ks.reshape(1, bs, ts, N_HEADS, 2, hd), vs.reshape(1, bs, ts, N_HEADS, hw),
            css.reshape(1, bs, 2, aw))
```

```python
import functools
import math

import jax
import jax.numpy as jnp
import numpy as np
from jax import lax
from jax.experimental import pallas as pl
from jax.experimental.pallas import tpu as pltpu

_F32 = jnp.float32
_BF16 = jnp.bfloat16

CHUNK = 64
N_HEADS = 8
TOP_K = 4
NORM_EPS = 1e-6
SWIGLU_ALPHA = 1.702
SWIGLU_LIMIT = 7.0
MASKED_SCORE = -1e30
ROUTER_LANES = 128

V7X_VMEM_LIMIT_BYTES = 56 * 1024 * 1024


def _params(semantics, vmem_bytes=V7X_VMEM_LIMIT_BYTES):
    return pltpu.CompilerParams(dimension_semantics=semantics, vmem_limit_bytes=vmem_bytes)


def _rms(x):
    return x * lax.rsqrt(jnp.mean(x * x, axis=-1, keepdims=True) + NORM_EPS)


def _ada_kernel(c_ref, w_ref, b_ref, o_ref):
    c = c_ref[...]
    s = (c * jax.nn.sigmoid(c)).astype(_BF16)
    o_ref[...] = jnp.dot(s, w_ref[...].astype(_BF16), preferred_element_type=_F32) + b_ref[...]


def _ada(c_all, w_ada, b_ada, tn=1024):
    rows, d = c_all.shape
    n = w_ada.shape[1]
    return pl.pallas_call(
        _ada_kernel,
        grid=(n // tn,),
        in_specs=[pl.BlockSpec((rows, d), lambda j: (0, 0)),
                  pl.BlockSpec((d, tn), lambda j: (0, j)),
                  pl.BlockSpec((1, tn), lambda j: (0, j))],
        out_specs=pl.BlockSpec((rows, tn), lambda j: (0, j)),
        out_shape=jax.ShapeDtypeStruct((rows, n), _F32),
        compiler_params=_params(("arbitrary",)),
        name="ada",
    )(c_all, w_ada, b_ada.reshape(1, n))


def _inproj_kernel(x_ref, shift_ref, scale_ref, g_ref, w_ref, cw_ref, st_ref,
                   q_ref, k_ref, v_ref, kb_ref, vt_ref, co_ref, cs_ref,
                   ug_scr, carry_scr, *, width, seg, tiles_per_batch, use_state, q_scale):
    i = pl.program_id(0)
    tm = x_ref.shape[0]
    h = _rms(x_ref[...]) * g_ref[...]
    h = (h * (1.0 + scale_ref[0]) + shift_ref[0]).astype(_BF16)

    def proj(j):
        return jnp.dot(h, w_ref[:, j * width:(j + 1) * width], preferred_element_type=_F32)

    q_ref[...] = (proj(0) * q_scale).astype(_BF16)
    zk = proj(1)
    k_ref[...] = zk
    kb_ref[...] = zk.astype(_BF16)
    zv = proj(2)
    v_ref[...] = zv
    vt_ref[...] = zv.T.astype(_BF16)

    ug = proj(3) * proj(5)
    gb = proj(4)
    w0 = cw_ref[0:1, :]
    w1 = cw_ref[1:2, :]
    w2 = cw_ref[2:3, :]
    if not use_state:
        @pl.when((i % tiles_per_batch) == 0)
        def _():
            carry_scr[...] = jnp.zeros_like(carry_scr)

    for s in range(tm // seg):
        lo = s * seg
        bnd = st_ref[s] if use_state else carry_scr[...]
        ug_s = ug[lo:lo + seg]
        ug_scr[6:8, :] = bnd
        ug_scr[8:8 + seg, :] = ug_s
        y = w2 * ug_s + w1 * ug_scr[7:7 + seg, :] + w0 * ug_scr[6:6 + seg, :]
        co_ref[lo:lo + seg, :] = (gb[lo:lo + seg] * y).astype(_BF16)
        last2 = ug_scr[6 + seg:8 + seg, :]
        cs_ref[s] = last2
        carry_scr[...] = last2


def _inproj(x2d, shift, scale, g_mix, w_in_bf, conv_w, state, *, tm, seg, tiles_per_batch,
            use_state, q_scale):
    r, d = x2d.shape
    width = w_in_bf.shape[1] // 6
    n_tiles = r // tm
    n_seg = tm // seg
    mod_rows = shift.shape[1]
    if use_state:
        mod_map = lambda i: (i, 0, 0)
        st_spec = pl.BlockSpec((n_seg, 2, width), lambda i: (i, 0, 0))
        cs_spec = pl.BlockSpec((n_seg, 2, width), lambda i: (i, 0, 0))
        n_state = n_tiles * n_seg
    else:
        mod_map = lambda i: (i // tiles_per_batch, 0, 0)
        st_spec = pl.BlockSpec((1, 2, width), lambda i: (0, 0, 0))
        cs_spec = pl.BlockSpec((1, 2, width), lambda i: (i // tiles_per_batch, 0, 0))
        n_state = n_tiles // tiles_per_batch
    kern = functools.partial(_inproj_kernel, width=width, seg=seg, tiles_per_batch=tiles_per_batch,
                             use_state=use_state, q_scale=q_scale)
    row_spec = lambda: pl.BlockSpec((tm, width), lambda i: (i, 0))
    return pl.pallas_call(
        kern,
        grid=(n_tiles,),
        in_specs=[pl.BlockSpec((tm, d), lambda i: (i, 0)),
                  pl.BlockSpec((1, mod_rows, d), mod_map),
                  pl.BlockSpec((1, mod_rows, d), mod_map),
                  pl.BlockSpec((1, d), lambda i: (0, 0)),
                  pl.BlockSpec((d, 6 * width), lambda i: (0, 0), pipeline_mode=pl.Buffered(1)),
                  pl.BlockSpec((3, width), lambda i: (0, 0)),
                  st_spec],
        out_specs=[row_spec(), row_spec(), row_spec(), row_spec(),
                   pl.BlockSpec((width, tm), lambda i: (0, i)),
                   row_spec(), cs_spec],
        out_shape=[jax.ShapeDtypeStruct((r, width), _BF16),
                   jax.ShapeDtypeStruct((r, width), _F32),
                   jax.ShapeDtypeStruct((r, width), _F32),
                   jax.ShapeDtypeStruct((r, width), _BF16),
                   jax.ShapeDtypeStruct((width, r), _BF16),
                   jax.ShapeDtypeStruct((r, width), _BF16),
                   jax.ShapeDtypeStruct((n_state, 2, width), _F32)],
        scratch_shapes=[pltpu.VMEM((seg + 8, width), _F32), pltpu.VMEM((2, width), _F32)],
        compiler_params=_params(("arbitrary",)),
        name="inproj",
    )(x2d, shift, scale, g_mix.reshape(1, d), w_in_bf, conv_w, state)


LOG2E = math.log2(math.e)
SKIP_MARGIN = 160.0
FAST_MARGIN = 60.0
N_AUG = 12


def _split3(v):
    t1 = v.astype(_BF16).astype(_F32)
    r = v - t1
    t2 = r.astype(_BF16).astype(_F32)
    t3 = (r - t2).astype(_BF16).astype(_F32)
    return t1, t2, t3


def _aug_group_term(idx):
    group = (idx >= 3).astype(jnp.int32) + (idx >= 6).astype(jnp.int32) + (idx >= 9).astype(jnp.int32)
    return group, idx - 3 * group


def _attn_prompt_kernel(scal_ref, q_ref, k_ref, vt_ref, g_ref, o_ref,
                        acc_scr, ml_scr, kn_scr, dbias_scr, obias_scr, aug_scr, *, tq, hd, out_scale):
    hh = pl.program_id(1)
    qi = pl.program_id(2)
    lam = scal_ref[0]
    slope2 = scal_ref[1 + hh]
    inv_slope2 = scal_ref[1 + N_HEADS + hh]
    tk = tq
    hw = 2 * hd
    n_kv_total = k_ref.shape[0] // tk

    @pl.when(qi == 0)
    def _():
        lane = lax.broadcasted_iota(jnp.int32, (tk, hw), 1)

        def chunk(c, mx):
            kk = k_ref[pl.ds(pl.multiple_of(c * tk, tk), tk), :].astype(_F32)
            sq = kk * kk
            n0 = jnp.sum(jnp.where(lane < hd, sq, 0.0), axis=1, keepdims=True)
            n1 = jnp.sum(jnp.where(lane >= hd, sq, 0.0), axis=1, keepdims=True)
            return jnp.maximum(mx[0], n0), jnp.maximum(mx[1], n1)

        zero = jnp.zeros((tk, 1), _F32)
        n0, n1 = lax.fori_loop(0, n_kv_total, chunk, (zero, zero))
        kn0 = jnp.sqrt(jnp.max(n0, axis=0, keepdims=True))
        kn1 = jnp.sqrt(jnp.max(n1, axis=0, keepdims=True))
        col = lax.broadcasted_iota(jnp.int32, (1, 2 * tq), 1)
        kn_scr[...] = jnp.where(col < tq, kn0, kn1) * 1.001

        jj = lax.broadcasted_iota(jnp.int32, (tk, tq), 0)
        ii = lax.broadcasted_iota(jnp.int32, (tk, tq), 1)
        jf = jj.astype(_F32)
        iif = ii.astype(_F32)
        visible = (jj // CHUNK) <= (ii // CHUNK)
        dbias = jnp.where(visible, slope2 * iif - slope2 * jnp.abs(iif - jf), MASKED_SCORE)
        dbias_scr[...] = jnp.concatenate([dbias, dbias], axis=1)
        obias = slope2 * jf
        obias_scr[...] = jnp.concatenate([obias, obias], axis=1)

        lc = lane & (hd - 1)
        group, _ = _aug_group_term(lc)
        jrow = lax.broadcasted_iota(jnp.int32, (tk, hw), 0)
        val = jnp.where(group == 0, jrow >> 4, jnp.where(group == 1, jrow & 15, jnp.where(group == 2, 0, 1)))
        aug_scr[...] = jnp.where(lc < N_AUG, val, 0).astype(_F32).astype(_BF16)

    qt = q_ref[...].astype(_F32).T
    qtb = qt.astype(_BF16)
    row2 = lax.broadcasted_iota(jnp.int32, (hw, 2 * tq), 0)
    col2 = lax.broadcasted_iota(jnp.int32, (hw, 2 * tq), 1)
    rhs_bd = jnp.where((row2 < hd) == (col2 < tq), jnp.concatenate([qtb, qtb], axis=1), 0).astype(_BF16)

    def kv_tiles(kv):
        start = pl.multiple_of(kv * tk, tk)
        return k_ref[pl.ds(start, tk), :], vt_ref[:, pl.ds(start, tk)]

    k, vt = kv_tiles(qi)
    s = jnp.dot(k, rhs_bd, preferred_element_type=_F32) + dbias_scr[...]
    m0 = jnp.max(s, axis=0, keepdims=True)
    p = jnp.exp2(s - m0)
    l0 = jnp.sum(p, axis=0, keepdims=True)
    acc_scr[...] = jnp.dot(vt, p.astype(_BF16), preferred_element_type=_F32)

    qsq = qt * qt
    qn = jnp.concatenate([jnp.sqrt(jnp.sum(qsq[:hd], axis=0, keepdims=True)),
                          jnp.sqrt(jnp.sum(qsq[hd:], axis=0, keepdims=True))], axis=1)
    gap = jnp.max(qn * kn_scr[...] - m0, axis=1, keepdims=True)
    reach = ((gap + SKIP_MARGIN) * inv_slope2 - 1.0) * (1.0 / tk)
    n_off = jnp.clip(jnp.floor(reach) + 1.0, 0.0, qi.astype(_F32)).astype(jnp.int32)[0, 0]
    fast = (gap <= FAST_MARGIN).astype(jnp.int32)[0, 0] == 1
    kv_lo = qi - n_off

    @pl.when(fast)
    def _():
        lane_lo = lax.broadcasted_iota(jnp.int32, (tk, hw), 1) < hd
        rowq = lax.broadcasted_iota(jnp.int32, (hw, tq), 0)

        def make_rhs(base, m_half, q_rows):
            local = rowq - base
            group, term = _aug_group_term(local)
            val = jnp.where(group == 0, slope2 * 16.0,
                            jnp.where(group == 1, slope2, jnp.where(group == 2, -slope2 * tk, -m_half)))
            t1, t2, t3 = _split3(val)
            v = jnp.where(term == 0, t1, jnp.where(term == 1, t2, t3))
            bias_rows = jnp.where((local >= 0) & (local < N_AUG), v, 0.0)
            return jnp.where(q_rows, qt, bias_rows).astype(_BF16)

        rhs0 = make_rhs(hd, m0[:, :tq], rowq < hd)
        rhs1 = make_rhs(0, m0[:, tq:], rowq >= hd)
        lcol = lax.broadcasted_iota(jnp.int32, (1, hw), 1) & (hd - 1)
        delta_cols = ((lcol >= 6) & (lcol < 9)).astype(_F32)

        def body(kv, l8):
            k, vt = kv_tiles(kv)
            delta = (qi - kv).astype(_F32)
            aug = aug_scr[...] + (delta * delta_cols).astype(_BF16)
            s = jnp.concatenate(
                [jnp.dot(jnp.where(lane_lo, k, aug), rhs0, preferred_element_type=_F32),
                 jnp.dot(jnp.where(lane_lo, aug, k), rhs1, preferred_element_type=_F32)], axis=1)
            p = jnp.exp2(s)
            l8 = l8 + jnp.sum(p.reshape(tk // 8, 8, 2 * tq), axis=0)
            acc_scr[...] = acc_scr[...] + jnp.dot(vt, p.astype(_BF16), preferred_element_type=_F32)
            return l8

        l8 = lax.fori_loop(kv_lo, qi, body, jnp.zeros((8, 2 * tq), _F32))
        ml_scr[...] = l0 + jnp.sum(l8, axis=0, keepdims=True)

    @pl.when(jnp.logical_not(fast))
    def _():
        def body(kv, carry):
            m, l = carry
            c = -slope2 * ((qi - kv) * tk).astype(_F32)
            k, vt = kv_tiles(kv)
            s = jnp.dot(k, rhs_bd, preferred_element_type=_F32) + obias_scr[...]
            m_new = jnp.maximum(m, jnp.max(s, axis=0, keepdims=True) + c)
            p = jnp.exp2(s - (m_new - c))
            alpha = jnp.exp2(m - m_new)
            l = alpha * l + jnp.sum(p, axis=0, keepdims=True)
            acc_scr[...] = acc_scr[...] * alpha + jnp.dot(vt, p.astype(_BF16), preferred_element_type=_F32)
            return m_new, l

        _, l = lax.fori_loop(kv_lo, qi, body, (m0, l0))
        ml_scr[...] = l

    acc = acc_scr[...] / ml_scr[...]
    o = acc[:, :tq] - lam * acc[:, tq:]
    o = o * lax.rsqrt(jnp.mean(o * o, axis=0, keepdims=True) + NORM_EPS)
    o = o * g_ref[...] * out_scale
    o_ref[...] = o.T.astype(_BF16)


def _attn_prompt(scal, q, kb, vt, subln_col, *, batch, seq, tq, out_scale):
    width = q.shape[1]
    hw = width // N_HEADS
    nq = seq // tq
    kern = functools.partial(_attn_prompt_kernel, tq=tq, hd=hw // 2, out_scale=out_scale)
    return pl.pallas_call(
        kern,
        grid=(batch, N_HEADS, nq),
        in_specs=[pl.BlockSpec(memory_space=pltpu.SMEM),
                  pl.BlockSpec((tq, hw), lambda b, h, i: (b * nq + i, h)),
                  pl.BlockSpec((seq, hw), lambda b, h, i: (b, h)),
                  pl.BlockSpec((hw, seq), lambda b, h, i: (h, b)),
                  pl.BlockSpec((hw, 1), lambda b, h, i: (0, 0))],
        out_specs=pl.BlockSpec((tq, hw), lambda b, h, i: (b * nq + i, h)),
        out_shape=jax.ShapeDtypeStruct((batch * seq, width), _BF16),
        scratch_shapes=[pltpu.VMEM((hw, 2 * tq), _F32),
                        pltpu.VMEM((1, 2 * tq), _F32),
                        pltpu.VMEM((1, 2 * tq), _F32),
                        pltpu.VMEM((tq, 2 * tq), _F32),
                        pltpu.VMEM((tq, 2 * tq), _F32),
                        pltpu.VMEM((tq, hw), _BF16)],
        compiler_params=_params(("arbitrary", "arbitrary", "arbitrary")),
        name="attn_prompt",
    )(scal, q, kb, vt, subln_col)


def _attn_sample_kernel(scal_ref, q_ref, kn_ref, vn_ref, kc_ref, vc_ref, g_ref, o_ref, *, hd, out_scale):
    hh = pl.program_id(1)
    lam = scal_ref[0]
    slope = scal_ref[1 + hh]
    t = q_ref.shape[0]
    past = kc_ref.shape[1]
    q = q_ref[...]
    kc = kc_ref[0].astype(_BF16)
    kn = kn_ref[...].astype(_BF16)
    vc = vc_ref[0].astype(_BF16)
    vn = vn_ref[...].astype(_BF16)
    qpos = lax.broadcasted_iota(jnp.int32, (t, past), 0).astype(_F32) + float(past)
    kpos = lax.broadcasted_iota(jnp.int32, (t, past), 1).astype(_F32)
    bias_c = -slope * jnp.abs(qpos - kpos)
    tq_i = lax.broadcasted_iota(jnp.int32, (t, t), 0).astype(_F32)
    tk_i = lax.broadcasted_iota(jnp.int32, (t, t), 1).astype(_F32)
    bias_n = -slope * jnp.abs(tq_i - tk_i)
    nt = (((1,), (1,)), ((), ()))
    outs = []
    for j in range(2):
        sl = slice(j * hd, (j + 1) * hd)
        sc = lax.dot_general(q[:, sl], kc[:, sl], nt, preferred_element_type=_F32) + bias_c
        sn = lax.dot_general(q[:, sl], kn[:, sl], nt, preferred_element_type=_F32) + bias_n
        m = jnp.maximum(jnp.max(sc, axis=-1, keepdims=True), jnp.max(sn, axis=-1, keepdims=True))
        pc = jnp.exp(sc - m)
        pn = jnp.exp(sn - m)
        l = jnp.sum(pc, axis=-1, keepdims=True) + jnp.sum(pn, axis=-1, keepdims=True)
        o = (jnp.dot(pc.astype(_BF16), vc, preferred_element_type=_F32)
             + jnp.dot(pn.astype(_BF16), vn, preferred_element_type=_F32))
        outs.append(o / l)
    o = outs[0] - lam * outs[1]
    o = _rms(o) * g_ref[...] * out_scale
    o_ref[...] = o.astype(_BF16)


def _attn_sample(scal, q, k_new, v_new, cache_k, cache_v, subln_row, *, batch, t, out_scale):
    width = q.shape[1]
    hw = width // N_HEADS
    past = cache_k.shape[1]
    kern = functools.partial(_attn_sample_kernel, hd=hw // 2, out_scale=out_scale)
    new_spec = lambda: pl.BlockSpec((t, hw), lambda b, h: (b, h))
    cache_spec = lambda: pl.BlockSpec((1, past, hw), lambda b, h: (b, 0, h))
    return pl.pallas_call(
        kern,
        grid=(batch, N_HEADS),
        in_specs=[pl.BlockSpec(memory_space=pltpu.SMEM), new_spec(), new_spec(), new_spec(),
                  cache_spec(), cache_spec(), pl.BlockSpec((1, hw), lambda b, h: (0, 0))],
        out_specs=new_spec(),
        out_shape=jax.ShapeDtypeStruct((batch * t, width), _BF16),
        compiler_params=_params(("arbitrary", "arbitrary")),
        name="attn_sample",
    )(scal, q, k_new, v_new, cache_k, cache_v, subln_row)


def _outproj_kernel(a_ref, c_ref, x_ref, gate_ref, shift_ref, scale_ref, g_ref, wo_ref, wr_ref, br_ref,
                    x1_ref, h2_ref, lg_ref, *, half):
    mix = (jnp.dot(a_ref[...], wo_ref[0:half, :], preferred_element_type=_F32)
           + jnp.dot(c_ref[...], wo_ref[half:2 * half, :], preferred_element_type=_F32))
    x1 = x_ref[...] + gate_ref[0] * mix
    x1_ref[...] = x1
    h2 = _rms(x1) * g_ref[...]
    h2 = h2 * (1.0 + scale_ref[0]) + shift_ref[0]
    h2_ref[...] = h2.astype(_BF16)
    lg_ref[...] = jnp.dot(h2, wr_ref[...], preferred_element_type=_F32,
                          precision=lax.Precision.HIGHEST) + br_ref[...]


def _outproj(attn, conv, x2d, gate, shift, scale, g_ffn, w_o_bf, w_router_pad, b_router_pad, *, tm,
             tiles_per_batch, per_row):
    r, d = x2d.shape
    half = attn.shape[1]
    mod_rows = gate.shape[1]
    mod_map = (lambda i: (i, 0, 0)) if per_row else (lambda i: (i // tiles_per_batch, 0, 0))
    mod_spec = lambda: pl.BlockSpec((1, mod_rows, d), mod_map)
    kern = functools.partial(_outproj_kernel, half=half)
    return pl.pallas_call(
        kern,
        grid=(r // tm,),
        in_specs=[pl.BlockSpec((tm, half), lambda i: (i, 0)),
                  pl.BlockSpec((tm, half), lambda i: (i, 0)),
                  pl.BlockSpec((tm, d), lambda i: (i, 0)),
                  mod_spec(), mod_spec(), mod_spec(),
                  pl.BlockSpec((1, d), lambda i: (0, 0)),
                  pl.BlockSpec((2 * half, d), lambda i: (0, 0), pipeline_mode=pl.Buffered(1)),
                  pl.BlockSpec((d, ROUTER_LANES), lambda i: (0, 0)),
                  pl.BlockSpec((1, ROUTER_LANES), lambda i: (0, 0))],
        out_specs=[pl.BlockSpec((tm, d), lambda i: (i, 0)),
                   pl.BlockSpec((tm, d), lambda i: (i, 0)),
                   pl.BlockSpec((tm, ROUTER_LANES), lambda i: (i, 0))],
        out_shape=[jax.ShapeDtypeStruct((r, d), _F32),
                   jax.ShapeDtypeStruct((r, d), _BF16),
                   jax.ShapeDtypeStruct((r, ROUTER_LANES), _F32)],
        compiler_params=_params(("arbitrary",)),
        name="outproj",
    )(attn, conv, x2d, gate, shift, scale, g_ffn.reshape(1, d), w_o_bf, w_router_pad, b_router_pad)


def _moe_kernel(be_ref, nv_ref, x_ref, wg_ref, wu_ref, bg_ref, bu_ref, wd_ref, bd_ref, o_ref):
    i = pl.program_id(0)
    j = pl.program_id(1)
    valid = i < nv_ref[0]

    @pl.when(valid)
    def _():
        x = x_ref[...]
        g = jnp.dot(x, wg_ref[0].astype(_BF16), preferred_element_type=_F32) + bg_ref[0]
        u = jnp.dot(x, wu_ref[0].astype(_BF16), preferred_element_type=_F32) + bu_ref[0]
        g = jnp.minimum(g, SWIGLU_LIMIT)
        u = jnp.clip(u, -SWIGLU_LIMIT, SWIGLU_LIMIT)
        act = ((u + 1.0) * g * jax.nn.sigmoid(SWIGLU_ALPHA * g)).astype(_BF16)
        part = jnp.dot(act, wd_ref[0].astype(_BF16), preferred_element_type=_F32)

        @pl.when(j == 0)
        def _():
            o_ref[...] = part + bd_ref[0]

        @pl.when(j > 0)
        def _():
            o_ref[...] = o_ref[...] + part

    @pl.when(jnp.logical_not(valid))
    def _():
        o_ref[...] = jnp.zeros_like(o_ref)


def _moe(block_e, n_valid, xs, w_gu, b_gu, w_down, b_down, *, tm, tf):
    rows, d = xs.shape
    n_exp, _, two_ff = w_gu.shape
    d_ff = two_ff // 2
    n_ff = d_ff // tf
    n_blocks = rows // tm

    def ff_idx(i, j, nv):
        return jnp.where(i < nv[0], j, n_ff - 1)

    grid_spec = pltpu.PrefetchScalarGridSpec(
        num_scalar_prefetch=2,
        grid=(n_blocks, n_ff),
        in_specs=[pl.BlockSpec((tm, d), lambda i, j, be, nv: (i, 0)),
                  pl.BlockSpec((1, d, tf), lambda i, j, be, nv: (be[i], 0, ff_idx(i, j, nv))),
                  pl.BlockSpec((1, d, tf), lambda i, j, be, nv: (be[i], 0, n_ff + ff_idx(i, j, nv))),
                  pl.BlockSpec((1, 1, tf), lambda i, j, be, nv: (be[i], 0, ff_idx(i, j, nv))),
                  pl.BlockSpec((1, 1, tf), lambda i, j, be, nv: (be[i], 0, n_ff + ff_idx(i, j, nv))),
                  pl.BlockSpec((1, tf, d), lambda i, j, be, nv: (be[i], ff_idx(i, j, nv), 0)),
                  pl.BlockSpec((1, 1, d), lambda i, j, be, nv: (be[i], 0, 0))],
        out_specs=pl.BlockSpec((tm, d), lambda i, j, be, nv: (i, 0)),
    )
    b_gu3 = b_gu.reshape(n_exp, 1, two_ff)
    return pl.pallas_call(
        _moe_kernel,
        grid_spec=grid_spec,
        out_shape=jax.ShapeDtypeStruct((rows, d), _F32),
        compiler_params=_params(("arbitrary", "arbitrary")),
        name="moe",
    )(block_e, n_valid, xs, w_gu, w_gu, b_gu3, b_gu3, w_down, b_down.reshape(n_exp, 1, d))


def _combine_kernel(x1_ref, y0_ref, y1_ref, y2_ref, y3_ref, gk_ref, gate_ref, g_ref, y_ref):
    gk = gk_ref[...]
    ff = gk[:, 0:1] * y0_ref[...]
    for k, yk_ref in enumerate((y1_ref, y2_ref, y3_ref), start=1):
        ff = ff + gk[:, k:k + 1] * yk_ref[...]
    x2 = x1_ref[...] + gate_ref[0] * ff
    y_ref[...] = _rms(x2) * g_ref[...]


def _combine(x1, yks, gk, gate, g_final, *, tm, tiles_per_batch, per_row):
    r, d = x1.shape
    mod_rows = gate.shape[1]
    mod_map = (lambda i: (i, 0, 0)) if per_row else (lambda i: (i // tiles_per_batch, 0, 0))
    return pl.pallas_call(
        _combine_kernel,
        grid=(r // tm,),
        in_specs=[pl.BlockSpec((tm, d), lambda i: (i, 0))] * (1 + TOP_K) + [
                  pl.BlockSpec((tm, TOP_K), lambda i: (i, 0)),
                  pl.BlockSpec((1, mod_rows, d), mod_map),
                  pl.BlockSpec((1, d), lambda i: (0, 0))],
        out_specs=pl.BlockSpec((tm, d), lambda i: (i, 0)),
        out_shape=jax.ShapeDtypeStruct((r, d), _F32),
        compiler_params=_params(("arbitrary",)),
        name="combine",
    )(x1, *yks, gk, gate, g_final.reshape(1, d))


def _route(logits, n_exp, tm):
    t = logits.shape[0]
    top_v, top_e = lax.top_k(logits, TOP_K)
    gate = jax.nn.softmax(top_v, axis=-1)
    m = t * TOP_K
    n_blocks = -(-m // tm) + n_exp
    flat_e = top_e.reshape(-1).astype(jnp.int32)
    onehot = (flat_e[:, None] == jnp.arange(n_exp, dtype=jnp.int32)[None, :]).astype(jnp.int32)
    rank = jnp.take_along_axis(jnp.cumsum(onehot, axis=0), flat_e[:, None], axis=1)[:, 0] - 1
    counts = jnp.sum(onehot, axis=0)
    padded = ((counts + tm - 1) // tm) * tm
    cum_end = jnp.cumsum(padded)
    pstart = cum_end - padded
    dest = (pstart[flat_e] + rank).astype(jnp.int32)
    buf_tok = jnp.zeros((n_blocks * tm,), jnp.int32).at[dest].set(jnp.arange(m, dtype=jnp.int32) // TOP_K)
    n_valid = (cum_end[-1] // tm).astype(jnp.int32).reshape(1)
    blk_start = jnp.arange(n_blocks, dtype=jnp.int32) * tm
    block_e = jnp.minimum(jnp.searchsorted(cum_end, blk_start, side='right'), n_exp - 1).astype(jnp.int32)
    last_e = block_e[jnp.maximum(n_valid[0] - 1, 0)]
    block_e = jnp.where(jnp.arange(n_blocks) < n_valid[0], block_e, last_e)
    return gate, dest.reshape(t, TOP_K), buf_tok, block_e, n_valid


def _pick_tile(n, pref):
    t = min(n, pref)
    while n % t:
        t //= 2
    return t


def kernel(x_prompt, x_sample, cache_k, cache_v, state_conv, c_prompt, c_sample, g_mix, g_ffn, w_ada, b_ada, w_in, lambda_q1, lambda_k1, lambda_q2, lambda_k2, subln_g, conv_w, w_o, w_router, b_router, w_gu, b_gu, w_down, b_down, g_final):
    depth = g_mix.shape[0]
    assert depth == 1
    bp, sp, d = x_prompt.shape
    bs, ts, _ = x_sample.shape
    past = cache_k.shape[2]
    aw = d // 2
    hw = aw // N_HEADS
    hd = hw // 2
    n_exp = w_router.shape[-1]
    layer = 0

    n_c = bp + bs
    c_rows = -(-n_c // 8) * 8
    c_all = jnp.concatenate([c_prompt, c_sample, jnp.zeros((c_rows - n_c, d), _F32)], axis=0)
    ada = _ada(c_all, w_ada[layer], b_ada[layer])
    shift1, scale1, gate1, shift2, scale2, gate2 = [ada[:, i * d:(i + 1) * d] for i in range(6)]

    lam_init = 0.8 - 0.6 * math.exp(-0.3 * layer)
    lam = (jnp.exp(jnp.sum(lambda_q1[layer] * lambda_k1[layer]))
           - jnp.exp(jnp.sum(lambda_q2[layer] * lambda_k2[layer])) + lam_init)
    slopes = 2.0 ** (-8.0 * np.arange(1, N_HEADS + 1) / N_HEADS)
    lam1 = lam.reshape(1).astype(_F32)
    scal_s = jnp.concatenate([lam1, jnp.asarray(slopes, _F32)])
    scal_p = jnp.concatenate([lam1, jnp.asarray(slopes * LOG2E, _F32), jnp.asarray(1.0 / (slopes * LOG2E), _F32)])
    out_scale = 1.0 - lam_init
    q_scale = hd ** -0.5

    w_in_bf = w_in[layer].astype(_BF16)
    w_o_bf = w_o[layer].astype(_BF16)
    w_router_pad = jnp.zeros((d, ROUTER_LANES), _F32).at[:, :n_exp].set(w_router[layer])
    b_router_pad = jnp.zeros((1, ROUTER_LANES), _F32).at[0, :n_exp].set(b_router[layer])

    rp = bp * sp
    tm_p = _pick_tile(sp, 512)
    tpb = sp // tm_p
    xp2 = x_prompt.reshape(rp, d)
    mod_p = lambda a: a[:bp].reshape(bp, 1, d)
    qp, kp, vp, kbp, vtp, cop, csp = _inproj(
        xp2, mod_p(shift1), mod_p(scale1), g_mix[layer], w_in_bf, conv_w[layer],
        jnp.zeros((1, 2, aw), _F32), tm=tm_p, seg=tm_p, tiles_per_batch=tpb, use_state=False,
        q_scale=q_scale * LOG2E)
    tq = _pick_tile(sp, 512)
    ap = _attn_prompt(scal_p, qp, kbp, vtp, subln_g[layer].reshape(hw, 1), batch=bp, seq=sp, tq=tq,
                      out_scale=out_scale)
    x1p, h2p, lgp = _outproj(ap, cop, xp2, mod_p(gate1), mod_p(shift2), mod_p(scale2), g_ffn[layer],
                             w_o_bf, w_router_pad, b_router_pad, tm=tm_p, tiles_per_batch=tpb,
                             per_row=False)

    rs = bs * ts
    mod_s = lambda a: jnp.repeat(a[bp:bp + bs], ts, axis=0).reshape(1, rs, d)
    xs2 = x_sample.reshape(rs, d)
    qs, ks, vs, _, _, cos, css = _inproj(
        xs2, mod_s(shift1), mod_s(scale1), g_mix[layer], w_in_bf, conv_w[layer],
        state_conv[layer], tm=rs, seg=ts, tiles_per_batch=1, use_state=True, q_scale=q_scale)
    a_s = _attn_sample(scal_s, qs, ks, vs, cache_k[layer].reshape(bs, past, aw),
                       cache_v[layer].reshape(bs, past, aw), subln_g[layer].reshape(1, hw),
                       batch=bs, t=ts, out_scale=out_scale)
    x1s, h2s, lgs = _outproj(a_s, cos, xs2, mod_s(gate1), mod_s(shift2), mod_s(scale2), g_ffn[layer],
                             w_o_bf, w_router_pad, b_router_pad, tm=rs, tiles_per_batch=1, per_row=True)

    tm_e = 512
    h2_all = jnp.concatenate([h2p, h2s], axis=0)
    logits = jnp.concatenate([lgp, lgs], axis=0)[:, :n_exp]
    gate, dest, buf_tok, block_e, n_valid = _route(logits, n_exp, tm_e)
    take_rows = lambda a, idx: a.at[idx].get(mode="promise_in_bounds")
    xs_sorted = take_rows(h2_all, buf_tok)
    d_ff = w_down.shape[2]
    ys = _moe(block_e, n_valid, xs_sorted, w_gu[layer], b_gu[layer], w_down[layer], b_down[layer],
              tm=tm_e, tf=_pick_tile(d_ff, 512))
    yk_p = [take_rows(ys, dest[:rp, k]) for k in range(TOP_K)]
    yk_s = [take_rows(ys, dest[rp:, k]) for k in range(TOP_K)]

    tm_c = _pick_tile(sp, 256)
    y_p = _combine(x1p, yk_p, gate[:rp], mod_p(gate2), g_final, tm=tm_c, tiles_per_batch=sp // tm_c,
                   per_row=False)
    y_s = _combine(x1s, yk_s, gate[rp:], mod_s(gate2), g_final, tm=rs, tiles_per_batch=1, per_row=True)

    return (y_p.reshape(bp, sp, d), y_s.reshape(bs, ts, d),
            kp.reshape(1, bp, sp, N_HEADS, 2, hd), vp.reshape(1, bp, sp, N_HEADS, hw),
            csp.reshape(1, bp, 2, aw),
            ks.reshape(1, bs, ts, N_HEADS, 2, hd), vs.reshape(1, bs, ts, N_HEADS, hw),
            css.reshape(1, bs, 2, aw))
```

```python
import functools
import math

import jax
import jax.numpy as jnp
import numpy as np
from jax import lax
from jax.experimental import pallas as pl
from jax.experimental.pallas import tpu as pltpu

_F32 = jnp.float32
_BF16 = jnp.bfloat16

CHUNK = 64
N_HEADS = 8
TOP_K = 4
NORM_EPS = 1e-6
SWIGLU_ALPHA = 1.702
SWIGLU_LIMIT = 7.0
MASKED_SCORE = -1e30
ROUTER_LANES = 128

V7X_VMEM_LIMIT_BYTES = 56 * 1024 * 1024


def _params(semantics, vmem_bytes=V7X_VMEM_LIMIT_BYTES):
    return pltpu.CompilerParams(dimension_semantics=semantics, vmem_limit_bytes=vmem_bytes)


def _rms(x):
    return x * lax.rsqrt(jnp.mean(x * x, axis=-1, keepdims=True) + NORM_EPS)


def _ada_kernel(c_ref, w_ref, b_ref, o_ref):
    c = c_ref[...]
    s = (c * jax.nn.sigmoid(c)).astype(_BF16)
    o_ref[...] = jnp.dot(s, w_ref[...].astype(_BF16), preferred_element_type=_F32) + b_ref[...]


def _ada(c_all, w_ada, b_ada, tn=1024):
    rows, d = c_all.shape
    n = w_ada.shape[1]
    return pl.pallas_call(
        _ada_kernel,
        grid=(n // tn,),
        in_specs=[pl.BlockSpec((rows, d), lambda j: (0, 0)),
                  pl.BlockSpec((d, tn), lambda j: (0, j)),
                  pl.BlockSpec((1, tn), lambda j: (0, j))],
        out_specs=pl.BlockSpec((rows, tn), lambda j: (0, j)),
        out_shape=jax.ShapeDtypeStruct((rows, n), _F32),
        compiler_params=_params(("arbitrary",)),
        name="ada",
    )(c_all, w_ada, b_ada.reshape(1, n))


def _inproj_kernel(x_ref, shift_ref, scale_ref, g_ref, w_ref, cw_ref, st_ref,
                   q_ref, k_ref, v_ref, kb_ref, vt_ref, co_ref, cs_ref,
                   ug_scr, carry_scr, *, width, seg, tiles_per_batch, use_state, q_scale):
    i = pl.program_id(0)
    tm = x_ref.shape[0]
    h = _rms(x_ref[...]) * g_ref[...]
    h = (h * (1.0 + scale_ref[0]) + shift_ref[0]).astype(_BF16)

    def proj(j):
        return jnp.dot(h, w_ref[:, j * width:(j + 1) * width], preferred_element_type=_F32)

    q_ref[...] = (proj(0) * q_scale).astype(_BF16)
    zk = proj(1)
    k_ref[...] = zk
    kb_ref[...] = zk.astype(_BF16)
    zv = proj(2)
    v_ref[...] = zv
    vt_ref[...] = zv.T.astype(_BF16)

    ug = proj(3) * proj(5)
    gb = proj(4)
    w0 = cw_ref[0:1, :]
    w1 = cw_ref[1:2, :]
    w2 = cw_ref[2:3, :]
    if not use_state:
        @pl.when((i % tiles_per_batch) == 0)
        def _():
            carry_scr[...] = jnp.zeros_like(carry_scr)

    for s in range(tm // seg):
        lo = s * seg
        bnd = st_ref[s] if use_state else carry_scr[...]
        ug_s = ug[lo:lo + seg]
        ug_scr[6:8, :] = bnd
        ug_scr[8:8 + seg, :] = ug_s
        y = w2 * ug_s + w1 * ug_scr[7:7 + seg, :] + w0 * ug_scr[6:6 + seg, :]
        co_ref[lo:lo + seg, :] = (gb[lo:lo + seg] * y).astype(_BF16)
        last2 = ug_scr[6 + seg:8 + seg, :]
        cs_ref[s] = last2
        carry_scr[...] = last2


def _inproj(x2d, shift, scale, g_mix, w_in_bf, conv_w, state, *, tm, seg, tiles_per_batch,
            use_state, q_scale):
    r, d = x2d.shape
    width = w_in_bf.shape[1] // 6
    n_tiles = r // tm
    n_seg = tm // seg
    mod_rows = shift.shape[1]
    if use_state:
        mod_map = lambda i: (i, 0, 0)
        st_spec = pl.BlockSpec((n_seg, 2, width), lambda i: (i, 0, 0))
        cs_spec = pl.BlockSpec((n_seg, 2, width), lambda i: (i, 0, 0))
        n_state = n_tiles * n_seg
    else:
        mod_map = lambda i: (i // tiles_per_batch, 0, 0)
        st_spec = pl.BlockSpec((1, 2, width), lambda i: (0, 0, 0))
        cs_spec = pl.BlockSpec((1, 2, width), lambda i: (i // tiles_per_batch, 0, 0))
        n_state = n_tiles // tiles_per_batch
    kern = functools.partial(_inproj_kernel, width=width, seg=seg, tiles_per_batch=tiles_per_batch,
                             use_state=use_state, q_scale=q_scale)
    row_spec = lambda: pl.BlockSpec((tm, width), lambda i: (i, 0))
    return pl.pallas_call(
        kern,
        grid=(n_tiles,),
        in_specs=[pl.BlockSpec((tm, d), lambda i: (i, 0)),
                  pl.BlockSpec((1, mod_rows, d), mod_map),
                  pl.BlockSpec((1, mod_rows, d), mod_map),
                  pl.BlockSpec((1, d), lambda i: (0, 0)),
                  pl.BlockSpec((d, 6 * width), lambda i: (0, 0), pipeline_mode=pl.Buffered(1)),
                  pl.BlockSpec((3, width), lambda i: (0, 0)),
                  st_spec],
        out_specs=[row_spec(), row_spec(), row_spec(), row_spec(),
                   pl.BlockSpec((width, tm), lambda i: (0, i)),
                   row_spec(), cs_spec],
        out_shape=[jax.ShapeDtypeStruct((r, width), _BF16),
                   jax.ShapeDtypeStruct((r, width), _F32),
                   jax.ShapeDtypeStruct((r, width), _F32),
                   jax.ShapeDtypeStruct((r, width), _BF16),
                   jax.ShapeDtypeStruct((width, r), _BF16),
                   jax.ShapeDtypeStruct((r, width), _BF16),
                   jax.ShapeDtypeStruct((n_state, 2, width), _F32)],
        scratch_shapes=[pltpu.VMEM((seg + 8, width), _F32), pltpu.VMEM((2, width), _F32)],
        compiler_params=_params(("arbitrary",)),
        name="inproj",
    )(x2d, shift, scale, g_mix.reshape(1, d), w_in_bf, conv_w, state)


LOG2E = math.log2(math.e)
SKIP_MARGIN = 160.0
FAST_MARGIN = 60.0
N_AUG = 12


def _split3(v):
    t1 = v.astype(_BF16).astype(_F32)
    r = v - t1
    t2 = r.astype(_BF16).astype(_F32)
    t3 = (r - t2).astype(_BF16).astype(_F32)
    return t1, t2, t3


def _aug_group_term(idx):
    group = (idx >= 3).astype(jnp.int32) + (idx >= 6).astype(jnp.int32) + (idx >= 9).astype(jnp.int32)
    return group, idx - 3 * group


def _attn_prompt_kernel(scal_ref, q_ref, k_ref, vt_ref, g_ref, o_ref,
                        acc_scr, ml_scr, kn_scr, dbias_scr, obias_scr, aug_scr, *, tq, hd, out_scale):
    hh = pl.program_id(1)
    qi = pl.program_id(2)
    lam = scal_ref[0]
    slope2 = scal_ref[1 + hh]
    inv_slope2 = scal_ref[1 + N_HEADS + hh]
    tk = tq
    hw = 2 * hd
    n_kv_total = k_ref.shape[0] // tk

    @pl.when(qi == 0)
    def _():
        lane = lax.broadcasted_iota(jnp.int32, (tk, hw), 1)

        def chunk(c, mx):
            kk = k_ref[pl.ds(pl.multiple_of(c * tk, tk), tk), :].astype(_F32)
            sq = kk * kk
            n0 = jnp.sum(jnp.where(lane < hd, sq, 0.0), axis=1, keepdims=True)
            n1 = jnp.sum(jnp.where(lane >= hd, sq, 0.0), axis=1, keepdims=True)
            return jnp.maximum(mx[0], n0), jnp.maximum(mx[1], n1)

        zero = jnp.zeros((tk, 1), _F32)
        n0, n1 = lax.fori_loop(0, n_kv_total, chunk, (zero, zero))
        kn0 = jnp.sqrt(jnp.max(n0, axis=0, keepdims=True))
        kn1 = jnp.sqrt(jnp.max(n1, axis=0, keepdims=True))
        col = lax.broadcasted_iota(jnp.int32, (1, 2 * tq), 1)
        kn_scr[...] = jnp.where(col < tq, kn0, kn1) * 1.001

        jj = lax.broadcasted_iota(jnp.int32, (tk, tq), 0)
        ii = lax.broadcasted_iota(jnp.int32, (tk, tq), 1)
        jf = jj.astype(_F32)
        iif = ii.astype(_F32)
        visible = (jj // CHUNK) <= (ii // CHUNK)
        dbias = jnp.where(visible, slope2 * iif - slope2 * jnp.abs(iif - jf), MASKED_SCORE)
        dbias_scr[...] = jnp.concatenate([dbias, dbias], axis=1)
        obias = slope2 * jf
        obias_scr[...] = jnp.concatenate([obias, obias], axis=1)

        lc = lane & (hd - 1)
        group, _ = _aug_group_term(lc)
        jrow = lax.broadcasted_iota(jnp.int32, (tk, hw), 0)
        val = jnp.where(group == 0, jrow >> 4, jnp.where(group == 1, jrow & 15, jnp.where(group == 2, 0, 1)))
        aug_scr[...] = jnp.where(lc < N_AUG, val, 0).astype(_F32).astype(_BF16)

    qt = q_ref[...].astype(_F32).T
    qtb = qt.astype(_BF16)
    row2 = lax.broadcasted_iota(jnp.int32, (hw, 2 * tq), 0)
    col2 = lax.broadcasted_iota(jnp.int32, (hw, 2 * tq), 1)
    rhs_bd = jnp.where((row2 < hd) == (col2 < tq), jnp.concatenate([qtb, qtb], axis=1), 0).astype(_BF16)

    def kv_tiles(kv):
        start = pl.multiple_of(kv * tk, tk)
        return k_ref[pl.ds(start, tk), :], vt_ref[:, pl.ds(start, tk)]

    k, vt = kv_tiles(qi)
    s = jnp.dot(k, rhs_bd, preferred_element_type=_F32) + dbias_scr[...]
    m0 = jnp.max(s, axis=0, keepdims=True)
    p = jnp.exp2(s - m0)
    l0 = jnp.sum(p, axis=0, keepdims=True)
    acc_scr[...] = jnp.dot(vt, p.astype(_BF16), preferred_element_type=_F32)

    qsq = qt * qt
    qn = jnp.concatenate([jnp.sqrt(jnp.sum(qsq[:hd], axis=0, keepdims=True)),
                          jnp.sqrt(jnp.sum(qsq[hd:], axis=0, keepdims=True))], axis=1)
    gap = jnp.max(qn * kn_scr[...] - m0, axis=1, keepdims=True)
    reach = ((gap + SKIP_MARGIN) * inv_slope2 - 1.0) * (1.0 / tk)
    n_off = jnp.clip(jnp.floor(reach) + 1.0, 0.0, qi.astype(_F32)).astype(jnp.int32)[0, 0]
    fast = (gap <= FAST_MARGIN).astype(jnp.int32)[0, 0] == 1
    kv_lo = qi - n_off

    @pl.when(fast)
    def _():
        lane_lo = lax.broadcasted_iota(jnp.int32, (tk, hw), 1) < hd
        rowq = lax.broadcasted_iota(jnp.int32, (hw, tq), 0)

        def make_rhs(base, m_half, q_rows):
            local = rowq - base
            group, term = _aug_group_term(local)
            val = jnp.where(group == 0, slope2 * 16.0,
                            jnp.where(group == 1, slope2, jnp.where(group == 2, -slope2 * tk, -m_half)))
            t1, t2, t3 = _split3(val)
            v = jnp.where(term == 0, t1, jnp.where(term == 1, t2, t3))
            bias_rows = jnp.where((local >= 0) & (local < N_AUG), v, 0.0)
            return jnp.where(q_rows, qt, bias_rows).astype(_BF16)

        rhs0 = make_rhs(hd, m0[:, :tq], rowq < hd)
        rhs1 = make_rhs(0, m0[:, tq:], rowq >= hd)
        lcol = lax.broadcasted_iota(jnp.int32, (1, hw), 1) & (hd - 1)
        delta_cols = ((lcol >= 6) & (lcol < 9)).astype(_F32)

        def body(kv, l8):
            k, vt = kv_tiles(kv)
            delta = (qi - kv).astype(_F32)
            aug = aug_scr[...] + (delta * delta_cols).astype(_BF16)
            s = jnp.concatenate(
                [jnp.dot(jnp.where(lane_lo, k, aug), rhs0, preferred_element_type=_F32),
                 jnp.dot(jnp.where(lane_lo, aug, k), rhs1, preferred_element_type=_F32)], axis=1)
            p = jnp.exp2(s)
            l8 = l8 + jnp.sum(p.reshape(tk // 8, 8, 2 * tq), axis=0)
            acc_scr[...] = acc_scr[...] + jnp.dot(vt, p.astype(_BF16), preferred_element_type=_F32)
            return l8

        l8 = lax.fori_loop(kv_lo, qi, body, jnp.zeros((8, 2 * tq), _F32))
        ml_scr[...] = l0 + jnp.sum(l8, axis=0, keepdims=True)

    @pl.when(jnp.logical_not(fast))
    def _():
        def body(kv, carry):
            m, l = carry
            c = -slope2 * ((qi - kv) * tk).astype(_F32)
            k, vt = kv_tiles(kv)
            s = jnp.dot(k, rhs_bd, preferred_element_type=_F32) + obias_scr[...]
            m_new = jnp.maximum(m, jnp.max(s, axis=0, keepdims=True) + c)
            p = jnp.exp2(s - (m_new - c))
            alpha = jnp.exp2(m - m_new)
            l = alpha * l + jnp.sum(p, axis=0, keepdims=True)
            acc_scr[...] = acc_scr[...] * alpha + jnp.dot(vt, p.astype(_BF16), preferred_element_type=_F32)
            return m_new, l

        _, l = lax.fori_loop(kv_lo, qi, body, (m0, l0))
        ml_scr[...] = l

    acc = acc_scr[...] / ml_scr[...]
    o = acc[:, :tq] - lam * acc[:, tq:]
    o = o * lax.rsqrt(jnp.mean(o * o, axis=0, keepdims=True) + NORM_EPS)
    o = o * g_ref[...] * out_scale
    o_ref[...] = o.T.astype(_BF16)


def _attn_prompt(scal, q, kb, vt, subln_col, *, batch, seq, tq, out_scale):
    width = q.shape[1]
    hw = width // N_HEADS
    nq = seq // tq
    kern = functools.partial(_attn_prompt_kernel, tq=tq, hd=hw // 2, out_scale=out_scale)
    return pl.pallas_call(
        kern,
        grid=(batch, N_HEADS, nq),
        in_specs=[pl.BlockSpec(memory_space=pltpu.SMEM),
                  pl.BlockSpec((tq, hw), lambda b, h, i: (b * nq + i, h)),
                  pl.BlockSpec((seq, hw), lambda b, h, i: (b, h)),
                  pl.BlockSpec((hw, seq), lambda b, h, i: (h, b)),
                  pl.BlockSpec((hw, 1), lambda b, h, i: (0, 0))],
        out_specs=pl.BlockSpec((tq, hw), lambda b, h, i: (b * nq + i, h)),
        out_shape=jax.ShapeDtypeStruct((batch * seq, width), _BF16),
        scratch_shapes=[pltpu.VMEM((hw, 2 * tq), _F32),
                        pltpu.VMEM((1, 2 * tq), _F32),
                        pltpu.VMEM((1, 2 * tq), _F32),
                        pltpu.VMEM((tq, 2 * tq), _F32),
                        pltpu.VMEM((tq, 2 * tq), _F32),
                        pltpu.VMEM((tq, hw), _BF16)],
        compiler_params=_params(("arbitrary", "arbitrary", "arbitrary")),
        name="attn_prompt",
    )(scal, q, kb, vt, subln_col)


def _attn_sample_kernel(scal_ref, q_ref, kn_ref, vn_ref, kc_ref, vc_ref, g_ref, o_ref, *, hd, out_scale):
    hh = pl.program_id(1)
    lam = scal_ref[0]
    slope = scal_ref[1 + hh]
    t = q_ref.shape[0]
    past = kc_ref.shape[1]
    q = q_ref[...]
    kc = kc_ref[0].astype(_BF16)
    kn = kn_ref[...].astype(_BF16)
    vc = vc_ref[0].astype(_BF16)
    vn = vn_ref[...].astype(_BF16)
    qpos = lax.broadcasted_iota(jnp.int32, (t, past), 0).astype(_F32) + float(past)
    kpos = lax.broadcasted_iota(jnp.int32, (t, past), 1).astype(_F32)
    bias_c = -slope * jnp.abs(qpos - kpos)
    tq_i = lax.broadcasted_iota(jnp.int32, (t, t), 0).astype(_F32)
    tk_i = lax.broadcasted_iota(jnp.int32, (t, t), 1).astype(_F32)
    bias_n = -slope * jnp.abs(tq_i - tk_i)
    nt = (((1,), (1,)), ((), ()))
    outs = []
    for j in range(2):
        sl = slice(j * hd, (j + 1) * hd)
        sc = lax.dot_general(q[:, sl], kc[:, sl], nt, preferred_element_type=_F32) + bias_c
        sn = lax.dot_general(q[:, sl], kn[:, sl], nt, preferred_element_type=_F32) + bias_n
        m = jnp.maximum(jnp.max(sc, axis=-1, keepdims=True), jnp.max(sn, axis=-1, keepdims=True))
        pc = jnp.exp(sc - m)
        pn = jnp.exp(sn - m)
        l = jnp.sum(pc, axis=-1, keepdims=True) + jnp.sum(pn, axis=-1, keepdims=True)
        o = (jnp.dot(pc.astype(_BF16), vc, preferred_element_type=_F32)
             + jnp.dot(pn.astype(_BF16), vn, preferred_element_type=_F32))
        outs.append(o / l)
    o = outs[0] - lam * outs[1]
    o = _rms(o) * g_ref[...] * out_scale
    o_ref[...] = o.astype(_BF16)


def _attn_sample(scal, q, k_new, v_new, cache_k, cache_v, subln_row, *, batch, t, out_scale):
    width = q.shape[1]
    hw = width // N_HEADS
    past = cache_k.shape[1]
    kern = functools.partial(_attn_sample_kernel, hd=hw // 2, out_scale=out_scale)
    new_spec = lambda: pl.BlockSpec((t, hw), lambda b, h: (b, h))
    cache_spec = lambda: pl.BlockSpec((1, past, hw), lambda b, h: (b, 0, h))
    return pl.pallas_call(
        kern,
        grid=(batch, N_HEADS),
        in_specs=[pl.BlockSpec(memory_space=pltpu.SMEM), new_spec(), new_spec(), new_spec(),
                  cache_spec(), cache_spec(), pl.BlockSpec((1, hw), lambda b, h: (0, 0))],
        out_specs=new_spec(),
        out_shape=jax.ShapeDtypeStruct((batch * t, width), _BF16),
        compiler_params=_params(("arbitrary", "arbitrary")),
        name="attn_sample",
    )(scal, q, k_new, v_new, cache_k, cache_v, subln_row)


def _outproj_kernel(a_ref, c_ref, x_ref, gate_ref, shift_ref, scale_ref, g_ref, wo_ref, wrh_ref, wrl_ref,
                    br_ref, x1_ref, h2_ref, lg_ref, *, half):
    mix = (jnp.dot(a_ref[...], wo_ref[0:half, :], preferred_element_type=_F32)
           + jnp.dot(c_ref[...], wo_ref[half:2 * half, :], preferred_element_type=_F32))
    x1 = x_ref[...] + gate_ref[0] * mix
    x1_ref[...] = x1
    h2 = _rms(x1) * g_ref[...]
    h2 = h2 * (1.0 + scale_ref[0]) + shift_ref[0]
    hi = h2.astype(_BF16)
    h2_ref[...] = hi
    lo = (h2 - hi.astype(_F32)).astype(_BF16)
    lg_ref[...] = (jnp.dot(hi, wrh_ref[...], preferred_element_type=_F32)
                   + jnp.dot(lo, wrh_ref[...], preferred_element_type=_F32)
                   + jnp.dot(hi, wrl_ref[...], preferred_element_type=_F32)) + br_ref[...]


def _outproj(attn, conv, x2d, gate, shift, scale, g_ffn, w_o_bf, w_router_pad, b_router_pad, *, tm,
             tiles_per_batch, per_row):
    w_router_hi = w_router_pad.astype(_BF16)
    w_router_lo = (w_router_pad - w_router_hi.astype(_F32)).astype(_BF16)
    r, d = x2d.shape
    half = attn.shape[1]
    mod_rows = gate.shape[1]
    mod_map = (lambda i: (i, 0, 0)) if per_row else (lambda i: (i // tiles_per_batch, 0, 0))
    mod_spec = lambda: pl.BlockSpec((1, mod_rows, d), mod_map)
    kern = functools.partial(_outproj_kernel, half=half)
    return pl.pallas_call(
        kern,
        grid=(r // tm,),
        in_specs=[pl.BlockSpec((tm, half), lambda i: (i, 0)),
                  pl.BlockSpec((tm, half), lambda i: (i, 0)),
                  pl.BlockSpec((tm, d), lambda i: (i, 0)),
                  mod_spec(), mod_spec(), mod_spec(),
                  pl.BlockSpec((1, d), lambda i: (0, 0)),
                  pl.BlockSpec((2 * half, d), lambda i: (0, 0), pipeline_mode=pl.Buffered(1)),
                  pl.BlockSpec((d, ROUTER_LANES), lambda i: (0, 0)),
                  pl.BlockSpec((d, ROUTER_LANES), lambda i: (0, 0)),
                  pl.BlockSpec((1, ROUTER_LANES), lambda i: (0, 0))],
        out_specs=[pl.BlockSpec((tm, d), lambda i: (i, 0)),
                   pl.BlockSpec((tm, d), lambda i: (i, 0)),
                   pl.BlockSpec((tm, ROUTER_LANES), lambda i: (i, 0))],
        out_shape=[jax.ShapeDtypeStruct((r, d), _F32),
                   jax.ShapeDtypeStruct((r, d), _BF16),
                   jax.ShapeDtypeStruct((r, ROUTER_LANES), _F32)],
        compiler_params=_params(("arbitrary",)),
        name="outproj",
    )(attn, conv, x2d, gate, shift, scale, g_ffn.reshape(1, d), w_o_bf, w_router_hi, w_router_lo,
      b_router_pad)


def _expert_changed(be_ref, i):
    return jnp.logical_or(i == 0, be_ref[i] != be_ref[jnp.maximum(i - 1, 0)])


def _moe_up_kernel(be_ref, nv_ref, x_ref, wg_ref, wu_ref, bg_ref, bu_ref, act_ref, wg_scr, wu_scr):
    i = pl.program_id(1)
    valid = i < nv_ref[0]

    @pl.when(jnp.logical_and(valid, _expert_changed(be_ref, i)))
    def _():
        wg_scr[...] = wg_ref[0].astype(_BF16)
        wu_scr[...] = wu_ref[0].astype(_BF16)

    @pl.when(valid)
    def _():
        x = x_ref[...]
        g = jnp.dot(x, wg_scr[...], preferred_element_type=_F32) + bg_ref[0]
        u = jnp.dot(x, wu_scr[...], preferred_element_type=_F32) + bu_ref[0]
        g = jnp.minimum(g, SWIGLU_LIMIT)
        u = jnp.clip(u, -SWIGLU_LIMIT, SWIGLU_LIMIT)
        act_ref[...] = ((u + 1.0) * g * jax.nn.sigmoid(SWIGLU_ALPHA * g)).astype(_BF16)

    @pl.when(jnp.logical_not(valid))
    def _():
        act_ref[...] = jnp.zeros_like(act_ref)


def _moe_down_kernel(be_ref, nv_ref, a_ref, wd_ref, bd_ref, y_ref, wd_scr):
    i = pl.program_id(1)
    valid = i < nv_ref[0]

    @pl.when(jnp.logical_and(valid, _expert_changed(be_ref, i)))
    def _():
        wd_scr[...] = wd_ref[0].astype(_BF16)

    @pl.when(valid)
    def _():
        y_ref[...] = jnp.dot(a_ref[...], wd_scr[...], preferred_element_type=_F32) + bd_ref[0]

    @pl.when(jnp.logical_not(valid))
    def _():
        y_ref[...] = jnp.zeros_like(y_ref)


def _moe(block_e, n_valid, xs, w_gu, b_gu, w_down, b_down, *, tm, tn):
    rows, d = xs.shape
    n_exp, _, two_ff = w_gu.shape
    d_ff = two_ff // 2
    n_blocks = rows // tm
    n_up = d_ff // tn
    b_gu3 = b_gu.reshape(n_exp, 1, two_ff)
    act = pl.pallas_call(
        _moe_up_kernel,
        grid_spec=pltpu.PrefetchScalarGridSpec(
            num_scalar_prefetch=2,
            grid=(n_up, n_blocks),
            in_specs=[pl.BlockSpec((tm, d), lambda n, i, be, nv: (i, 0)),
                      pl.BlockSpec((1, d, tn), lambda n, i, be, nv: (be[i], 0, n)),
                      pl.BlockSpec((1, d, tn), lambda n, i, be, nv: (be[i], 0, n_up + n)),
                      pl.BlockSpec((1, 1, tn), lambda n, i, be, nv: (be[i], 0, n)),
                      pl.BlockSpec((1, 1, tn), lambda n, i, be, nv: (be[i], 0, n_up + n))],
            out_specs=pl.BlockSpec((tm, tn), lambda n, i, be, nv: (i, n)),
            scratch_shapes=[pltpu.VMEM((d, tn), _BF16), pltpu.VMEM((d, tn), _BF16)]),
        out_shape=jax.ShapeDtypeStruct((rows, d_ff), _BF16),
        compiler_params=_params(("arbitrary", "arbitrary")),
        name="moe_up",
    )(block_e, n_valid, xs, w_gu, w_gu, b_gu3, b_gu3)
    n_down = d // tn
    return pl.pallas_call(
        _moe_down_kernel,
        grid_spec=pltpu.PrefetchScalarGridSpec(
            num_scalar_prefetch=2,
            grid=(n_down, n_blocks),
            in_specs=[pl.BlockSpec((tm, d_ff), lambda n, i, be, nv: (i, 0)),
                      pl.BlockSpec((1, d_ff, tn), lambda n, i, be, nv: (be[i], 0, n)),
                      pl.BlockSpec((1, 1, tn), lambda n, i, be, nv: (be[i], 0, n))],
            out_specs=pl.BlockSpec((tm, tn), lambda n, i, be, nv: (i, n)),
            scratch_shapes=[pltpu.VMEM((d_ff, tn), _BF16)]),
        out_shape=jax.ShapeDtypeStruct((rows, d), _F32),
        compiler_params=_params(("arbitrary", "arbitrary")),
        name="moe_down",
    )(block_e, n_valid, act, w_down, b_down.reshape(n_exp, 1, d))


def _combine_kernel(x1_ref, y0_ref, y1_ref, y2_ref, y3_ref, gk_ref, gate_ref, g_ref, y_ref):
    gk = gk_ref[...]
    ff = gk[:, 0:1] * y0_ref[...]
    for k, yk_ref in enumerate((y1_ref, y2_ref, y3_ref), start=1):
        ff = ff + gk[:, k:k + 1] * yk_ref[...]
    x2 = x1_ref[...] + gate_ref[0] * ff
    y_ref[...] = _rms(x2) * g_ref[...]


def _combine(x1, yks, gk, gate, g_final, *, tm, tiles_per_batch, per_row):
    r, d = x1.shape
    mod_rows = gate.shape[1]
    mod_map = (lambda i: (i, 0, 0)) if per_row else (lambda i: (i // tiles_per_batch, 0, 0))
    return pl.pallas_call(
        _combine_kernel,
        grid=(r // tm,),
        in_specs=[pl.BlockSpec((tm, d), lambda i: (i, 0))] * (1 + TOP_K) + [
                  pl.BlockSpec((tm, TOP_K), lambda i: (i, 0)),
                  pl.BlockSpec((1, mod_rows, d), mod_map),
                  pl.BlockSpec((1, d), lambda i: (0, 0))],
        out_specs=pl.BlockSpec((tm, d), lambda i: (i, 0)),
        out_shape=jax.ShapeDtypeStruct((r, d), _F32),
        compiler_params=_params(("arbitrary",)),
        name="combine",
    )(x1, *yks, gk, gate, g_final.reshape(1, d))


def _route(logits, n_exp, tm):
    t = logits.shape[0]
    top_v, top_e = lax.top_k(logits, TOP_K)
    gate = jax.nn.softmax(top_v, axis=-1)
    m = t * TOP_K
    n_blocks = -(-m // tm) + n_exp
    flat_e = top_e.reshape(-1).astype(jnp.int32)
    onehot = (flat_e[:, None] == jnp.arange(n_exp, dtype=jnp.int32)[None, :]).astype(jnp.int32)
    rank = jnp.take_along_axis(jnp.cumsum(onehot, axis=0), flat_e[:, None], axis=1)[:, 0] - 1
    counts = jnp.sum(onehot, axis=0)
    padded = ((counts + tm - 1) // tm) * tm
    cum_end = jnp.cumsum(padded)
    pstart = cum_end - padded
    dest = (pstart[flat_e] + rank).astype(jnp.int32)
    buf_tok = jnp.zeros((n_blocks * tm,), jnp.int32).at[dest].set(jnp.arange(m, dtype=jnp.int32) // TOP_K)
    n_valid = (cum_end[-1] // tm).astype(jnp.int32).reshape(1)
    blk_start = jnp.arange(n_blocks, dtype=jnp.int32) * tm
    block_e = jnp.minimum(jnp.searchsorted(cum_end, blk_start, side='right'), n_exp - 1).astype(jnp.int32)
    last_e = block_e[jnp.maximum(n_valid[0] - 1, 0)]
    block_e = jnp.where(jnp.arange(n_blocks) < n_valid[0], block_e, last_e)
    return gate, dest.reshape(t, TOP_K), buf_tok, block_e, n_valid


def _pick_tile(n, pref):
    t = min(n, pref)
    while n % t:
        t //= 2
    return t


def kernel(x_prompt, x_sample, cache_k, cache_v, state_conv, c_prompt, c_sample, g_mix, g_ffn, w_ada, b_ada, w_in, lambda_q1, lambda_k1, lambda_q2, lambda_k2, subln_g, conv_w, w_o, w_router, b_router, w_gu, b_gu, w_down, b_down, g_final):
    depth = g_mix.shape[0]
    assert depth == 1
    bp, sp, d = x_prompt.shape
    bs, ts, _ = x_sample.shape
    past = cache_k.shape[2]
    aw = d // 2
    hw = aw // N_HEADS
    hd = hw // 2
    n_exp = w_router.shape[-1]
    layer = 0

    n_c = bp + bs
    c_rows = -(-n_c // 8) * 8
    c_all = jnp.concatenate([c_prompt, c_sample, jnp.zeros((c_rows - n_c, d), _F32)], axis=0)
    ada = _ada(c_all, w_ada[layer], b_ada[layer])
    shift1, scale1, gate1, shift2, scale2, gate2 = [ada[:, i * d:(i + 1) * d] for i in range(6)]

    lam_init = 0.8 - 0.6 * math.exp(-0.3 * layer)
    lam = (jnp.exp(jnp.sum(lambda_q1[layer] * lambda_k1[layer]))
           - jnp.exp(jnp.sum(lambda_q2[layer] * lambda_k2[layer])) + lam_init)
    slopes = 2.0 ** (-8.0 * np.arange(1, N_HEADS + 1) / N_HEADS)
    lam1 = lam.reshape(1).astype(_F32)
    scal_s = jnp.concatenate([lam1, jnp.asarray(slopes, _F32)])
    scal_p = jnp.concatenate([lam1, jnp.asarray(slopes * LOG2E, _F32), jnp.asarray(1.0 / (slopes * LOG2E), _F32)])
    out_scale = 1.0 - lam_init
    q_scale = hd ** -0.5

    w_in_bf = w_in[layer].astype(_BF16)
    w_o_bf = w_o[layer].astype(_BF16)
    w_router_pad = jnp.zeros((d, ROUTER_LANES), _F32).at[:, :n_exp].set(w_router[layer])
    b_router_pad = jnp.zeros((1, ROUTER_LANES), _F32).at[0, :n_exp].set(b_router[layer])

    rp = bp * sp
    tm_p = _pick_tile(sp, 512)
    tpb = sp // tm_p
    xp2 = x_prompt.reshape(rp, d)
    mod_p = lambda a: a[:bp].reshape(bp, 1, d)
    qp, kp, vp, kbp, vtp, cop, csp = _inproj(
        xp2, mod_p(shift1), mod_p(scale1), g_mix[layer], w_in_bf, conv_w[layer],
        jnp.zeros((1, 2, aw), _F32), tm=tm_p, seg=tm_p, tiles_per_batch=tpb, use_state=False,
        q_scale=q_scale * LOG2E)
    tq = _pick_tile(sp, 512)
    ap = _attn_prompt(scal_p, qp, kbp, vtp, subln_g[layer].reshape(hw, 1), batch=bp, seq=sp, tq=tq,
                      out_scale=out_scale)
    x1p, h2p, lgp = _outproj(ap, cop, xp2, mod_p(gate1), mod_p(shift2), mod_p(scale2), g_ffn[layer],
                             w_o_bf, w_router_pad, b_router_pad, tm=tm_p, tiles_per_batch=tpb,
                             per_row=False)

    rs = bs * ts
    mod_s = lambda a: jnp.repeat(a[bp:bp + bs], ts, axis=0).reshape(1, rs, d)
    xs2 = x_sample.reshape(rs, d)
    qs, ks, vs, _, _, cos, css = _inproj(
        xs2, mod_s(shift1), mod_s(scale1), g_mix[layer], w_in_bf, conv_w[layer],
        state_conv[layer], tm=rs, seg=ts, tiles_per_batch=1, use_state=True, q_scale=q_scale)
    a_s = _attn_sample(scal_s, qs, ks, vs, cache_k[layer].reshape(bs, past, aw),
                       cache_v[layer].reshape(bs, past, aw), subln_g[layer].reshape(1, hw),
                       batch=bs, t=ts, out_scale=out_scale)
    x1s, h2s, lgs = _outproj(a_s, cos, xs2, mod_s(gate1), mod_s(shift2), mod_s(scale2), g_ffn[layer],
                             w_o_bf, w_router_pad, b_router_pad, tm=rs, tiles_per_batch=1, per_row=True)

    tm_e = 512
    h2_all = jnp.concatenate([h2p, h2s], axis=0)
    logits = jnp.concatenate([lgp, lgs], axis=0)[:, :n_exp]
    gate, dest, buf_tok, block_e, n_valid = _route(logits, n_exp, tm_e)
    take_rows = lambda a, idx: a.at[idx].get(mode="promise_in_bounds")
    xs_sorted = take_rows(h2_all, buf_tok)
    d_ff = w_down.shape[2]
    ys = _moe(block_e, n_valid, xs_sorted, w_gu[layer], b_gu[layer], w_down[layer], b_down[layer],
              tm=tm_e, tn=_pick_tile(d_ff, 1024))
    yk_p = [take_rows(ys, dest[:rp, k]) for k in range(TOP_K)]
    yk_s = [take_rows(ys, dest[rp:, k]) for k in range(TOP_K)]

    tm_c = _pick_tile(sp, 256)
    y_p = _combine(x1p, yk_p, gate[:rp], mod_p(gate2), g_final, tm=tm_c, tiles_per_batch=sp // tm_c,
                   per_row=False)
    y_s = _combine(x1s, yk_s, gate[rp:], mod_s(gate2), g_final, tm=rs, tiles_per_batch=1, per_row=True)

    return (y_p.reshape(bp, sp, d), y_s.reshape(bs, ts, d),
            kp.reshape(1, bp, sp, N_HEADS, 2, hd), vp.reshape(1, bp, sp, N_HEADS, hw),
            csp.reshape(1, bp, 2, aw),
            ks.reshape(1, bs, ts, N_HEADS, 2, hd), vs.reshape(1, bs, ts, N_HEADS, hw),
            css.reshape(1, bs, 2, aw))
```

```python
import functools
import math

import jax
import jax.numpy as jnp
import numpy as np
from jax import lax
from jax.experimental import pallas as pl
from jax.experimental.pallas import tpu as pltpu

_F32 = jnp.float32
_BF16 = jnp.bfloat16

CHUNK = 64
N_HEADS = 8
TOP_K = 4
NORM_EPS = 1e-6
SWIGLU_ALPHA = 1.702
SWIGLU_LIMIT = 7.0
MASKED_SCORE = -1e30
ROUTER_LANES = 128

V7X_VMEM_LIMIT_BYTES = 56 * 1024 * 1024


def _params(semantics, vmem_bytes=V7X_VMEM_LIMIT_BYTES):
    return pltpu.CompilerParams(dimension_semantics=semantics, vmem_limit_bytes=vmem_bytes)


def _rms(x):
    return x * lax.rsqrt(jnp.mean(x * x, axis=-1, keepdims=True) + NORM_EPS)


def _ada_kernel(c_ref, w_ref, b_ref, o_ref):
    c = c_ref[...]
    s = (c * jax.nn.sigmoid(c)).astype(_BF16)
    o_ref[...] = jnp.dot(s, w_ref[...].astype(_BF16), preferred_element_type=_F32) + b_ref[...]


def _ada(c_all, w_ada, b_ada, tn=1024):
    rows, d = c_all.shape
    n = w_ada.shape[1]
    return pl.pallas_call(
        _ada_kernel,
        grid=(n // tn,),
        in_specs=[pl.BlockSpec((rows, d), lambda j: (0, 0)),
                  pl.BlockSpec((d, tn), lambda j: (0, j)),
                  pl.BlockSpec((1, tn), lambda j: (0, j))],
        out_specs=pl.BlockSpec((rows, tn), lambda j: (0, j)),
        out_shape=jax.ShapeDtypeStruct((rows, n), _F32),
        compiler_params=_params(("arbitrary",)),
        name="ada",
    )(c_all, w_ada, b_ada.reshape(1, n))


def _inproj_kernel(x_ref, shift_ref, scale_ref, g_ref, w_ref, cw_ref, st_ref,
                   q_ref, k_ref, v_ref, kb_ref, vt_ref, co_ref, cs_ref,
                   ug_scr, carry_scr, *, width, seg, tiles_per_batch, use_state, q_scale):
    i = pl.program_id(0)
    tm = x_ref.shape[0]
    h = _rms(x_ref[...]) * g_ref[...]
    h = (h * (1.0 + scale_ref[0]) + shift_ref[0]).astype(_BF16)

    def proj(j):
        return jnp.dot(h, w_ref[:, j * width:(j + 1) * width], preferred_element_type=_F32)

    q_ref[...] = (proj(0) * q_scale).astype(_BF16)
    zk = proj(1)
    k_ref[...] = zk
    kb_ref[...] = zk.astype(_BF16)
    zv = proj(2)
    v_ref[...] = zv
    vt_ref[...] = zv.T.astype(_BF16)

    ug = proj(3) * proj(5)
    gb = proj(4)
    w0 = cw_ref[0:1, :]
    w1 = cw_ref[1:2, :]
    w2 = cw_ref[2:3, :]
    if not use_state:
        @pl.when((i % tiles_per_batch) == 0)
        def _():
            carry_scr[...] = jnp.zeros_like(carry_scr)

    for s in range(tm // seg):
        lo = s * seg
        bnd = st_ref[s] if use_state else carry_scr[...]
        ug_s = ug[lo:lo + seg]
        ug_scr[6:8, :] = bnd
        ug_scr[8:8 + seg, :] = ug_s
        y = w2 * ug_s + w1 * ug_scr[7:7 + seg, :] + w0 * ug_scr[6:6 + seg, :]
        co_ref[lo:lo + seg, :] = (gb[lo:lo + seg] * y).astype(_BF16)
        last2 = ug_scr[6 + seg:8 + seg, :]
        cs_ref[s] = last2
        carry_scr[...] = last2


def _inproj(x2d, shift, scale, g_mix, w_in_bf, conv_w, state, *, tm, seg, tiles_per_batch,
            use_state, q_scale):
    r, d = x2d.shape
    width = w_in_bf.shape[1] // 6
    n_tiles = r // tm
    n_seg = tm // seg
    mod_rows = shift.shape[1]
    if use_state:
        mod_map = lambda i: (i, 0, 0)
        st_spec = pl.BlockSpec((n_seg, 2, width), lambda i: (i, 0, 0))
        cs_spec = pl.BlockSpec((n_seg, 2, width), lambda i: (i, 0, 0))
        n_state = n_tiles * n_seg
    else:
        mod_map = lambda i: (i // tiles_per_batch, 0, 0)
        st_spec = pl.BlockSpec((1, 2, width), lambda i: (0, 0, 0))
        cs_spec = pl.BlockSpec((1, 2, width), lambda i: (i // tiles_per_batch, 0, 0))
        n_state = n_tiles // tiles_per_batch
    kern = functools.partial(_inproj_kernel, width=width, seg=seg, tiles_per_batch=tiles_per_batch,
                             use_state=use_state, q_scale=q_scale)
    row_spec = lambda: pl.BlockSpec((tm, width), lambda i: (i, 0))
    return pl.pallas_call(
        kern,
        grid=(n_tiles,),
        in_specs=[pl.BlockSpec((tm, d), lambda i: (i, 0)),
                  pl.BlockSpec((1, mod_rows, d), mod_map),
                  pl.BlockSpec((1, mod_rows, d), mod_map),
                  pl.BlockSpec((1, d), lambda i: (0, 0)),
                  pl.BlockSpec((d, 6 * width), lambda i: (0, 0), pipeline_mode=pl.Buffered(1)),
                  pl.BlockSpec((3, width), lambda i: (0, 0)),
                  st_spec],
        out_specs=[row_spec(), row_spec(), row_spec(), row_spec(),
                   pl.BlockSpec((width, tm), lambda i: (0, i)),
                   row_spec(), cs_spec],
        out_shape=[jax.ShapeDtypeStruct((r, width), _BF16),
                   jax.ShapeDtypeStruct((r, width), _F32),
                   jax.ShapeDtypeStruct((r, width), _F32),
                   jax.ShapeDtypeStruct((r, width), _BF16),
                   jax.ShapeDtypeStruct((width, r), _BF16),
                   jax.ShapeDtypeStruct((r, width), _BF16),
                   jax.ShapeDtypeStruct((n_state, 2, width), _F32)],
        scratch_shapes=[pltpu.VMEM((seg + 8, width), _F32), pltpu.VMEM((2, width), _F32)],
        compiler_params=_params(("arbitrary",)),
        name="inproj",
    )(x2d, shift, scale, g_mix.reshape(1, d), w_in_bf, conv_w, state)


LOG2E = math.log2(math.e)
SKIP_MARGIN = 160.0
FAST_MARGIN = 60.0
N_AUG = 12


def _split3(v):
    t1 = v.astype(_BF16).astype(_F32)
    r = v - t1
    t2 = r.astype(_BF16).astype(_F32)
    t3 = (r - t2).astype(_BF16).astype(_F32)
    return t1, t2, t3


def _aug_group_term(idx):
    group = (idx >= 3).astype(jnp.int32) + (idx >= 6).astype(jnp.int32) + (idx >= 9).astype(jnp.int32)
    return group, idx - 3 * group


def _attn_prompt_kernel(scal_ref, q_ref, k_ref, vt_ref, g_ref, o_ref,
                        acc_scr, ml_scr, kn_scr, dbias_scr, obias_scr, aug_scr, *, tq, hd, out_scale):
    hh = pl.program_id(1)
    qi = pl.program_id(2)
    lam = scal_ref[0]
    slope2 = scal_ref[1 + hh]
    inv_slope2 = scal_ref[1 + N_HEADS + hh]
    tk = tq
    hw = 2 * hd
    n_kv_total = k_ref.shape[0] // tk

    @pl.when(qi == 0)
    def _():
        lane = lax.broadcasted_iota(jnp.int32, (tk, hw), 1)

        def chunk(c, mx):
            kk = k_ref[pl.ds(pl.multiple_of(c * tk, tk), tk), :].astype(_F32)
            sq = kk * kk
            n0 = jnp.sum(jnp.where(lane < hd, sq, 0.0), axis=1, keepdims=True)
            n1 = jnp.sum(jnp.where(lane >= hd, sq, 0.0), axis=1, keepdims=True)
            return jnp.maximum(mx[0], n0), jnp.maximum(mx[1], n1)

        zero = jnp.zeros((tk, 1), _F32)
        n0, n1 = lax.fori_loop(0, n_kv_total, chunk, (zero, zero))
        kn0 = jnp.sqrt(jnp.max(n0, axis=0, keepdims=True))
        kn1 = jnp.sqrt(jnp.max(n1, axis=0, keepdims=True))
        col = lax.broadcasted_iota(jnp.int32, (1, 2 * tq), 1)
        kn_scr[...] = jnp.where(col < tq, kn0, kn1) * 1.001

        jj = lax.broadcasted_iota(jnp.int32, (tk, tq), 0)
        ii = lax.broadcasted_iota(jnp.int32, (tk, tq), 1)
        jf = jj.astype(_F32)
        iif = ii.astype(_F32)
        visible = (jj // CHUNK) <= (ii // CHUNK)
        dbias = jnp.where(visible, slope2 * iif - slope2 * jnp.abs(iif - jf), MASKED_SCORE)
        dbias_scr[...] = jnp.concatenate([dbias, dbias], axis=1)
        obias = slope2 * jf
        obias_scr[...] = jnp.concatenate([obias, obias], axis=1)

        lc = lane & (hd - 1)
        group, _ = _aug_group_term(lc)
        jrow = lax.broadcasted_iota(jnp.int32, (tk, hw), 0)
        val = jnp.where(group == 0, jrow >> 4, jnp.where(group == 1, jrow & 15, jnp.where(group == 2, 0, 1)))
        aug_scr[...] = jnp.where(lc < N_AUG, val, 0).astype(_F32).astype(_BF16)

    qt = q_ref[...].astype(_F32).T
    qtb = qt.astype(_BF16)
    row2 = lax.broadcasted_iota(jnp.int32, (hw, 2 * tq), 0)
    col2 = lax.broadcasted_iota(jnp.int32, (hw, 2 * tq), 1)
    rhs_bd = jnp.where((row2 < hd) == (col2 < tq), jnp.concatenate([qtb, qtb], axis=1), 0).astype(_BF16)

    def kv_tiles(kv):
        start = pl.multiple_of(kv * tk, tk)
        return k_ref[pl.ds(start, tk), :], vt_ref[:, pl.ds(start, tk)]

    k, vt = kv_tiles(qi)
    s = jnp.dot(k, rhs_bd, preferred_element_type=_F32) + dbias_scr[...]
    m0 = jnp.max(s, axis=0, keepdims=True)
    p = jnp.exp2(s - m0)
    l0 = jnp.sum(p, axis=0, keepdims=True)
    acc_scr[...] = jnp.dot(vt, p.astype(_BF16), preferred_element_type=_F32)

    qsq = qt * qt
    qn = jnp.concatenate([jnp.sqrt(jnp.sum(qsq[:hd], axis=0, keepdims=True)),
                          jnp.sqrt(jnp.sum(qsq[hd:], axis=0, keepdims=True))], axis=1)
    gap = jnp.max(qn * kn_scr[...] - m0, axis=1, keepdims=True)
    reach = ((gap + SKIP_MARGIN) * inv_slope2 - 1.0) * (1.0 / tk)
    n_off = jnp.clip(jnp.floor(reach) + 1.0, 0.0, qi.astype(_F32)).astype(jnp.int32)[0, 0]
    fast = (gap <= FAST_MARGIN).astype(jnp.int32)[0, 0] == 1
    kv_lo = qi - n_off

    @pl.when(fast)
    def _():
        lane_lo = lax.broadcasted_iota(jnp.int32, (tk, hw), 1) < hd
        rowq = lax.broadcasted_iota(jnp.int32, (hw, tq), 0)

        def make_rhs(base, m_half, q_rows):
            local = rowq - base
            group, term = _aug_group_term(local)
            val = jnp.where(group == 0, slope2 * 16.0,
                            jnp.where(group == 1, slope2, jnp.where(group == 2, -slope2 * tk, -m_half)))
            t1, t2, t3 = _split3(val)
            v = jnp.where(term == 0, t1, jnp.where(term == 1, t2, t3))
            bias_rows = jnp.where((local >= 0) & (local < N_AUG), v, 0.0)
            return jnp.where(q_rows, qt, bias_rows).astype(_BF16)

        rhs0 = make_rhs(hd, m0[:, :tq], rowq < hd)
        rhs1 = make_rhs(0, m0[:, tq:], rowq >= hd)
        lcol = lax.broadcasted_iota(jnp.int32, (1, hw), 1) & (hd - 1)
        delta_cols = ((lcol >= 6) & (lcol < 9)).astype(_F32)

        def body(kv, l8):
            k, vt = kv_tiles(kv)
            delta = (qi - kv).astype(_F32)
            aug = aug_scr[...] + (delta * delta_cols).astype(_BF16)
            s = jnp.concatenate(
                [jnp.dot(jnp.where(lane_lo, k, aug), rhs0, preferred_element_type=_F32),
                 jnp.dot(jnp.where(lane_lo, aug, k), rhs1, preferred_element_type=_F32)], axis=1)
            p = jnp.exp2(s)
            l8 = l8 + jnp.sum(p.reshape(tk // 8, 8, 2 * tq), axis=0)
            acc_scr[...] = acc_scr[...] + jnp.dot(vt, p.astype(_BF16), preferred_element_type=_F32)
            return l8

        l8 = lax.fori_loop(kv_lo, qi, body, jnp.zeros((8, 2 * tq), _F32))
        ml_scr[...] = l0 + jnp.sum(l8, axis=0, keepdims=True)

    @pl.when(jnp.logical_not(fast))
    def _():
        def body(kv, carry):
            m, l = carry
            c = -slope2 * ((qi - kv) * tk).astype(_F32)
            k, vt = kv_tiles(kv)
            s = jnp.dot(k, rhs_bd, preferred_element_type=_F32) + obias_scr[...]
            m_new = jnp.maximum(m, jnp.max(s, axis=0, keepdims=True) + c)
            p = jnp.exp2(s - (m_new - c))
            alpha = jnp.exp2(m - m_new)
            l = alpha * l + jnp.sum(p, axis=0, keepdims=True)
            acc_scr[...] = acc_scr[...] * alpha + jnp.dot(vt, p.astype(_BF16), preferred_element_type=_F32)
            return m_new, l

        _, l = lax.fori_loop(kv_lo, qi, body, (m0, l0))
        ml_scr[...] = l

    acc = acc_scr[...] / ml_scr[...]
    o = acc[:, :tq] - lam * acc[:, tq:]
    o = o * lax.rsqrt(jnp.mean(o * o, axis=0, keepdims=True) + NORM_EPS)
    o = o * g_ref[...] * out_scale
    o_ref[...] = o.T.astype(_BF16)


def _attn_prompt(scal, q, kb, vt, subln_col, *, batch, seq, tq, out_scale):
    width = q.shape[1]
    hw = width // N_HEADS
    nq = seq // tq
    kern = functools.partial(_attn_prompt_kernel, tq=tq, hd=hw // 2, out_scale=out_scale)
    return pl.pallas_call(
        kern,
        grid=(batch, N_HEADS, nq),
        in_specs=[pl.BlockSpec(memory_space=pltpu.SMEM),
                  pl.BlockSpec((tq, hw), lambda b, h, i: (b * nq + i, h)),
                  pl.BlockSpec((seq, hw), lambda b, h, i: (b, h)),
                  pl.BlockSpec((hw, seq), lambda b, h, i: (h, b)),
                  pl.BlockSpec((hw, 1), lambda b, h, i: (0, 0))],
        out_specs=pl.BlockSpec((tq, hw), lambda b, h, i: (b * nq + i, h)),
        out_shape=jax.ShapeDtypeStruct((batch * seq, width), _BF16),
        scratch_shapes=[pltpu.VMEM((hw, 2 * tq), _F32),
                        pltpu.VMEM((1, 2 * tq), _F32),
                        pltpu.VMEM((1, 2 * tq), _F32),
                        pltpu.VMEM((tq, 2 * tq), _F32),
                        pltpu.VMEM((tq, 2 * tq), _F32),
                        pltpu.VMEM((tq, hw), _BF16)],
        compiler_params=_params(("arbitrary", "arbitrary", "arbitrary")),
        name="attn_prompt",
    )(scal, q, kb, vt, subln_col)


def _attn_sample_kernel(scal_ref, q_ref, kn_ref, vn_ref, kc_ref, vc_ref, g_ref, o_ref, *, hd, out_scale):
    hh = pl.program_id(1)
    lam = scal_ref[0]
    slope = scal_ref[1 + hh]
    t = q_ref.shape[0]
    past = kc_ref.shape[1]
    q = q_ref[...]
    kc = kc_ref[0].astype(_BF16)
    kn = kn_ref[...].astype(_BF16)
    vc = vc_ref[0].astype(_BF16)
    vn = vn_ref[...].astype(_BF16)
    qpos = lax.broadcasted_iota(jnp.int32, (t, past), 0).astype(_F32) + float(past)
    kpos = lax.broadcasted_iota(jnp.int32, (t, past), 1).astype(_F32)
    bias_c = -slope * jnp.abs(qpos - kpos)
    tq_i = lax.broadcasted_iota(jnp.int32, (t, t), 0).astype(_F32)
    tk_i = lax.broadcasted_iota(jnp.int32, (t, t), 1).astype(_F32)
    bias_n = -slope * jnp.abs(tq_i - tk_i)
    nt = (((1,), (1,)), ((), ()))
    outs = []
    for j in range(2):
        sl = slice(j * hd, (j + 1) * hd)
        sc = lax.dot_general(q[:, sl], kc[:, sl], nt, preferred_element_type=_F32) + bias_c
        sn = lax.dot_general(q[:, sl], kn[:, sl], nt, preferred_element_type=_F32) + bias_n
        m = jnp.maximum(jnp.max(sc, axis=-1, keepdims=True), jnp.max(sn, axis=-1, keepdims=True))
        pc = jnp.exp(sc - m)
        pn = jnp.exp(sn - m)
        l = jnp.sum(pc, axis=-1, keepdims=True) + jnp.sum(pn, axis=-1, keepdims=True)
        o = (jnp.dot(pc.astype(_BF16), vc, preferred_element_type=_F32)
             + jnp.dot(pn.astype(_BF16), vn, preferred_element_type=_F32))
        outs.append(o / l)
    o = outs[0] - lam * outs[1]
    o = _rms(o) * g_ref[...] * out_scale
    o_ref[...] = o.astype(_BF16)


def _attn_sample(scal, q, k_new, v_new, cache_k, cache_v, subln_row, *, batch, t, out_scale):
    width = q.shape[1]
    hw = width // N_HEADS
    past = cache_k.shape[1]
    kern = functools.partial(_attn_sample_kernel, hd=hw // 2, out_scale=out_scale)
    new_spec = lambda: pl.BlockSpec((t, hw), lambda b, h: (b, h))
    cache_spec = lambda: pl.BlockSpec((1, past, hw), lambda b, h: (b, 0, h))
    return pl.pallas_call(
        kern,
        grid=(batch, N_HEADS),
        in_specs=[pl.BlockSpec(memory_space=pltpu.SMEM), new_spec(), new_spec(), new_spec(),
                  cache_spec(), cache_spec(), pl.BlockSpec((1, hw), lambda b, h: (0, 0))],
        out_specs=new_spec(),
        out_shape=jax.ShapeDtypeStruct((batch * t, width), _BF16),
        compiler_params=_params(("arbitrary", "arbitrary")),
        name="attn_sample",
    )(scal, q, k_new, v_new, cache_k, cache_v, subln_row)


def _outproj_kernel(a_ref, c_ref, x_ref, gate_ref, shift_ref, scale_ref, g_ref, wo_ref, wrh_ref, wrl_ref,
                    br_ref, *rest, half, n_tiles):
    x1_ref, h2_ref, lg_ref = rest[-3:]
    i = pl.program_id(0)

    @pl.when(i < n_tiles)
    def _():
        mix = (jnp.dot(a_ref[...], wo_ref[0:half, :], preferred_element_type=_F32)
               + jnp.dot(c_ref[...], wo_ref[half:2 * half, :], preferred_element_type=_F32))
        x1 = x_ref[...] + gate_ref[0] * mix
        x1_ref[...] = x1
        h2 = _rms(x1) * g_ref[...]
        h2 = h2 * (1.0 + scale_ref[0]) + shift_ref[0]
        hi = h2.astype(_BF16)
        h2_ref[...] = hi.astype(_F32)
        lo = (h2 - hi.astype(_F32)).astype(_BF16)
        lg_ref[...] = (jnp.dot(hi, wrh_ref[...], preferred_element_type=_F32)
                       + jnp.dot(lo, wrh_ref[...], preferred_element_type=_F32)
                       + jnp.dot(hi, wrl_ref[...], preferred_element_type=_F32)) + br_ref[...]

    @pl.when(i >= n_tiles)
    def _():
        h2_ref[...] = jnp.zeros_like(h2_ref)
        lg_ref[...] = jnp.zeros_like(lg_ref)


def _outproj(attn, conv, x2d, gate, shift, scale, g_ffn, w_o_bf, w_router_pad, b_router_pad, *, tm,
             tiles_per_batch, per_row, rows_total, row_off, into=None):
    w_router_hi = w_router_pad.astype(_BF16)
    w_router_lo = (w_router_pad - w_router_hi.astype(_F32)).astype(_BF16)
    r, d = x2d.shape
    half = attn.shape[1]
    mod_rows = gate.shape[1]
    blk_off = row_off // tm
    assert row_off % tm == 0
    n_tiles = r // tm
    n_fill = (rows_total - row_off - r) // tm if into is None else 0
    last = n_tiles - 1
    row_map = lambda i: (jnp.minimum(i, last), 0)
    if per_row:
        mod_map = lambda i: (jnp.minimum(i, last), 0, 0)
    else:
        mod_map = lambda i: (jnp.minimum(i, last) // tiles_per_batch, 0, 0)
    mod_spec = lambda: pl.BlockSpec((1, mod_rows, d), mod_map)
    kern = functools.partial(_outproj_kernel, half=half, n_tiles=n_tiles)
    in_specs = [pl.BlockSpec((tm, half), row_map),
                pl.BlockSpec((tm, half), row_map),
                pl.BlockSpec((tm, d), row_map),
                mod_spec(), mod_spec(), mod_spec(),
                pl.BlockSpec((1, d), lambda i: (0, 0)),
                pl.BlockSpec((2 * half, d), lambda i: (0, 0), pipeline_mode=pl.Buffered(1)),
                pl.BlockSpec((d, ROUTER_LANES), lambda i: (0, 0)),
                pl.BlockSpec((d, ROUTER_LANES), lambda i: (0, 0)),
                pl.BlockSpec((1, ROUTER_LANES), lambda i: (0, 0))]
    args = [attn, conv, x2d, gate, shift, scale, g_ffn.reshape(1, d), w_o_bf, w_router_hi, w_router_lo,
            b_router_pad]
    aliases = {}
    if into is not None:
        aliases = {len(args): 1, len(args) + 1: 2}
        in_specs += [pl.BlockSpec(memory_space=pl.ANY), pl.BlockSpec(memory_space=pl.ANY)]
        args += list(into)
    return pl.pallas_call(
        kern,
        grid=(n_tiles + n_fill,),
        in_specs=in_specs,
        out_specs=[pl.BlockSpec((tm, d), row_map),
                   pl.BlockSpec((tm, d), lambda i: (blk_off + i, 0)),
                   pl.BlockSpec((tm, ROUTER_LANES), lambda i: (blk_off + i, 0))],
        out_shape=[jax.ShapeDtypeStruct((r, d), _F32),
                   jax.ShapeDtypeStruct((rows_total, d), _F32),
                   jax.ShapeDtypeStruct((rows_total, ROUTER_LANES), _F32)],
        input_output_aliases=aliases,
        compiler_params=_params(("arbitrary",)),
        name="outproj",
    )(*args)


def _row_gather_start(src_hbm, idx_ref, n_rows, dst_of_row, sem):
    def body(r, carry):
        pltpu.make_async_copy(src_hbm.at[pl.ds(idx_ref[0, 0, r], 1), :], dst_of_row(r), sem).start()
        return carry
    lax.fori_loop(0, n_rows, body, 0, unroll=8)


def _dispatch_kernel(nv_ref, idx_ref, idx_next_ref, h_hbm, o_ref, buf, sem):
    i = pl.program_id(0)
    nv = nv_ref[0]
    tm = o_ref.shape[0]
    slot = i % 2

    def start(idx_blk_ref, s):
        _row_gather_start(h_hbm, idx_blk_ref, tm, lambda r: buf.at[s, pl.ds(r, 1), :], sem.at[s])

    @pl.when(i == 0)
    def _():
        start(idx_ref, 0)

    @pl.when(i + 1 < nv)
    def _():
        start(idx_next_ref, 1 - slot)

    @pl.when(i < nv)
    def _():
        pltpu.make_async_copy(h_hbm.at[pl.ds(0, tm), :], buf.at[slot], sem.at[slot]).wait()
        o_ref[...] = buf[slot].astype(_BF16)

    @pl.when(i >= nv)
    def _():
        o_ref[...] = jnp.zeros_like(o_ref)


def _dispatch(n_valid, tok_sorted, h2_all, *, tm):
    rows = tok_sorted.shape[0]
    n_blocks = rows // tm
    d = h2_all.shape[1]
    idx3 = tok_sorted.reshape(n_blocks, 1, tm)
    return pl.pallas_call(
        _dispatch_kernel,
        grid_spec=pltpu.PrefetchScalarGridSpec(
            num_scalar_prefetch=1,
            grid=(n_blocks,),
            in_specs=[pl.BlockSpec((1, 1, tm), lambda i, nv: (i, 0, 0), memory_space=pltpu.SMEM),
                      pl.BlockSpec((1, 1, tm), lambda i, nv: (jnp.minimum(i + 1, n_blocks - 1), 0, 0),
                                   memory_space=pltpu.SMEM),
                      pl.BlockSpec(memory_space=pl.ANY)],
            out_specs=pl.BlockSpec((tm, d), lambda i, nv: (i, 0)),
            scratch_shapes=[pltpu.VMEM((2, tm, d), _F32), pltpu.SemaphoreType.DMA((2,))]),
        out_shape=jax.ShapeDtypeStruct((rows, d), _BF16),
        compiler_params=_params(("arbitrary",)),
        name="dispatch",
    )(n_valid, idx3, idx3, h2_all)


def _expert_changed(be_ref, i):
    return jnp.logical_or(i == 0, be_ref[i] != be_ref[jnp.maximum(i - 1, 0)])


def _moe_up_kernel(be_ref, nv_ref, x_ref, wg_ref, wu_ref, bg_ref, bu_ref, act_ref, wg_scr, wu_scr):
    i = pl.program_id(1)
    valid = i < nv_ref[0]

    @pl.when(jnp.logical_and(valid, _expert_changed(be_ref, i)))
    def _():
        wg_scr[...] = wg_ref[0].astype(_BF16)
        wu_scr[...] = wu_ref[0].astype(_BF16)

    @pl.when(valid)
    def _():
        x = x_ref[...]
        g = jnp.dot(x, wg_scr[...], preferred_element_type=_F32) + bg_ref[0]
        u = jnp.dot(x, wu_scr[...], preferred_element_type=_F32) + bu_ref[0]
        g = jnp.minimum(g, SWIGLU_LIMIT)
        u = jnp.clip(u, -SWIGLU_LIMIT, SWIGLU_LIMIT)
        act_ref[...] = ((u + 1.0) * g * jax.nn.sigmoid(SWIGLU_ALPHA * g)).astype(_BF16)

    @pl.when(jnp.logical_not(valid))
    def _():
        act_ref[...] = jnp.zeros_like(act_ref)


def _moe_down_kernel(be_ref, nv_ref, a_ref, wd_ref, bd_ref, y_ref, wd_scr):
    i = pl.program_id(1)
    valid = i < nv_ref[0]

    @pl.when(jnp.logical_and(valid, _expert_changed(be_ref, i)))
    def _():
        wd_scr[...] = wd_ref[0].astype(_BF16)

    @pl.when(valid)
    def _():
        y_ref[...] = jnp.dot(a_ref[...], wd_scr[...], preferred_element_type=_F32) + bd_ref[0]

    @pl.when(jnp.logical_not(valid))
    def _():
        y_ref[...] = jnp.zeros_like(y_ref)


def _moe(block_e, n_valid, xs, w_gu, b_gu, w_down, b_down, *, tm, tn):
    rows, d = xs.shape
    n_exp, _, two_ff = w_gu.shape
    d_ff = two_ff // 2
    n_blocks = rows // tm
    n_up = d_ff // tn
    b_gu3 = b_gu.reshape(n_exp, 1, two_ff)
    act = pl.pallas_call(
        _moe_up_kernel,
        grid_spec=pltpu.PrefetchScalarGridSpec(
            num_scalar_prefetch=2,
            grid=(n_up, n_blocks),
            in_specs=[pl.BlockSpec((tm, d), lambda n, i, be, nv: (i, 0)),
                      pl.BlockSpec((1, d, tn), lambda n, i, be, nv: (be[i], 0, n)),
                      pl.BlockSpec((1, d, tn), lambda n, i, be, nv: (be[i], 0, n_up + n)),
                      pl.BlockSpec((1, 1, tn), lambda n, i, be, nv: (be[i], 0, n)),
                      pl.BlockSpec((1, 1, tn), lambda n, i, be, nv: (be[i], 0, n_up + n))],
            out_specs=pl.BlockSpec((tm, tn), lambda n, i, be, nv: (i, n)),
            scratch_shapes=[pltpu.VMEM((d, tn), _BF16), pltpu.VMEM((d, tn), _BF16)]),
        out_shape=jax.ShapeDtypeStruct((rows, d_ff), _BF16),
        compiler_params=_params(("arbitrary", "arbitrary")),
        name="moe_up",
    )(block_e, n_valid, xs, w_gu, w_gu, b_gu3, b_gu3)
    n_down = d // tn
    return pl.pallas_call(
        _moe_down_kernel,
        grid_spec=pltpu.PrefetchScalarGridSpec(
            num_scalar_prefetch=2,
            grid=(n_down, n_blocks),
            in_specs=[pl.BlockSpec((tm, d_ff), lambda n, i, be, nv: (i, 0)),
                      pl.BlockSpec((1, d_ff, tn), lambda n, i, be, nv: (be[i], 0, n)),
                      pl.BlockSpec((1, 1, tn), lambda n, i, be, nv: (be[i], 0, n))],
            out_specs=pl.BlockSpec((tm, tn), lambda n, i, be, nv: (i, n)),
            scratch_shapes=[pltpu.VMEM((d_ff, tn), _BF16)]),
        out_shape=jax.ShapeDtypeStruct((rows, d), _F32),
        compiler_params=_params(("arbitrary", "arbitrary")),
        name="moe_down",
    )(block_e, n_valid, act, w_down, b_down.reshape(n_exp, 1, d))


def _combine_kernel(idx_ref, idx_next_ref, x1_ref, gk_ref, gate_ref, g_ref, ys_hbm, y_ref, buf, sem):
    i = pl.program_id(0)
    n = pl.num_programs(0)
    tm, d = x1_ref.shape
    slot = i % 2

    def start(idx_blk_ref, s):
        _row_gather_start(ys_hbm, idx_blk_ref, tm * TOP_K,
                          lambda r: buf.at[s, r % TOP_K, pl.ds(r // TOP_K, 1), :], sem.at[s])

    @pl.when(i == 0)
    def _():
        start(idx_ref, 0)

    @pl.when(i + 1 < n)
    def _():
        start(idx_next_ref, 1 - slot)

    for k in range(TOP_K):
        pltpu.make_async_copy(ys_hbm.at[pl.ds(0, tm), :], buf.at[slot, k], sem.at[slot]).wait()
    gk = gk_ref[...]
    ff = gk[:, 0:1] * buf[slot, 0]
    for k in range(1, TOP_K):
        ff = ff + gk[:, k:k + 1] * buf[slot, k]
    x2 = x1_ref[...] + gate_ref[0] * ff
    y_ref[...] = _rms(x2) * g_ref[...]


def _combine(x1, ys, dest, gk, gate, g_final, *, tm, tiles_per_batch, per_row):
    r, d = x1.shape
    n_tiles = r // tm
    mod_rows = gate.shape[1]
    mod_map = (lambda i: (i, 0, 0)) if per_row else (lambda i: (i // tiles_per_batch, 0, 0))
    idx3 = dest.reshape(n_tiles, 1, tm * TOP_K)
    return pl.pallas_call(
        _combine_kernel,
        grid=(n_tiles,),
        in_specs=[pl.BlockSpec((1, 1, tm * TOP_K), lambda i: (i, 0, 0), memory_space=pltpu.SMEM),
                  pl.BlockSpec((1, 1, tm * TOP_K), lambda i: (jnp.minimum(i + 1, n_tiles - 1), 0, 0),
                               memory_space=pltpu.SMEM),
                  pl.BlockSpec((tm, d), lambda i: (i, 0)),
                  pl.BlockSpec((tm, TOP_K), lambda i: (i, 0)),
                  pl.BlockSpec((1, mod_rows, d), mod_map),
                  pl.BlockSpec((1, d), lambda i: (0, 0)),
                  pl.BlockSpec(memory_space=pl.ANY)],
        out_specs=pl.BlockSpec((tm, d), lambda i: (i, 0)),
        out_shape=jax.ShapeDtypeStruct((r, d), _F32),
        scratch_shapes=[pltpu.VMEM((2, TOP_K, tm, d), _F32), pltpu.SemaphoreType.DMA((2,))],
        compiler_params=_params(("arbitrary",)),
        name="combine",
    )(idx3, idx3, x1, gk, gate, g_final.reshape(1, d), ys)


def _route(logits, n_exp, tm):
    t = logits.shape[0]
    top_v, top_e = lax.top_k(logits, TOP_K)
    gate = jax.nn.softmax(top_v, axis=-1)
    m = t * TOP_K
    n_blocks = -(-m // tm) + n_exp
    rows = n_blocks * tm
    i32 = jnp.int32
    flat_e = top_e.reshape(-1).astype(i32)
    sorted_e, order = lax.sort((flat_e, jnp.arange(m, dtype=i32)), num_keys=1)
    counts = jnp.sum((flat_e[:, None] == jnp.arange(n_exp, dtype=i32)[None, :]).astype(i32), axis=0)
    padded = ((counts + tm - 1) // tm) * tm
    cum_end = jnp.cumsum(padded)
    pstart = cum_end - padded
    cstart = jnp.cumsum(counts) - counts
    n_valid = (cum_end[-1] // tm).astype(i32).reshape(1)
    blk_start = jnp.arange(n_blocks, dtype=i32) * tm
    block_e = jnp.minimum(jnp.searchsorted(cum_end, blk_start, side='right'), n_exp - 1).astype(i32)
    last_e = block_e[jnp.maximum(n_valid[0] - 1, 0)]
    block_e = jnp.where(jnp.arange(n_blocks) < n_valid[0], block_e, last_e)
    row = jnp.arange(rows, dtype=i32)
    e_row = jnp.repeat(block_e, tm)
    off = row - pstart[e_row]
    is_slot = (off < counts[e_row]) & (row < cum_end[-1])
    src = jnp.clip(cstart[e_row] + off, 0, m - 1)
    tok_sorted = jnp.where(is_slot, order[src] // TOP_K, 0).astype(i32)
    dest_sorted = pstart[sorted_e] + jnp.arange(m, dtype=i32) - cstart[sorted_e]
    _, dest = lax.sort((order, dest_sorted.astype(i32)), num_keys=1)
    return gate, dest.reshape(t, TOP_K), tok_sorted, block_e, n_valid


def _pick_tile(n, pref):
    t = min(n, pref)
    while n % t:
        t //= 2
    return t


def kernel(x_prompt, x_sample, cache_k, cache_v, state_conv, c_prompt, c_sample, g_mix, g_ffn, w_ada, b_ada, w_in, lambda_q1, lambda_k1, lambda_q2, lambda_k2, subln_g, conv_w, w_o, w_router, b_router, w_gu, b_gu, w_down, b_down, g_final):
    depth = g_mix.shape[0]
    assert depth == 1
    bp, sp, d = x_prompt.shape
    bs, ts, _ = x_sample.shape
    past = cache_k.shape[2]
    aw = d // 2
    hw = aw // N_HEADS
    hd = hw // 2
    n_exp = w_router.shape[-1]
    layer = 0

    n_c = bp + bs
    c_rows = -(-n_c // 8) * 8
    c_all = jnp.concatenate([c_prompt, c_sample, jnp.zeros((c_rows - n_c, d), _F32)], axis=0)
    ada = _ada(c_all, w_ada[layer], b_ada[layer])
    shift1, scale1, gate1, shift2, scale2, gate2 = [ada[:, i * d:(i + 1) * d] for i in range(6)]

    lam_init = 0.8 - 0.6 * math.exp(-0.3 * layer)
    lam = (jnp.exp(jnp.sum(lambda_q1[layer] * lambda_k1[layer]))
           - jnp.exp(jnp.sum(lambda_q2[layer] * lambda_k2[layer])) + lam_init)
    slopes = 2.0 ** (-8.0 * np.arange(1, N_HEADS + 1) / N_HEADS)
    lam1 = lam.reshape(1).astype(_F32)
    scal_s = jnp.concatenate([lam1, jnp.asarray(slopes, _F32)])
    scal_p = jnp.concatenate([lam1, jnp.asarray(slopes * LOG2E, _F32), jnp.asarray(1.0 / (slopes * LOG2E), _F32)])
    out_scale = 1.0 - lam_init
    q_scale = hd ** -0.5

    w_in_bf = w_in[layer].astype(_BF16)
    w_o_bf = w_o[layer].astype(_BF16)
    w_router_pad = jnp.zeros((d, ROUTER_LANES), _F32).at[:, :n_exp].set(w_router[layer])
    b_router_pad = jnp.zeros((1, ROUTER_LANES), _F32).at[0, :n_exp].set(b_router[layer])

    rp = bp * sp
    tm_p = _pick_tile(sp, 512)
    tpb = sp // tm_p
    xp2 = x_prompt.reshape(rp, d)
    mod_p = lambda a: a[:bp].reshape(bp, 1, d)
    qp, kp, vp, kbp, vtp, cop, csp = _inproj(
        xp2, mod_p(shift1), mod_p(scale1), g_mix[layer], w_in_bf, conv_w[layer],
        jnp.zeros((1, 2, aw), _F32), tm=tm_p, seg=tm_p, tiles_per_batch=tpb, use_state=False,
        q_scale=q_scale * LOG2E)
    tq = _pick_tile(sp, 512)
    ap = _attn_prompt(scal_p, qp, kbp, vtp, subln_g[layer].reshape(hw, 1), batch=bp, seq=sp, tq=tq,
                      out_scale=out_scale)
    rs = bs * ts
    x1p, h2_all, lg_all = _outproj(ap, cop, xp2, mod_p(gate1), mod_p(shift2), mod_p(scale2), g_ffn[layer],
                                   w_o_bf, w_router_pad, b_router_pad, tm=tm_p, tiles_per_batch=tpb,
                                   per_row=False, rows_total=rp + rs, row_off=0)

    mod_s = lambda a: jnp.repeat(a[bp:bp + bs], ts, axis=0).reshape(1, rs, d)
    xs2 = x_sample.reshape(rs, d)
    qs, ks, vs, _, _, cos, css = _inproj(
        xs2, mod_s(shift1), mod_s(scale1), g_mix[layer], w_in_bf, conv_w[layer],
        state_conv[layer], tm=rs, seg=ts, tiles_per_batch=1, use_state=True, q_scale=q_scale)
    a_s = _attn_sample(scal_s, qs, ks, vs, cache_k[layer].reshape(bs, past, aw),
                       cache_v[layer].reshape(bs, past, aw), subln_g[layer].reshape(1, hw),
                       batch=bs, t=ts, out_scale=out_scale)
    x1s, h2_all, lg_all = _outproj(a_s, cos, xs2, mod_s(gate1), mod_s(shift2), mod_s(scale2), g_ffn[layer],
                                   w_o_bf, w_router_pad, b_router_pad, tm=rs, tiles_per_batch=1,
                                   per_row=True, rows_total=rp + rs, row_off=rp, into=(h2_all, lg_all))

    tm_e = 512
    gate, dest, tok_sorted, block_e, n_valid = _route(lg_all[:, :n_exp], n_exp, tm_e)
    xs_sorted = _dispatch(n_valid, tok_sorted, h2_all, tm=tm_e)
    d_ff = w_down.shape[2]
    ys = _moe(block_e, n_valid, xs_sorted, w_gu[layer], b_gu[layer], w_down[layer], b_down[layer],
              tm=tm_e, tn=_pick_tile(d_ff, 1024))

    tm_c = _pick_tile(sp, 128)
    y_p = _combine(x1p, ys, dest[:rp], gate[:rp], mod_p(gate2), g_final, tm=tm_c,
                   tiles_per_batch=sp // tm_c, per_row=False)
    tm_cs = _pick_tile(rs, 128)
    gate2_s = mod_s(gate2).reshape(rs // tm_cs, tm_cs, d)
    y_s = _combine(x1s, ys, dest[rp:], gate[rp:], gate2_s, g_final, tm=tm_cs, tiles_per_batch=1,
                   per_row=True)

    return (y_p.reshape(bp, sp, d), y_s.reshape(bs, ts, d),
            kp.reshape(1, bp, sp, N_HEADS, 2, hd), vp.reshape(1, bp, sp, N_HEADS, hw),
            csp.reshape(1, bp, 2, aw),
            ks.reshape(1, bs, ts, N_HEADS, 2, hd), vs.reshape(1, bs, ts, N_HEADS, hw),
            css.reshape(1, bs, 2, aw))
```

```python
import functools
import math

import jax
import jax.numpy as jnp
import numpy as np
from jax import lax
from jax.experimental import pallas as pl
from jax.experimental.pallas import tpu as pltpu

_F32 = jnp.float32
_BF16 = jnp.bfloat16

CHUNK = 64
N_HEADS = 8
TOP_K = 4
NORM_EPS = 1e-6
SWIGLU_ALPHA = 1.702
SWIGLU_LIMIT = 7.0
MASKED_SCORE = -1e30
ROUTER_LANES = 128

V7X_VMEM_LIMIT_BYTES = 56 * 1024 * 1024


def _params(semantics, vmem_bytes=V7X_VMEM_LIMIT_BYTES):
    return pltpu.CompilerParams(dimension_semantics=semantics, vmem_limit_bytes=vmem_bytes)


def _rms(x):
    return x * lax.rsqrt(jnp.mean(x * x, axis=-1, keepdims=True) + NORM_EPS)


def _ada_kernel(c_ref, w_ref, b_ref, o_ref):
    c = c_ref[...]
    s = (c * jax.nn.sigmoid(c)).astype(_BF16)
    o_ref[...] = jnp.dot(s, w_ref[...].astype(_BF16), preferred_element_type=_F32) + b_ref[...]


def _ada(c_all, w_ada, b_ada, tn=1024):
    rows, d = c_all.shape
    n = w_ada.shape[1]
    return pl.pallas_call(
        _ada_kernel,
        grid=(n // tn,),
        in_specs=[pl.BlockSpec((rows, d), lambda j: (0, 0)),
                  pl.BlockSpec((d, tn), lambda j: (0, j)),
                  pl.BlockSpec((1, tn), lambda j: (0, j))],
        out_specs=pl.BlockSpec((rows, tn), lambda j: (0, j)),
        out_shape=jax.ShapeDtypeStruct((rows, n), _F32),
        compiler_params=_params(("arbitrary",)),
        name="ada",
    )(c_all, w_ada, b_ada.reshape(1, n))


def _inproj_kernel(x_ref, shift_ref, scale_ref, g_ref, w_ref, cw_ref, st_ref,
                   q_ref, k_ref, v_ref, kb_ref, vt_ref, co_ref, cs_ref,
                   ug_scr, carry_scr, *, width, seg, tiles_per_batch, use_state, q_scale):
    i = pl.program_id(0)
    tm = x_ref.shape[0]
    h = _rms(x_ref[...]) * g_ref[...]
    h = (h * (1.0 + scale_ref[0]) + shift_ref[0]).astype(_BF16)

    def proj(j):
        return jnp.dot(h, w_ref[:, j * width:(j + 1) * width], preferred_element_type=_F32)

    q_ref[...] = (proj(0) * q_scale).astype(_BF16)
    zk = proj(1)
    k_ref[...] = zk
    kb_ref[...] = zk.astype(_BF16)
    zv = proj(2)
    v_ref[...] = zv
    vt_ref[...] = zv.T.astype(_BF16)

    ug = proj(3) * proj(5)
    gb = proj(4)
    w0 = cw_ref[0:1, :]
    w1 = cw_ref[1:2, :]
    w2 = cw_ref[2:3, :]
    if not use_state:
        @pl.when((i % tiles_per_batch) == 0)
        def _():
            carry_scr[...] = jnp.zeros_like(carry_scr)

    for s in range(tm // seg):
        lo = s * seg
        bnd = st_ref[s] if use_state else carry_scr[...]
        ug_s = ug[lo:lo + seg]
        ug_scr[6:8, :] = bnd
        ug_scr[8:8 + seg, :] = ug_s
        y = w2 * ug_s + w1 * ug_scr[7:7 + seg, :] + w0 * ug_scr[6:6 + seg, :]
        co_ref[lo:lo + seg, :] = (gb[lo:lo + seg] * y).astype(_BF16)
        last2 = ug_scr[6 + seg:8 + seg, :]
        cs_ref[s] = last2
        carry_scr[...] = last2


def _inproj(x2d, shift, scale, g_mix, w_in_bf, conv_w, state, *, tm, seg, tiles_per_batch,
            use_state, q_scale):
    r, d = x2d.shape
    width = w_in_bf.shape[1] // 6
    n_tiles = r // tm
    n_seg = tm // seg
    mod_rows = shift.shape[1]
    if use_state:
        mod_map = lambda i: (i, 0, 0)
        st_spec = pl.BlockSpec((n_seg, 2, width), lambda i: (i, 0, 0))
        cs_spec = pl.BlockSpec((n_seg, 2, width), lambda i: (i, 0, 0))
        n_state = n_tiles * n_seg
    else:
        mod_map = lambda i: (i // tiles_per_batch, 0, 0)
        st_spec = pl.BlockSpec((1, 2, width), lambda i: (0, 0, 0))
        cs_spec = pl.BlockSpec((1, 2, width), lambda i: (i // tiles_per_batch, 0, 0))
        n_state = n_tiles // tiles_per_batch
    kern = functools.partial(_inproj_kernel, width=width, seg=seg, tiles_per_batch=tiles_per_batch,
                             use_state=use_state, q_scale=q_scale)
    row_spec = lambda: pl.BlockSpec((tm, width), lambda i: (i, 0))
    return pl.pallas_call(
        kern,
        grid=(n_tiles,),
        in_specs=[pl.BlockSpec((tm, d), lambda i: (i, 0)),
                  pl.BlockSpec((1, mod_rows, d), mod_map),
                  pl.BlockSpec((1, mod_rows, d), mod_map),
                  pl.BlockSpec((1, d), lambda i: (0, 0)),
                  pl.BlockSpec((d, 6 * width), lambda i: (0, 0), pipeline_mode=pl.Buffered(1)),
                  pl.BlockSpec((3, width), lambda i: (0, 0)),
                  st_spec],
        out_specs=[row_spec(), row_spec(), row_spec(), row_spec(),
                   pl.BlockSpec((width, tm), lambda i: (0, i)),
                   row_spec(), cs_spec],
        out_shape=[jax.ShapeDtypeStruct((r, width), _BF16),
                   jax.ShapeDtypeStruct((r, width), _F32),
                   jax.ShapeDtypeStruct((r, width), _F32),
                   jax.ShapeDtypeStruct((r, width), _BF16),
                   jax.ShapeDtypeStruct((width, r), _BF16),
                   jax.ShapeDtypeStruct((r, width), _BF16),
                   jax.ShapeDtypeStruct((n_state, 2, width), _F32)],
        scratch_shapes=[pltpu.VMEM((seg + 8, width), _F32), pltpu.VMEM((2, width), _F32)],
        compiler_params=_params(("arbitrary",)),
        name="inproj",
    )(x2d, shift, scale, g_mix.reshape(1, d), w_in_bf, conv_w, state)


LOG2E = math.log2(math.e)
SKIP_MARGIN = 160.0
FAST_MARGIN = 60.0
N_AUG = 12


def _split3(v):
    t1 = v.astype(_BF16).astype(_F32)
    r = v - t1
    t2 = r.astype(_BF16).astype(_F32)
    t3 = (r - t2).astype(_BF16).astype(_F32)
    return t1, t2, t3


def _aug_group_term(idx):
    group = (idx >= 3).astype(jnp.int32) + (idx >= 6).astype(jnp.int32) + (idx >= 9).astype(jnp.int32)
    return group, idx - 3 * group


def _attn_prompt_kernel(scal_ref, q_ref, k_ref, vt_ref, g_ref, o_ref,
                        acc_scr, ml_scr, kn_scr, dbias_scr, obias_scr, aug_scr, *, tq, hd, out_scale):
    hh = pl.program_id(1)
    qi = pl.program_id(2)
    lam = scal_ref[0]
    slope2 = scal_ref[1 + hh]
    inv_slope2 = scal_ref[1 + N_HEADS + hh]
    tk = tq
    hw = 2 * hd
    n_kv_total = k_ref.shape[0] // tk

    @pl.when(qi == 0)
    def _():
        lane = lax.broadcasted_iota(jnp.int32, (tk, hw), 1)

        def chunk(c, mx):
            kk = k_ref[pl.ds(pl.multiple_of(c * tk, tk), tk), :].astype(_F32)
            sq = kk * kk
            n0 = jnp.sum(jnp.where(lane < hd, sq, 0.0), axis=1, keepdims=True)
            n1 = jnp.sum(jnp.where(lane >= hd, sq, 0.0), axis=1, keepdims=True)
            return jnp.maximum(mx[0], n0), jnp.maximum(mx[1], n1)

        zero = jnp.zeros((tk, 1), _F32)
        n0, n1 = lax.fori_loop(0, n_kv_total, chunk, (zero, zero))
        kn0 = jnp.sqrt(jnp.max(n0, axis=0, keepdims=True))
        kn1 = jnp.sqrt(jnp.max(n1, axis=0, keepdims=True))
        col = lax.broadcasted_iota(jnp.int32, (1, 2 * tq), 1)
        kn_scr[...] = jnp.where(col < tq, kn0, kn1) * 1.001

        jj = lax.broadcasted_iota(jnp.int32, (tk, tq), 0)
        ii = lax.broadcasted_iota(jnp.int32, (tk, tq), 1)
        jf = jj.astype(_F32)
        iif = ii.astype(_F32)
        visible = (jj // CHUNK) <= (ii // CHUNK)
        dbias = jnp.where(visible, slope2 * iif - slope2 * jnp.abs(iif - jf), MASKED_SCORE)
        dbias_scr[...] = jnp.concatenate([dbias, dbias], axis=1)
        obias = slope2 * jf
        obias_scr[...] = jnp.concatenate([obias, obias], axis=1)

        lc = lane & (hd - 1)
        group, _ = _aug_group_term(lc)
        jrow = lax.broadcasted_iota(jnp.int32, (tk, hw), 0)
        val = jnp.where(group == 0, jrow >> 4, jnp.where(group == 1, jrow & 15, jnp.where(group == 2, 0, 1)))
        aug_scr[...] = jnp.where(lc < N_AUG, val, 0).astype(_F32).astype(_BF16)

    qt = q_ref[...].astype(_F32).T
    qtb = qt.astype(_BF16)
    row2 = lax.broadcasted_iota(jnp.int32, (hw, 2 * tq), 0)
    col2 = lax.broadcasted_iota(jnp.int32, (hw, 2 * tq), 1)
    rhs_bd = jnp.where((row2 < hd) == (col2 < tq), jnp.concatenate([qtb, qtb], axis=1), 0).astype(_BF16)

    def kv_tiles(kv):
        start = pl.multiple_of(kv * tk, tk)
        return k_ref[pl.ds(start, tk), :], vt_ref[:, pl.ds(start, tk)]

    k, vt = kv_tiles(qi)
    s = jnp.dot(k, rhs_bd, preferred_element_type=_F32) + dbias_scr[...]
    m0 = jnp.max(s, axis=0, keepdims=True)
    p = jnp.exp2(s - m0)
    l0 = jnp.sum(p, axis=0, keepdims=True)
    acc_scr[...] = jnp.dot(vt, p.astype(_BF16), preferred_element_type=_F32)

    qsq = qt * qt
    qn = jnp.concatenate([jnp.sqrt(jnp.sum(qsq[:hd], axis=0, keepdims=True)),
                          jnp.sqrt(jnp.sum(qsq[hd:], axis=0, keepdims=True))], axis=1)
    gap = jnp.max(qn * kn_scr[...] - m0, axis=1, keepdims=True)
    reach = ((gap + SKIP_MARGIN) * inv_slope2 - 1.0) * (1.0 / tk)
    n_off = jnp.clip(jnp.floor(reach) + 1.0, 0.0, qi.astype(_F32)).astype(jnp.int32)[0, 0]
    fast = (gap <= FAST_MARGIN).astype(jnp.int32)[0, 0] == 1
    kv_lo = qi - n_off

    @pl.when(fast)
    def _():
        lane_lo = lax.broadcasted_iota(jnp.int32, (tk, hw), 1) < hd
        rowq = lax.broadcasted_iota(jnp.int32, (hw, tq), 0)

        def make_rhs(base, m_half, q_rows):
            local = rowq - base
            group, term = _aug_group_term(local)
            val = jnp.where(group == 0, slope2 * 16.0,
                            jnp.where(group == 1, slope2, jnp.where(group == 2, -slope2 * tk, -m_half)))
            t1, t2, t3 = _split3(val)
            v = jnp.where(term == 0, t1, jnp.where(term == 1, t2, t3))
            bias_rows = jnp.where((local >= 0) & (local < N_AUG), v, 0.0)
            return jnp.where(q_rows, qt, bias_rows).astype(_BF16)

        rhs0 = make_rhs(hd, m0[:, :tq], rowq < hd)
        rhs1 = make_rhs(0, m0[:, tq:], rowq >= hd)
        lcol = lax.broadcasted_iota(jnp.int32, (1, hw), 1) & (hd - 1)
        delta_cols = ((lcol >= 6) & (lcol < 9)).astype(_F32)

        def body(kv, l8):
            k, vt = kv_tiles(kv)
            delta = (qi - kv).astype(_F32)
            aug = aug_scr[...] + (delta * delta_cols).astype(_BF16)
            s = jnp.concatenate(
                [jnp.dot(jnp.where(lane_lo, k, aug), rhs0, preferred_element_type=_F32),
                 jnp.dot(jnp.where(lane_lo, aug, k), rhs1, preferred_element_type=_F32)], axis=1)
            p = jnp.exp2(s)
            l8 = l8 + jnp.sum(p.reshape(tk // 8, 8, 2 * tq), axis=0)
            acc_scr[...] = acc_scr[...] + jnp.dot(vt, p.astype(_BF16), preferred_element_type=_F32)
            return l8

        l8 = lax.fori_loop(kv_lo, qi, body, jnp.zeros((8, 2 * tq), _F32))
        ml_scr[...] = l0 + jnp.sum(l8, axis=0, keepdims=True)

    @pl.when(jnp.logical_not(fast))
    def _():
        def body(kv, carry):
            m, l = carry
            c = -slope2 * ((qi - kv) * tk).astype(_F32)
            k, vt = kv_tiles(kv)
            s = jnp.dot(k, rhs_bd, preferred_element_type=_F32) + obias_scr[...]
            m_new = jnp.maximum(m, jnp.max(s, axis=0, keepdims=True) + c)
            p = jnp.exp2(s - (m_new - c))
            alpha = jnp.exp2(m - m_new)
            l = alpha * l + jnp.sum(p, axis=0, keepdims=True)
            acc_scr[...] = acc_scr[...] * alpha + jnp.dot(vt, p.astype(_BF16), preferred_element_type=_F32)
            return m_new, l

        _, l = lax.fori_loop(kv_lo, qi, body, (m0, l0))
        ml_scr[...] = l

    acc = acc_scr[...] / ml_scr[...]
    o = acc[:, :tq] - lam * acc[:, tq:]
    o = o * lax.rsqrt(jnp.mean(o * o, axis=0, keepdims=True) + NORM_EPS)
    o = o * g_ref[...] * out_scale
    o_ref[...] = o.T.astype(_BF16)


def _attn_prompt(scal, q, kb, vt, subln_col, *, batch, seq, tq, out_scale):
    width = q.shape[1]
    hw = width // N_HEADS
    nq = seq // tq
    kern = functools.partial(_attn_prompt_kernel, tq=tq, hd=hw // 2, out_scale=out_scale)
    return pl.pallas_call(
        kern,
        grid=(batch, N_HEADS, nq),
        in_specs=[pl.BlockSpec(memory_space=pltpu.SMEM),
                  pl.BlockSpec((tq, hw), lambda b, h, i: (b * nq + i, h)),
                  pl.BlockSpec((seq, hw), lambda b, h, i: (b, h)),
                  pl.BlockSpec((hw, seq), lambda b, h, i: (h, b)),
                  pl.BlockSpec((hw, 1), lambda b, h, i: (0, 0))],
        out_specs=pl.BlockSpec((tq, hw), lambda b, h, i: (b * nq + i, h)),
        out_shape=jax.ShapeDtypeStruct((batch * seq, width), _BF16),
        scratch_shapes=[pltpu.VMEM((hw, 2 * tq), _F32),
                        pltpu.VMEM((1, 2 * tq), _F32),
                        pltpu.VMEM((1, 2 * tq), _F32),
                        pltpu.VMEM((tq, 2 * tq), _F32),
                        pltpu.VMEM((tq, 2 * tq), _F32),
                        pltpu.VMEM((tq, hw), _BF16)],
        compiler_params=_params(("arbitrary", "arbitrary", "arbitrary")),
        name="attn_prompt",
    )(scal, q, kb, vt, subln_col)


def _attn_sample_kernel(scal_ref, q_ref, kn_ref, vn_ref, kc_ref, vc_ref, g_ref, o_ref, *, hd, out_scale):
    hh = pl.program_id(1)
    lam = scal_ref[0]
    slope = scal_ref[1 + hh]
    t = q_ref.shape[0]
    past = kc_ref.shape[1]
    q = q_ref[...]
    kc = kc_ref[0].astype(_BF16)
    kn = kn_ref[...].astype(_BF16)
    vc = vc_ref[0].astype(_BF16)
    vn = vn_ref[...].astype(_BF16)
    qpos = lax.broadcasted_iota(jnp.int32, (t, past), 0).astype(_F32) + float(past)
    kpos = lax.broadcasted_iota(jnp.int32, (t, past), 1).astype(_F32)
    bias_c = -slope * jnp.abs(qpos - kpos)
    tq_i = lax.broadcasted_iota(jnp.int32, (t, t), 0).astype(_F32)
    tk_i = lax.broadcasted_iota(jnp.int32, (t, t), 1).astype(_F32)
    bias_n = -slope * jnp.abs(tq_i - tk_i)
    nt = (((1,), (1,)), ((), ()))
    outs = []
    for j in range(2):
        sl = slice(j * hd, (j + 1) * hd)
        sc = lax.dot_general(q[:, sl], kc[:, sl], nt, preferred_element_type=_F32) + bias_c
        sn = lax.dot_general(q[:, sl], kn[:, sl], nt, preferred_element_type=_F32) + bias_n
        m = jnp.maximum(jnp.max(sc, axis=-1, keepdims=True), jnp.max(sn, axis=-1, keepdims=True))
        pc = jnp.exp(sc - m)
        pn = jnp.exp(sn - m)
        l = jnp.sum(pc, axis=-1, keepdims=True) + jnp.sum(pn, axis=-1, keepdims=True)
        o = (jnp.dot(pc.astype(_BF16), vc, preferred_element_type=_F32)
             + jnp.dot(pn.astype(_BF16), vn, preferred_element_type=_F32))
        outs.append(o / l)
    o = outs[0] - lam * outs[1]
    o = _rms(o) * g_ref[...] * out_scale
    o_ref[...] = o.astype(_BF16)


def _attn_sample(scal, q, k_new, v_new, cache_k, cache_v, subln_row, *, batch, t, out_scale):
    width = q.shape[1]
    hw = width // N_HEADS
    past = cache_k.shape[1]
    kern = functools.partial(_attn_sample_kernel, hd=hw // 2, out_scale=out_scale)
    new_spec = lambda: pl.BlockSpec((t, hw), lambda b, h: (b, h))
    cache_spec = lambda: pl.BlockSpec((1, past, hw), lambda b, h: (b, 0, h))
    return pl.pallas_call(
        kern,
        grid=(batch, N_HEADS),
        in_specs=[pl.BlockSpec(memory_space=pltpu.SMEM), new_spec(), new_spec(), new_spec(),
                  cache_spec(), cache_spec(), pl.BlockSpec((1, hw), lambda b, h: (0, 0))],
        out_specs=new_spec(),
        out_shape=jax.ShapeDtypeStruct((batch * t, width), _BF16),
        compiler_params=_params(("arbitrary", "arbitrary")),
        name="attn_sample",
    )(scal, q, k_new, v_new, cache_k, cache_v, subln_row)


def _outproj_kernel(a_ref, c_ref, x_ref, gate_ref, shift_ref, scale_ref, g_ref, wo_ref, wrh_ref, wrl_ref,
                    br_ref, *rest, half, n_tiles):
    x1_ref, h2_ref, lg_ref = rest[-3:]
    i = pl.program_id(0)

    @pl.when(i < n_tiles)
    def _():
        mix = (jnp.dot(a_ref[...], wo_ref[0:half, :], preferred_element_type=_F32)
               + jnp.dot(c_ref[...], wo_ref[half:2 * half, :], preferred_element_type=_F32))
        x1 = x_ref[...] + gate_ref[0] * mix
        x1_ref[...] = x1
        h2 = _rms(x1) * g_ref[...]
        h2 = h2 * (1.0 + scale_ref[0]) + shift_ref[0]
        hi = h2.astype(_BF16)
        h2_ref[...] = hi.astype(_F32)
        lo = (h2 - hi.astype(_F32)).astype(_BF16)
        lg_ref[...] = (jnp.dot(hi, wrh_ref[...], preferred_element_type=_F32)
                       + jnp.dot(lo, wrh_ref[...], preferred_element_type=_F32)
                       + jnp.dot(hi, wrl_ref[...], preferred_element_type=_F32)) + br_ref[...]

    @pl.when(i >= n_tiles)
    def _():
        h2_ref[...] = jnp.zeros_like(h2_ref)
        lg_ref[...] = jnp.zeros_like(lg_ref)


def _outproj(attn, conv, x2d, gate, shift, scale, g_ffn, w_o_bf, w_router_pad, b_router_pad, *, tm,
             tiles_per_batch, per_row, rows_total, row_off, into=None):
    w_router_hi = w_router_pad.astype(_BF16)
    w_router_lo = (w_router_pad - w_router_hi.astype(_F32)).astype(_BF16)
    r, d = x2d.shape
    half = attn.shape[1]
    mod_rows = gate.shape[1]
    blk_off = row_off // tm
    assert row_off % tm == 0
    n_tiles = r // tm
    n_fill = (rows_total - row_off - r) // tm if into is None else 0
    last = n_tiles - 1
    row_map = lambda i: (jnp.minimum(i, last), 0)
    if per_row:
        mod_map = lambda i: (jnp.minimum(i, last), 0, 0)
    else:
        mod_map = lambda i: (jnp.minimum(i, last) // tiles_per_batch, 0, 0)
    mod_spec = lambda: pl.BlockSpec((1, mod_rows, d), mod_map)
    kern = functools.partial(_outproj_kernel, half=half, n_tiles=n_tiles)
    in_specs = [pl.BlockSpec((tm, half), row_map),
                pl.BlockSpec((tm, half), row_map),
                pl.BlockSpec((tm, d), row_map),
                mod_spec(), mod_spec(), mod_spec(),
                pl.BlockSpec((1, d), lambda i: (0, 0)),
                pl.BlockSpec((2 * half, d), lambda i: (0, 0), pipeline_mode=pl.Buffered(1)),
                pl.BlockSpec((d, ROUTER_LANES), lambda i: (0, 0)),
                pl.BlockSpec((d, ROUTER_LANES), lambda i: (0, 0)),
                pl.BlockSpec((1, ROUTER_LANES), lambda i: (0, 0))]
    args = [attn, conv, x2d, gate, shift, scale, g_ffn.reshape(1, d), w_o_bf, w_router_hi, w_router_lo,
            b_router_pad]
    aliases = {}
    if into is not None:
        aliases = {len(args): 1, len(args) + 1: 2}
        in_specs += [pl.BlockSpec(memory_space=pl.ANY), pl.BlockSpec(memory_space=pl.ANY)]
        args += list(into)
    return pl.pallas_call(
        kern,
        grid=(n_tiles + n_fill,),
        in_specs=in_specs,
        out_specs=[pl.BlockSpec((tm, d), row_map),
                   pl.BlockSpec((tm, d), lambda i: (blk_off + i, 0)),
                   pl.BlockSpec((tm, ROUTER_LANES), lambda i: (blk_off + i, 0))],
        out_shape=[jax.ShapeDtypeStruct((r, d), _F32),
                   jax.ShapeDtypeStruct((rows_total, d), _F32),
                   jax.ShapeDtypeStruct((rows_total, ROUTER_LANES), _F32)],
        input_output_aliases=aliases,
        compiler_params=_params(("arbitrary",)),
        name="outproj",
    )(*args)


def _row_gather_start(src_hbm, idx_ref, n_rows, dst_of_row, sem):
    for r in range(n_rows):
        pltpu.make_async_copy(src_hbm.at[pl.ds(idx_ref[0, 0, r], 1), :], dst_of_row(r), sem).start(
            priority=r % 2)


def _dispatch_kernel(nv_ref, idx_ref, idx_next_ref, h_hbm, o_ref, buf, sem):
    i = pl.program_id(0)
    nv = nv_ref[0]
    tm = o_ref.shape[0]
    slot = i % 2

    def start(idx_blk_ref, s):
        _row_gather_start(h_hbm, idx_blk_ref, tm, lambda r: buf.at[s, pl.ds(r, 1), :], sem.at[s])

    @pl.when(i == 0)
    def _():
        start(idx_ref, 0)

    @pl.when(i + 1 < nv)
    def _():
        start(idx_next_ref, 1 - slot)

    @pl.when(i < nv)
    def _():
        pltpu.make_async_copy(h_hbm.at[pl.ds(0, tm), :], buf.at[slot], sem.at[slot]).wait()
        o_ref[...] = buf[slot].astype(_BF16)

    @pl.when(i >= nv)
    def _():
        o_ref[...] = jnp.zeros_like(o_ref)


def _dispatch(n_valid, tok_sorted, h2_all, *, tm):
    rows = tok_sorted.shape[0]
    n_blocks = rows // tm
    d = h2_all.shape[1]
    idx3 = tok_sorted.reshape(n_blocks, 1, tm)
    return pl.pallas_call(
        _dispatch_kernel,
        grid_spec=pltpu.PrefetchScalarGridSpec(
            num_scalar_prefetch=1,
            grid=(n_blocks,),
            in_specs=[pl.BlockSpec((1, 1, tm), lambda i, nv: (i, 0, 0), memory_space=pltpu.SMEM),
                      pl.BlockSpec((1, 1, tm), lambda i, nv: (jnp.minimum(i + 1, n_blocks - 1), 0, 0),
                                   memory_space=pltpu.SMEM),
                      pl.BlockSpec(memory_space=pl.ANY)],
            out_specs=pl.BlockSpec((tm, d), lambda i, nv: (i, 0)),
            scratch_shapes=[pltpu.VMEM((2, tm, d), _F32), pltpu.SemaphoreType.DMA((2,))]),
        out_shape=jax.ShapeDtypeStruct((rows, d), _BF16),
        compiler_params=_params(("arbitrary",)),
        name="dispatch",
    )(n_valid, idx3, idx3, h2_all)


def _expert_changed(be_ref, i):
    return jnp.logical_or(i == 0, be_ref[i] != be_ref[jnp.maximum(i - 1, 0)])


def _moe_up_kernel(be_ref, nv_ref, x_ref, wg_ref, wu_ref, bg_ref, bu_ref, act_ref, wg_scr, wu_scr):
    i = pl.program_id(1)
    valid = i < nv_ref[0]

    @pl.when(jnp.logical_and(valid, _expert_changed(be_ref, i)))
    def _():
        wg_scr[...] = wg_ref[0].astype(_BF16)
        wu_scr[...] = wu_ref[0].astype(_BF16)

    @pl.when(valid)
    def _():
        x = x_ref[...]
        g = jnp.dot(x, wg_scr[...], preferred_element_type=_F32) + bg_ref[0]
        u = jnp.dot(x, wu_scr[...], preferred_element_type=_F32) + bu_ref[0]
        g = jnp.minimum(g, SWIGLU_LIMIT)
        u = jnp.clip(u, -SWIGLU_LIMIT, SWIGLU_LIMIT)
        act_ref[...] = ((u + 1.0) * g * jax.nn.sigmoid(SWIGLU_ALPHA * g)).astype(_BF16)

    @pl.when(jnp.logical_not(valid))
    def _():
        act_ref[...] = jnp.zeros_like(act_ref)


def _moe_down_kernel(be_ref, nv_ref, a_ref, wd_ref, bd_ref, y_ref, wd_scr):
    i = pl.program_id(1)
    valid = i < nv_ref[0]

    @pl.when(jnp.logical_and(valid, _expert_changed(be_ref, i)))
    def _():
        wd_scr[...] = wd_ref[0].astype(_BF16)

    @pl.when(valid)
    def _():
        y_ref[...] = jnp.dot(a_ref[...], wd_scr[...], preferred_element_type=_F32) + bd_ref[0]

    @pl.when(jnp.logical_not(valid))
    def _():
        y_ref[...] = jnp.zeros_like(y_ref)


def _moe(block_e, n_valid, xs, w_gu, b_gu, w_down, b_down, *, tm, tn):
    rows, d = xs.shape
    n_exp, _, two_ff = w_gu.shape
    d_ff = two_ff // 2
    n_blocks = rows // tm
    n_up = d_ff // tn
    b_gu3 = b_gu.reshape(n_exp, 1, two_ff)
    act = pl.pallas_call(
        _moe_up_kernel,
        grid_spec=pltpu.PrefetchScalarGridSpec(
            num_scalar_prefetch=2,
            grid=(n_up, n_blocks),
            in_specs=[pl.BlockSpec((tm, d), lambda n, i, be, nv: (i, 0)),
                      pl.BlockSpec((1, d, tn), lambda n, i, be, nv: (be[i], 0, n)),
                      pl.BlockSpec((1, d, tn), lambda n, i, be, nv: (be[i], 0, n_up + n)),
                      pl.BlockSpec((1, 1, tn), lambda n, i, be, nv: (be[i], 0, n)),
                      pl.BlockSpec((1, 1, tn), lambda n, i, be, nv: (be[i], 0, n_up + n))],
            out_specs=pl.BlockSpec((tm, tn), lambda n, i, be, nv: (i, n)),
            scratch_shapes=[pltpu.VMEM((d, tn), _BF16), pltpu.VMEM((d, tn), _BF16)]),
        out_shape=jax.ShapeDtypeStruct((rows, d_ff), _BF16),
        compiler_params=_params(("arbitrary", "arbitrary")),
        name="moe_up",
    )(block_e, n_valid, xs, w_gu, w_gu, b_gu3, b_gu3)
    n_down = d // tn
    return pl.pallas_call(
        _moe_down_kernel,
        grid_spec=pltpu.PrefetchScalarGridSpec(
            num_scalar_prefetch=2,
            grid=(n_down, n_blocks),
            in_specs=[pl.BlockSpec((tm, d_ff), lambda n, i, be, nv: (i, 0)),
                      pl.BlockSpec((1, d_ff, tn), lambda n, i, be, nv: (be[i], 0, n)),
                      pl.BlockSpec((1, 1, tn), lambda n, i, be, nv: (be[i], 0, n))],
            out_specs=pl.BlockSpec((tm, tn), lambda n, i, be, nv: (i, n)),
            scratch_shapes=[pltpu.VMEM((d_ff, tn), _BF16)]),
        out_shape=jax.ShapeDtypeStruct((rows, d), _F32),
        compiler_params=_params(("arbitrary", "arbitrary")),
        name="moe_down",
    )(block_e, n_valid, act, w_down, b_down.reshape(n_exp, 1, d))


def _combine_kernel(idx_ref, idx_next_ref, x1_ref, gk_ref, gate_ref, g_ref, ys_hbm, y_ref, buf, sem):
    i = pl.program_id(0)
    n = pl.num_programs(0)
    tm, d = x1_ref.shape
    slot = i % 2

    def start(idx_blk_ref, s):
        _row_gather_start(ys_hbm, idx_blk_ref, tm * TOP_K,
                          lambda r: buf.at[s, r % TOP_K, pl.ds(r // TOP_K, 1), :], sem.at[s])

    @pl.when(i == 0)
    def _():
        start(idx_ref, 0)

    @pl.when(i + 1 < n)
    def _():
        start(idx_next_ref, 1 - slot)

    for k in range(TOP_K):
        pltpu.make_async_copy(ys_hbm.at[pl.ds(0, tm), :], buf.at[slot, k], sem.at[slot]).wait()
    gk = gk_ref[...]
    ff = gk[:, 0:1] * buf[slot, 0]
    for k in range(1, TOP_K):
        ff = ff + gk[:, k:k + 1] * buf[slot, k]
    x2 = x1_ref[...] + gate_ref[0] * ff
    y_ref[...] = _rms(x2) * g_ref[...]


def _combine(x1, ys, dest, gk, gate, g_final, *, tm, tiles_per_batch, per_row):
    r, d = x1.shape
    n_tiles = r // tm
    mod_rows = gate.shape[1]
    mod_map = (lambda i: (i, 0, 0)) if per_row else (lambda i: (i // tiles_per_batch, 0, 0))
    idx3 = dest.reshape(n_tiles, 1, tm * TOP_K)
    return pl.pallas_call(
        _combine_kernel,
        grid=(n_tiles,),
        in_specs=[pl.BlockSpec((1, 1, tm * TOP_K), lambda i: (i, 0, 0), memory_space=pltpu.SMEM),
                  pl.BlockSpec((1, 1, tm * TOP_K), lambda i: (jnp.minimum(i + 1, n_tiles - 1), 0, 0),
                               memory_space=pltpu.SMEM),
                  pl.BlockSpec((tm, d), lambda i: (i, 0)),
                  pl.BlockSpec((tm, TOP_K), lambda i: (i, 0)),
                  pl.BlockSpec((1, mod_rows, d), mod_map),
                  pl.BlockSpec((1, d), lambda i: (0, 0)),
                  pl.BlockSpec(memory_space=pl.ANY)],
        out_specs=pl.BlockSpec((tm, d), lambda i: (i, 0)),
        out_shape=jax.ShapeDtypeStruct((r, d), _F32),
        scratch_shapes=[pltpu.VMEM((2, TOP_K, tm, d), _F32), pltpu.SemaphoreType.DMA((2,))],
        compiler_params=_params(("arbitrary",)),
        name="combine",
    )(idx3, idx3, x1, gk, gate, g_final.reshape(1, d), ys)


def _route(logits, n_exp, tm):
    t = logits.shape[0]
    top_v, top_e = lax.top_k(logits, TOP_K)
    gate = jax.nn.softmax(top_v, axis=-1)
    m = t * TOP_K
    n_blocks = -(-m // tm) + n_exp
    rows = n_blocks * tm
    i32 = jnp.int32
    flat_e = top_e.reshape(-1).astype(i32)
    sorted_e, order = lax.sort((flat_e, jnp.arange(m, dtype=i32)), num_keys=1)
    counts = jnp.sum((flat_e[:, None] == jnp.arange(n_exp, dtype=i32)[None, :]).astype(i32), axis=0)
    padded = ((counts + tm - 1) // tm) * tm
    cum_end = jnp.cumsum(padded)
    pstart = cum_end - padded
    cstart = jnp.cumsum(counts) - counts
    n_valid = (cum_end[-1] // tm).astype(i32).reshape(1)
    blk_start = jnp.arange(n_blocks, dtype=i32) * tm
    block_e = jnp.minimum(jnp.searchsorted(cum_end, blk_start, side='right'), n_exp - 1).astype(i32)
    last_e = block_e[jnp.maximum(n_valid[0] - 1, 0)]
    block_e = jnp.where(jnp.arange(n_blocks) < n_valid[0], block_e, last_e)
    row = jnp.arange(rows, dtype=i32)
    e_row = jnp.repeat(block_e, tm)
    off = row - pstart[e_row]
    is_slot = (off < counts[e_row]) & (row < cum_end[-1])
    src = jnp.clip(cstart[e_row] + off, 0, m - 1)
    tok_sorted = jnp.where(is_slot, order[src] // TOP_K, 0).astype(i32)
    dest_sorted = pstart[sorted_e] + jnp.arange(m, dtype=i32) - cstart[sorted_e]
    _, dest = lax.sort((order, dest_sorted.astype(i32)), num_keys=1)
    return gate, dest.reshape(t, TOP_K), tok_sorted, block_e, n_valid


def _pick_tile(n, pref):
    t = min(n, pref)
    while n % t:
        t //= 2
    return t


def kernel(x_prompt, x_sample, cache_k, cache_v, state_conv, c_prompt, c_sample, g_mix, g_ffn, w_ada, b_ada, w_in, lambda_q1, lambda_k1, lambda_q2, lambda_k2, subln_g, conv_w, w_o, w_router, b_router, w_gu, b_gu, w_down, b_down, g_final):
    depth = g_mix.shape[0]
    assert depth == 1
    bp, sp, d = x_prompt.shape
    bs, ts, _ = x_sample.shape
    past = cache_k.shape[2]
    aw = d // 2
    hw = aw // N_HEADS
    hd = hw // 2
    n_exp = w_router.shape[-1]
    layer = 0

    n_c = bp + bs
    c_rows = -(-n_c // 8) * 8
    c_all = jnp.concatenate([c_prompt, c_sample, jnp.zeros((c_rows - n_c, d), _F32)], axis=0)
    ada = _ada(c_all, w_ada[layer], b_ada[layer])
    shift1, scale1, gate1, shift2, scale2, gate2 = [ada[:, i * d:(i + 1) * d] for i in range(6)]

    lam_init = 0.8 - 0.6 * math.exp(-0.3 * layer)
    lam = (jnp.exp(jnp.sum(lambda_q1[layer] * lambda_k1[layer]))
           - jnp.exp(jnp.sum(lambda_q2[layer] * lambda_k2[layer])) + lam_init)
    slopes = 2.0 ** (-8.0 * np.arange(1, N_HEADS + 1) / N_HEADS)
    lam1 = lam.reshape(1).astype(_F32)
    scal_s = jnp.concatenate([lam1, jnp.asarray(slopes, _F32)])
    scal_p = jnp.concatenate([lam1, jnp.asarray(slopes * LOG2E, _F32), jnp.asarray(1.0 / (slopes * LOG2E), _F32)])
    out_scale = 1.0 - lam_init
    q_scale = hd ** -0.5

    w_in_bf = w_in[layer].astype(_BF16)
    w_o_bf = w_o[layer].astype(_BF16)
    w_router_pad = jnp.zeros((d, ROUTER_LANES), _F32).at[:, :n_exp].set(w_router[layer])
    b_router_pad = jnp.zeros((1, ROUTER_LANES), _F32).at[0, :n_exp].set(b_router[layer])

    rp = bp * sp
    tm_p = _pick_tile(sp, 512)
    tpb = sp // tm_p
    xp2 = x_prompt.reshape(rp, d)
    mod_p = lambda a: a[:bp].reshape(bp, 1, d)
    qp, kp, vp, kbp, vtp, cop, csp = _inproj(
        xp2, mod_p(shift1), mod_p(scale1), g_mix[layer], w_in_bf, conv_w[layer],
        jnp.zeros((1, 2, aw), _F32), tm=tm_p, seg=tm_p, tiles_per_batch=tpb, use_state=False,
        q_scale=q_scale * LOG2E)
    tq = _pick_tile(sp, 512)
    ap = _attn_prompt(scal_p, qp, kbp, vtp, subln_g[layer].reshape(hw, 1), batch=bp, seq=sp, tq=tq,
                      out_scale=out_scale)
    rs = bs * ts
    x1p, h2_all, lg_all = _outproj(ap, cop, xp2, mod_p(gate1), mod_p(shift2), mod_p(scale2), g_ffn[layer],
                                   w_o_bf, w_router_pad, b_router_pad, tm=tm_p, tiles_per_batch=tpb,
                                   per_row=False, rows_total=rp + rs, row_off=0)

    mod_s = lambda a: jnp.repeat(a[bp:bp + bs], ts, axis=0).reshape(1, rs, d)
    xs2 = x_sample.reshape(rs, d)
    qs, ks, vs, _, _, cos, css = _inproj(
        xs2, mod_s(shift1), mod_s(scale1), g_mix[layer], w_in_bf, conv_w[layer],
        state_conv[layer], tm=rs, seg=ts, tiles_per_batch=1, use_state=True, q_scale=q_scale)
    a_s = _attn_sample(scal_s, qs, ks, vs, cache_k[layer].reshape(bs, past, aw),
                       cache_v[layer].reshape(bs, past, aw), subln_g[layer].reshape(1, hw),
                       batch=bs, t=ts, out_scale=out_scale)
    x1s, h2_all, lg_all = _outproj(a_s, cos, xs2, mod_s(gate1), mod_s(shift2), mod_s(scale2), g_ffn[layer],
                                   w_o_bf, w_router_pad, b_router_pad, tm=rs, tiles_per_batch=1,
                                   per_row=True, rows_total=rp + rs, row_off=rp, into=(h2_all, lg_all))

    tm_e = 512
    gate, dest, tok_sorted, block_e, n_valid = _route(lg_all[:, :n_exp], n_exp, tm_e)
    xs_sorted = _dispatch(n_valid, tok_sorted, h2_all, tm=tm_e)
    d_ff = w_down.shape[2]
    ys = _moe(block_e, n_valid, xs_sorted, w_gu[layer], b_gu[layer], w_down[layer], b_down[layer],
              tm=tm_e, tn=_pick_tile(d_ff, 1024))

    tm_c = _pick_tile(sp, 128)
    y_p = _combine(x1p, ys, dest[:rp], gate[:rp], mod_p(gate2), g_final, tm=tm_c,
                   tiles_per_batch=sp // tm_c, per_row=False)
    tm_cs = _pick_tile(rs, 128)
    gate2_s = mod_s(gate2).reshape(rs // tm_cs, tm_cs, d)
    y_s = _combine(x1s, ys, dest[rp:], gate[rp:], gate2_s, g_final, tm=tm_cs, tiles_per_batch=1,
                   per_row=True)

    return (y_p.reshape(bp, sp, d), y_s.reshape(bs, ts, d),
            kp.reshape(1, bp, sp, N_HEADS, 2, hd), vp.reshape(1, bp, sp, N_HEADS, hw),
            csp.reshape(1, bp, 2, aw),
            ks.reshape(1, bs, ts, N_HEADS, 2, hd), vs.reshape(1, bs, ts, N_HEADS, hw),
            css.reshape(1, bs, 2, aw))
```

```python
import functools
import math

import jax
import jax.numpy as jnp
import numpy as np
from jax import lax
from jax.experimental import pallas as pl
from jax.experimental.pallas import tpu as pltpu

_F32 = jnp.float32
_BF16 = jnp.bfloat16

CHUNK = 64
N_HEADS = 8
TOP_K = 4
NORM_EPS = 1e-6
SWIGLU_ALPHA = 1.702
SWIGLU_LIMIT = 7.0
MASKED_SCORE = -1e30
ROUTER_LANES = 128

V7X_VMEM_LIMIT_BYTES = 56 * 1024 * 1024


def _params(semantics, vmem_bytes=V7X_VMEM_LIMIT_BYTES):
    return pltpu.CompilerParams(dimension_semantics=semantics, vmem_limit_bytes=vmem_bytes)


def _rms(x):
    return x * lax.rsqrt(jnp.mean(x * x, axis=-1, keepdims=True) + NORM_EPS)


def _ada_kernel(c_ref, w_ref, b_ref, o_ref):
    c = c_ref[...]
    s = (c * jax.nn.sigmoid(c)).astype(_BF16)
    o_ref[...] = jnp.dot(s, w_ref[...].astype(_BF16), preferred_element_type=_F32) + b_ref[...]


def _ada(c_all, w_ada, b_ada, tn=1024):
    rows, d = c_all.shape
    n = w_ada.shape[1]
    return pl.pallas_call(
        _ada_kernel,
        grid=(n // tn,),
        in_specs=[pl.BlockSpec((rows, d), lambda j: (0, 0)),
                  pl.BlockSpec((d, tn), lambda j: (0, j)),
                  pl.BlockSpec((1, tn), lambda j: (0, j))],
        out_specs=pl.BlockSpec((rows, tn), lambda j: (0, j)),
        out_shape=jax.ShapeDtypeStruct((rows, n), _F32),
        compiler_params=_params(("arbitrary",)),
        name="ada",
    )(c_all, w_ada, b_ada.reshape(1, n))


def _inproj_kernel(x_ref, shift_ref, scale_ref, g_ref, w_ref, cw_ref, st_ref,
                   q_ref, k_ref, v_ref, kb_ref, vt_ref, co_ref, cs_ref,
                   ug_scr, carry_scr, *, width, seg, tiles_per_batch, use_state, q_scale):
    i = pl.program_id(0)
    tm = x_ref.shape[0]
    h = _rms(x_ref[...]) * g_ref[...]
    h = (h * (1.0 + scale_ref[0]) + shift_ref[0]).astype(_BF16)

    def proj(j):
        return jnp.dot(h, w_ref[:, j * width:(j + 1) * width], preferred_element_type=_F32)

    q_ref[...] = (proj(0) * q_scale).astype(_BF16)
    zk = proj(1)
    k_ref[...] = zk
    kb_ref[...] = zk.astype(_BF16)
    zv = proj(2)
    v_ref[...] = zv
    vt_ref[...] = zv.T.astype(_BF16)

    ug = proj(3) * proj(5)
    gb = proj(4)
    w0 = cw_ref[0:1, :]
    w1 = cw_ref[1:2, :]
    w2 = cw_ref[2:3, :]
    if not use_state:
        @pl.when((i % tiles_per_batch) == 0)
        def _():
            carry_scr[...] = jnp.zeros_like(carry_scr)

    for s in range(tm // seg):
        lo = s * seg
        bnd = st_ref[s] if use_state else carry_scr[...]
        ug_s = ug[lo:lo + seg]
        ug_scr[6:8, :] = bnd
        ug_scr[8:8 + seg, :] = ug_s
        y = w2 * ug_s + w1 * ug_scr[7:7 + seg, :] + w0 * ug_scr[6:6 + seg, :]
        co_ref[lo:lo + seg, :] = (gb[lo:lo + seg] * y).astype(_BF16)
        last2 = ug_scr[6 + seg:8 + seg, :]
        cs_ref[s] = last2
        carry_scr[...] = last2


def _inproj(x2d, shift, scale, g_mix, w_in_bf, conv_w, state, *, tm, seg, tiles_per_batch,
            use_state, q_scale):
    r, d = x2d.shape
    width = w_in_bf.shape[1] // 6
    n_tiles = r // tm
    n_seg = tm // seg
    mod_rows = shift.shape[1]
    if use_state:
        mod_map = lambda i: (i, 0, 0)
        st_spec = pl.BlockSpec((n_seg, 2, width), lambda i: (i, 0, 0))
        cs_spec = pl.BlockSpec((n_seg, 2, width), lambda i: (i, 0, 0))
        n_state = n_tiles * n_seg
    else:
        mod_map = lambda i: (i // tiles_per_batch, 0, 0)
        st_spec = pl.BlockSpec((1, 2, width), lambda i: (0, 0, 0))
        cs_spec = pl.BlockSpec((1, 2, width), lambda i: (i // tiles_per_batch, 0, 0))
        n_state = n_tiles // tiles_per_batch
    kern = functools.partial(_inproj_kernel, width=width, seg=seg, tiles_per_batch=tiles_per_batch,
                             use_state=use_state, q_scale=q_scale)
    row_spec = lambda: pl.BlockSpec((tm, width), lambda i: (i, 0))
    return pl.pallas_call(
        kern,
        grid=(n_tiles,),
        in_specs=[pl.BlockSpec((tm, d), lambda i: (i, 0)),
                  pl.BlockSpec((1, mod_rows, d), mod_map),
                  pl.BlockSpec((1, mod_rows, d), mod_map),
                  pl.BlockSpec((1, d), lambda i: (0, 0)),
                  pl.BlockSpec((d, 6 * width), lambda i: (0, 0), pipeline_mode=pl.Buffered(1)),
                  pl.BlockSpec((3, width), lambda i: (0, 0)),
                  st_spec],
        out_specs=[row_spec(), row_spec(), row_spec(), row_spec(),
                   pl.BlockSpec((width, tm), lambda i: (0, i)),
                   row_spec(), cs_spec],
        out_shape=[jax.ShapeDtypeStruct((r, width), _BF16),
                   jax.ShapeDtypeStruct((r, width), _F32),
                   jax.ShapeDtypeStruct((r, width), _F32),
                   jax.ShapeDtypeStruct((r, width), _BF16),
                   jax.ShapeDtypeStruct((width, r), _BF16),
                   jax.ShapeDtypeStruct((r, width), _BF16),
                   jax.ShapeDtypeStruct((n_state, 2, width), _F32)],
        scratch_shapes=[pltpu.VMEM((seg + 8, width), _F32), pltpu.VMEM((2, width), _F32)],
        compiler_params=_params(("arbitrary",)),
        name="inproj",
    )(x2d, shift, scale, g_mix.reshape(1, d), w_in_bf, conv_w, state)


LOG2E = math.log2(math.e)
SKIP_MARGIN = 160.0
FAST_MARGIN = 60.0
N_AUG = 12


def _split3(v):
    t1 = v.astype(_BF16).astype(_F32)
    r = v - t1
    t2 = r.astype(_BF16).astype(_F32)
    t3 = (r - t2).astype(_BF16).astype(_F32)
    return t1, t2, t3


def _aug_group_term(idx):
    group = (idx >= 3).astype(jnp.int32) + (idx >= 6).astype(jnp.int32) + (idx >= 9).astype(jnp.int32)
    return group, idx - 3 * group


def _attn_prompt_kernel(scal_ref, q_ref, k_ref, vt_ref, g_ref, o_ref,
                        acc_scr, ml_scr, kn_scr, dbias_scr, obias_scr, aug_scr, *, tq, hd, out_scale):
    hh = pl.program_id(1)
    qi = pl.program_id(2)
    lam = scal_ref[0]
    slope2 = scal_ref[1 + hh]
    inv_slope2 = scal_ref[1 + N_HEADS + hh]
    tk = tq
    hw = 2 * hd
    n_kv_total = k_ref.shape[0] // tk

    @pl.when(qi == 0)
    def _():
        lane = lax.broadcasted_iota(jnp.int32, (tk, hw), 1)

        def chunk(c, mx):
            kk = k_ref[pl.ds(pl.multiple_of(c * tk, tk), tk), :].astype(_F32)
            sq = kk * kk
            n0 = jnp.sum(jnp.where(lane < hd, sq, 0.0), axis=1, keepdims=True)
            n1 = jnp.sum(jnp.where(lane >= hd, sq, 0.0), axis=1, keepdims=True)
            return jnp.maximum(mx[0], n0), jnp.maximum(mx[1], n1)

        zero = jnp.zeros((tk, 1), _F32)
        n0, n1 = lax.fori_loop(0, n_kv_total, chunk, (zero, zero))
        kn0 = jnp.sqrt(jnp.max(n0, axis=0, keepdims=True))
        kn1 = jnp.sqrt(jnp.max(n1, axis=0, keepdims=True))
        col = lax.broadcasted_iota(jnp.int32, (1, 2 * tq), 1)
        kn_scr[...] = jnp.where(col < tq, kn0, kn1) * 1.001

        jj = lax.broadcasted_iota(jnp.int32, (tk, tq), 0)
        ii = lax.broadcasted_iota(jnp.int32, (tk, tq), 1)
        jf = jj.astype(_F32)
        iif = ii.astype(_F32)
        visible = (jj // CHUNK) <= (ii // CHUNK)
        dbias = jnp.where(visible, slope2 * iif - slope2 * jnp.abs(iif - jf), MASKED_SCORE)
        dbias_scr[...] = jnp.concatenate([dbias, dbias], axis=1)
        obias = slope2 * jf
        obias_scr[...] = jnp.concatenate([obias, obias], axis=1)

        lc = lane & (hd - 1)
        group, _ = _aug_group_term(lc)
        jrow = lax.broadcasted_iota(jnp.int32, (tk, hw), 0)
        val = jnp.where(group == 0, jrow >> 4, jnp.where(group == 1, jrow & 15, jnp.where(group == 2, 0, 1)))
        aug_scr[...] = jnp.where(lc < N_AUG, val, 0).astype(_F32).astype(_BF16)

    qt = q_ref[...].astype(_F32).T
    qtb = qt.astype(_BF16)
    row2 = lax.broadcasted_iota(jnp.int32, (hw, 2 * tq), 0)
    col2 = lax.broadcasted_iota(jnp.int32, (hw, 2 * tq), 1)
    rhs_bd = jnp.where((row2 < hd) == (col2 < tq), jnp.concatenate([qtb, qtb], axis=1), 0).astype(_BF16)

    def kv_tiles(kv):
        start = pl.multiple_of(kv * tk, tk)
        return k_ref[pl.ds(start, tk), :], vt_ref[:, pl.ds(start, tk)]

    k, vt = kv_tiles(qi)
    s = jnp.dot(k, rhs_bd, preferred_element_type=_F32) + dbias_scr[...]
    m0 = jnp.max(s, axis=0, keepdims=True)
    p = jnp.exp2(s - m0)
    l0 = jnp.sum(p, axis=0, keepdims=True)
    acc_scr[...] = jnp.dot(vt, p.astype(_BF16), preferred_element_type=_F32)

    qsq = qt * qt
    qn = jnp.concatenate([jnp.sqrt(jnp.sum(qsq[:hd], axis=0, keepdims=True)),
                          jnp.sqrt(jnp.sum(qsq[hd:], axis=0, keepdims=True))], axis=1)
    gap = jnp.max(qn * kn_scr[...] - m0, axis=1, keepdims=True)
    reach = ((gap + SKIP_MARGIN) * inv_slope2 - 1.0) * (1.0 / tk)
    n_off = jnp.clip(jnp.floor(reach) + 1.0, 0.0, qi.astype(_F32)).astype(jnp.int32)[0, 0]
    fast = (gap <= FAST_MARGIN).astype(jnp.int32)[0, 0] == 1
    kv_lo = qi - n_off

    @pl.when(fast)
    def _():
        lane_lo = lax.broadcasted_iota(jnp.int32, (tk, hw), 1) < hd
        rowq = lax.broadcasted_iota(jnp.int32, (hw, tq), 0)

        def make_rhs(base, m_half, q_rows):
            local = rowq - base
            group, term = _aug_group_term(local)
            val = jnp.where(group == 0, slope2 * 16.0,
                            jnp.where(group == 1, slope2, jnp.where(group == 2, -slope2 * tk, -m_half)))
            t1, t2, t3 = _split3(val)
            v = jnp.where(term == 0, t1, jnp.where(term == 1, t2, t3))
            bias_rows = jnp.where((local >= 0) & (local < N_AUG), v, 0.0)
            return jnp.where(q_rows, qt, bias_rows).astype(_BF16)

        rhs0 = make_rhs(hd, m0[:, :tq], rowq < hd)
        rhs1 = make_rhs(0, m0[:, tq:], rowq >= hd)
        lcol = lax.broadcasted_iota(jnp.int32, (1, hw), 1) & (hd - 1)
        delta_cols = ((lcol >= 6) & (lcol < 9)).astype(_F32)

        def tile(kv):
            k, vt = kv_tiles(kv)
            delta = (qi - kv).astype(_F32)
            aug = aug_scr[...] + (delta * delta_cols).astype(_BF16)
            s = jnp.concatenate(
                [jnp.dot(jnp.where(lane_lo, k, aug), rhs0, preferred_element_type=_F32),
                 jnp.dot(jnp.where(lane_lo, aug, k), rhs1, preferred_element_type=_F32)], axis=1)
            p = jnp.exp2(s)
            return (jnp.sum(p.reshape(tk // 8, 8, 2 * tq), axis=0),
                    jnp.dot(vt, p.astype(_BF16), preferred_element_type=_F32))

        def pair(j, l8):
            kv = kv_lo + 2 * j
            la, pva = tile(kv)
            lb, pvb = tile(kv + 1)
            acc_scr[...] = acc_scr[...] + (pva + pvb)
            return l8 + (la + lb)

        l8 = lax.fori_loop(0, lax.shift_right_logical(n_off, 1), pair, jnp.zeros((8, 2 * tq), _F32))
        ml_scr[...] = l0 + jnp.sum(l8, axis=0, keepdims=True)

        @pl.when((n_off & 1) == 1)
        def _():
            la, pva = tile(qi - 1)
            acc_scr[...] = acc_scr[...] + pva
            ml_scr[...] = ml_scr[...] + jnp.sum(la, axis=0, keepdims=True)

    @pl.when(jnp.logical_not(fast))
    def _():
        def body(kv, carry):
            m, l = carry
            c = -slope2 * ((qi - kv) * tk).astype(_F32)
            k, vt = kv_tiles(kv)
            s = jnp.dot(k, rhs_bd, preferred_element_type=_F32) + obias_scr[...]
            m_new = jnp.maximum(m, jnp.max(s, axis=0, keepdims=True) + c)
            p = jnp.exp2(s - (m_new - c))
            alpha = jnp.exp2(m - m_new)
            l = alpha * l + jnp.sum(p, axis=0, keepdims=True)
            acc_scr[...] = acc_scr[...] * alpha + jnp.dot(vt, p.astype(_BF16), preferred_element_type=_F32)
            return m_new, l

        _, l = lax.fori_loop(kv_lo, qi, body, (m0, l0))
        ml_scr[...] = l

    acc = acc_scr[...] / ml_scr[...]
    o = acc[:, :tq] - lam * acc[:, tq:]
    o = o * lax.rsqrt(jnp.mean(o * o, axis=0, keepdims=True) + NORM_EPS)
    o = o * g_ref[...] * out_scale
    o_ref[...] = o.T.astype(_BF16)


def _attn_prompt(scal, q, kb, vt, subln_col, *, batch, seq, tq, out_scale):
    width = q.shape[1]
    hw = width // N_HEADS
    nq = seq // tq
    kern = functools.partial(_attn_prompt_kernel, tq=tq, hd=hw // 2, out_scale=out_scale)
    return pl.pallas_call(
        kern,
        grid=(batch, N_HEADS, nq),
        in_specs=[pl.BlockSpec(memory_space=pltpu.SMEM),
                  pl.BlockSpec((tq, hw), lambda b, h, i: (b * nq + i, h)),
                  pl.BlockSpec((seq, hw), lambda b, h, i: (b, h)),
                  pl.BlockSpec((hw, seq), lambda b, h, i: (h, b)),
                  pl.BlockSpec((hw, 1), lambda b, h, i: (0, 0))],
        out_specs=pl.BlockSpec((tq, hw), lambda b, h, i: (b * nq + i, h)),
        out_shape=jax.ShapeDtypeStruct((batch * seq, width), _BF16),
        scratch_shapes=[pltpu.VMEM((hw, 2 * tq), _F32),
                        pltpu.VMEM((1, 2 * tq), _F32),
                        pltpu.VMEM((1, 2 * tq), _F32),
                        pltpu.VMEM((tq, 2 * tq), _F32),
                        pltpu.VMEM((tq, 2 * tq), _F32),
                        pltpu.VMEM((tq, hw), _BF16)],
        compiler_params=_params(("arbitrary", "arbitrary", "arbitrary")),
        name="attn_prompt",
    )(scal, q, kb, vt, subln_col)


def _attn_sample_kernel(scal_ref, q_ref, kn_ref, vn_ref, kc_ref, vc_ref, g_ref, o_ref, *, hd, out_scale):
    hh = pl.program_id(1)
    lam = scal_ref[0]
    slope = scal_ref[1 + hh]
    t = q_ref.shape[0]
    past = kc_ref.shape[1]
    q = q_ref[...]
    kc = kc_ref[0].astype(_BF16)
    kn = kn_ref[...].astype(_BF16)
    vc = vc_ref[0].astype(_BF16)
    vn = vn_ref[...].astype(_BF16)
    qpos = lax.broadcasted_iota(jnp.int32, (t, past), 0).astype(_F32) + float(past)
    kpos = lax.broadcasted_iota(jnp.int32, (t, past), 1).astype(_F32)
    bias_c = -slope * jnp.abs(qpos - kpos)
    tq_i = lax.broadcasted_iota(jnp.int32, (t, t), 0).astype(_F32)
    tk_i = lax.broadcasted_iota(jnp.int32, (t, t), 1).astype(_F32)
    bias_n = -slope * jnp.abs(tq_i - tk_i)
    nt = (((1,), (1,)), ((), ()))
    outs = []
    for j in range(2):
        sl = slice(j * hd, (j + 1) * hd)
        sc = lax.dot_general(q[:, sl], kc[:, sl], nt, preferred_element_type=_F32) + bias_c
        sn = lax.dot_general(q[:, sl], kn[:, sl], nt, preferred_element_type=_F32) + bias_n
        m = jnp.maximum(jnp.max(sc, axis=-1, keepdims=True), jnp.max(sn, axis=-1, keepdims=True))
        pc = jnp.exp(sc - m)
        pn = jnp.exp(sn - m)
        l = jnp.sum(pc, axis=-1, keepdims=True) + jnp.sum(pn, axis=-1, keepdims=True)
        o = (jnp.dot(pc.astype(_BF16), vc, preferred_element_type=_F32)
             + jnp.dot(pn.astype(_BF16), vn, preferred_element_type=_F32))
        outs.append(o / l)
    o = outs[0] - lam * outs[1]
    o = _rms(o) * g_ref[...] * out_scale
    o_ref[...] = o.astype(_BF16)


def _attn_sample(scal, q, k_new, v_new, cache_k, cache_v, subln_row, *, batch, t, out_scale):
    width = q.shape[1]
    hw = width // N_HEADS
    past = cache_k.shape[1]
    kern = functools.partial(_attn_sample_kernel, hd=hw // 2, out_scale=out_scale)
    new_spec = lambda: pl.BlockSpec((t, hw), lambda b, h: (b, h))
    cache_spec = lambda: pl.BlockSpec((1, past, hw), lambda b, h: (b, 0, h))
    return pl.pallas_call(
        kern,
        grid=(batch, N_HEADS),
        in_specs=[pl.BlockSpec(memory_space=pltpu.SMEM), new_spec(), new_spec(), new_spec(),
                  cache_spec(), cache_spec(), pl.BlockSpec((1, hw), lambda b, h: (0, 0))],
        out_specs=new_spec(),
        out_shape=jax.ShapeDtypeStruct((batch * t, width), _BF16),
        compiler_params=_params(("arbitrary", "arbitrary")),
        name="attn_sample",
    )(scal, q, k_new, v_new, cache_k, cache_v, subln_row)


def _outproj_kernel(a_ref, c_ref, x_ref, gate_ref, shift_ref, scale_ref, g_ref, wo_ref, wrh_ref, wrl_ref,
                    br_ref, *rest, half, n_tiles):
    x1_ref, h2_ref, lg_ref = rest[-3:]
    i = pl.program_id(0)

    @pl.when(i < n_tiles)
    def _():
        mix = (jnp.dot(a_ref[...], wo_ref[0:half, :], preferred_element_type=_F32)
               + jnp.dot(c_ref[...], wo_ref[half:2 * half, :], preferred_element_type=_F32))
        x1 = x_ref[...] + gate_ref[0] * mix
        x1_ref[...] = x1
        h2 = _rms(x1) * g_ref[...]
        h2 = h2 * (1.0 + scale_ref[0]) + shift_ref[0]
        hi = h2.astype(_BF16)
        h2_ref[...] = hi.astype(_F32)
        lo = (h2 - hi.astype(_F32)).astype(_BF16)
        lg_ref[...] = (jnp.dot(hi, wrh_ref[...], preferred_element_type=_F32)
                       + jnp.dot(lo, wrh_ref[...], preferred_element_type=_F32)
                       + jnp.dot(hi, wrl_ref[...], preferred_element_type=_F32)) + br_ref[...]

    @pl.when(i >= n_tiles)
    def _():
        h2_ref[...] = jnp.zeros_like(h2_ref)
        lg_ref[...] = jnp.zeros_like(lg_ref)


def _outproj(attn, conv, x2d, gate, shift, scale, g_ffn, w_o_bf, w_router_pad, b_router_pad, *, tm,
             tiles_per_batch, per_row, rows_total, row_off, into=None):
    w_router_hi = w_router_pad.astype(_BF16)
    w_router_lo = (w_router_pad - w_router_hi.astype(_F32)).astype(_BF16)
    r, d = x2d.shape
    half = attn.shape[1]
    mod_rows = gate.shape[1]
    blk_off = row_off // tm
    assert row_off % tm == 0
    n_tiles = r // tm
    n_fill = (rows_total - row_off - r) // tm if into is None else 0
    last = n_tiles - 1
    row_map = lambda i: (jnp.minimum(i, last), 0)
    if per_row:
        mod_map = lambda i: (jnp.minimum(i, last), 0, 0)
    else:
        mod_map = lambda i: (jnp.minimum(i, last) // tiles_per_batch, 0, 0)
    mod_spec = lambda: pl.BlockSpec((1, mod_rows, d), mod_map)
    kern = functools.partial(_outproj_kernel, half=half, n_tiles=n_tiles)
    in_specs = [pl.BlockSpec((tm, half), row_map),
                pl.BlockSpec((tm, half), row_map),
                pl.BlockSpec((tm, d), row_map),
                mod_spec(), mod_spec(), mod_spec(),
                pl.BlockSpec((1, d), lambda i: (0, 0)),
                pl.BlockSpec((2 * half, d), lambda i: (0, 0), pipeline_mode=pl.Buffered(1)),
                pl.BlockSpec((d, ROUTER_LANES), lambda i: (0, 0)),
                pl.BlockSpec((d, ROUTER_LANES), lambda i: (0, 0)),
                pl.BlockSpec((1, ROUTER_LANES), lambda i: (0, 0))]
    args = [attn, conv, x2d, gate, shift, scale, g_ffn.reshape(1, d), w_o_bf, w_router_hi, w_router_lo,
            b_router_pad]
    aliases = {}
    if into is not None:
        aliases = {len(args): 1, len(args) + 1: 2}
        in_specs += [pl.BlockSpec(memory_space=pl.ANY), pl.BlockSpec(memory_space=pl.ANY)]
        args += list(into)
    return pl.pallas_call(
        kern,
        grid=(n_tiles + n_fill,),
        in_specs=in_specs,
        out_specs=[pl.BlockSpec((tm, d), row_map),
                   pl.BlockSpec((tm, d), lambda i: (blk_off + i, 0)),
                   pl.BlockSpec((tm, ROUTER_LANES), lambda i: (blk_off + i, 0))],
        out_shape=[jax.ShapeDtypeStruct((r, d), _F32),
                   jax.ShapeDtypeStruct((rows_total, d), _F32),
                   jax.ShapeDtypeStruct((rows_total, ROUTER_LANES), _F32)],
        input_output_aliases=aliases,
        compiler_params=_params(("arbitrary",)),
        name="outproj",
    )(*args)


def _row_gather_start(src_hbm, idx_ref, n_rows, dst_of_row, sem):
    for r in range(n_rows):
        pltpu.make_async_copy(src_hbm.at[pl.ds(idx_ref[0, 0, r], 1), :], dst_of_row(r), sem).start(
            priority=r % 2)


def _dispatch_kernel(nv_ref, idx_ref, idx_next_ref, h_hbm, o_ref, buf, sem):
    i = pl.program_id(0)
    nv = nv_ref[0]
    tm = o_ref.shape[0]
    slot = i % 2

    def start(idx_blk_ref, s):
        _row_gather_start(h_hbm, idx_blk_ref, tm, lambda r: buf.at[s, pl.ds(r, 1), :], sem.at[s])

    @pl.when(i == 0)
    def _():
        start(idx_ref, 0)

    @pl.when(i + 1 < nv)
    def _():
        start(idx_next_ref, 1 - slot)

    @pl.when(i < nv)
    def _():
        pltpu.make_async_copy(h_hbm.at[pl.ds(0, tm), :], buf.at[slot], sem.at[slot]).wait()
        o_ref[...] = buf[slot].astype(_BF16)

    @pl.when(i >= nv)
    def _():
        o_ref[...] = jnp.zeros_like(o_ref)


def _dispatch(n_valid, tok_sorted, h2_all, *, tm):
    rows = tok_sorted.shape[0]
    n_blocks = rows // tm
    d = h2_all.shape[1]
    idx3 = tok_sorted.reshape(n_blocks, 1, tm)
    return pl.pallas_call(
        _dispatch_kernel,
        grid_spec=pltpu.PrefetchScalarGridSpec(
            num_scalar_prefetch=1,
            grid=(n_blocks,),
            in_specs=[pl.BlockSpec((1, 1, tm), lambda i, nv: (i, 0, 0), memory_space=pltpu.SMEM),
                      pl.BlockSpec((1, 1, tm), lambda i, nv: (jnp.minimum(i + 1, n_blocks - 1), 0, 0),
                                   memory_space=pltpu.SMEM),
                      pl.BlockSpec(memory_space=pl.ANY)],
            out_specs=pl.BlockSpec((tm, d), lambda i, nv: (i, 0)),
            scratch_shapes=[pltpu.VMEM((2, tm, d), _F32), pltpu.SemaphoreType.DMA((2,))]),
        out_shape=jax.ShapeDtypeStruct((rows, d), _BF16),
        compiler_params=_params(("arbitrary",)),
        name="dispatch",
    )(n_valid, idx3, idx3, h2_all)


def _expert_changed(be_ref, i):
    return jnp.logical_or(i == 0, be_ref[i] != be_ref[jnp.maximum(i - 1, 0)])


def _moe_up_kernel(be_ref, nv_ref, x_ref, wg_ref, wu_ref, bg_ref, bu_ref, act_ref, wg_scr, wu_scr):
    i = pl.program_id(1)
    valid = i < nv_ref[0]

    @pl.when(jnp.logical_and(valid, _expert_changed(be_ref, i)))
    def _():
        wg_scr[...] = wg_ref[0].astype(_BF16)
        wu_scr[...] = wu_ref[0].astype(_BF16)

    @pl.when(valid)
    def _():
        x = x_ref[...]
        g = jnp.dot(x, wg_scr[...], preferred_element_type=_F32) + bg_ref[0]
        u = jnp.dot(x, wu_scr[...], preferred_element_type=_F32) + bu_ref[0]
        g = jnp.minimum(g, SWIGLU_LIMIT)
        u = jnp.clip(u, -SWIGLU_LIMIT, SWIGLU_LIMIT)
        act_ref[...] = ((u + 1.0) * g * jax.nn.sigmoid(SWIGLU_ALPHA * g)).astype(_BF16)

    @pl.when(jnp.logical_not(valid))
    def _():
        act_ref[...] = jnp.zeros_like(act_ref)


def _moe_down_kernel(be_ref, nv_ref, a_ref, wd_ref, bd_ref, y_ref, wd_scr):
    i = pl.program_id(1)
    valid = i < nv_ref[0]

    @pl.when(jnp.logical_and(valid, _expert_changed(be_ref, i)))
    def _():
        wd_scr[...] = wd_ref[0].astype(_BF16)

    @pl.when(valid)
    def _():
        y_ref[...] = jnp.dot(a_ref[...], wd_scr[...], preferred_element_type=_F32) + bd_ref[0]

    @pl.when(jnp.logical_not(valid))
    def _():
        y_ref[...] = jnp.zeros_like(y_ref)


def _moe(block_e, n_valid, xs, w_gu, b_gu, w_down, b_down, *, tm, tn):
    rows, d = xs.shape
    n_exp, _, two_ff = w_gu.shape
    d_ff = two_ff // 2
    n_blocks = rows // tm
    n_up = d_ff // tn
    b_gu3 = b_gu.reshape(n_exp, 1, two_ff)
    act = pl.pallas_call(
        _moe_up_kernel,
        grid_spec=pltpu.PrefetchScalarGridSpec(
            num_scalar_prefetch=2,
            grid=(n_up, n_blocks),
            in_specs=[pl.BlockSpec((tm, d), lambda n, i, be, nv: (i, 0)),
                      pl.BlockSpec((1, d, tn), lambda n, i, be, nv: (be[i], 0, n)),
                      pl.BlockSpec((1, d, tn), lambda n, i, be, nv: (be[i], 0, n_up + n)),
                      pl.BlockSpec((1, 1, tn), lambda n, i, be, nv: (be[i], 0, n)),
                      pl.BlockSpec((1, 1, tn), lambda n, i, be, nv: (be[i], 0, n_up + n))],
            out_specs=pl.BlockSpec((tm, tn), lambda n, i, be, nv: (i, n)),
            scratch_shapes=[pltpu.VMEM((d, tn), _BF16), pltpu.VMEM((d, tn), _BF16)]),
        out_shape=jax.ShapeDtypeStruct((rows, d_ff), _BF16),
        compiler_params=_params(("arbitrary", "arbitrary")),
        name="moe_up",
    )(block_e, n_valid, xs, w_gu, w_gu, b_gu3, b_gu3)
    n_down = d // tn
    return pl.pallas_call(
        _moe_down_kernel,
        grid_spec=pltpu.PrefetchScalarGridSpec(
            num_scalar_prefetch=2,
            grid=(n_down, n_blocks),
            in_specs=[pl.BlockSpec((tm, d_ff), lambda n, i, be, nv: (i, 0)),
                      pl.BlockSpec((1, d_ff, tn), lambda n, i, be, nv: (be[i], 0, n)),
                      pl.BlockSpec((1, 1, tn), lambda n, i, be, nv: (be[i], 0, n))],
            out_specs=pl.BlockSpec((tm, tn), lambda n, i, be, nv: (i, n)),
            scratch_shapes=[pltpu.VMEM((d_ff, tn), _BF16)]),
        out_shape=jax.ShapeDtypeStruct((rows, d), _F32),
        compiler_params=_params(("arbitrary", "arbitrary")),
        name="moe_down",
    )(block_e, n_valid, act, w_down, b_down.reshape(n_exp, 1, d))


def _combine_kernel(idx_ref, idx_next_ref, x1_ref, gk_ref, gate_ref, g_ref, ys_hbm, y_ref, buf, sem):
    i = pl.program_id(0)
    n = pl.num_programs(0)
    tm, d = x1_ref.shape
    slot = i % 2

    def start(idx_blk_ref, s):
        _row_gather_start(ys_hbm, idx_blk_ref, tm * TOP_K,
                          lambda r: buf.at[s, r % TOP_K, pl.ds(r // TOP_K, 1), :], sem.at[s])

    @pl.when(i == 0)
    def _():
        start(idx_ref, 0)

    @pl.when(i + 1 < n)
    def _():
        start(idx_next_ref, 1 - slot)

    for k in range(TOP_K):
        pltpu.make_async_copy(ys_hbm.at[pl.ds(0, tm), :], buf.at[slot, k], sem.at[slot]).wait()
    gk = gk_ref[...]
    ff = gk[:, 0:1] * buf[slot, 0]
    for k in range(1, TOP_K):
        ff = ff + gk[:, k:k + 1] * buf[slot, k]
    x2 = x1_ref[...] + gate_ref[0] * ff
    y_ref[...] = _rms(x2) * g_ref[...]


def _combine(x1, ys, dest, gk, gate, g_final, *, tm, tiles_per_batch, per_row):
    r, d = x1.shape
    n_tiles = r // tm
    mod_rows = gate.shape[1]
    mod_map = (lambda i: (i, 0, 0)) if per_row else (lambda i: (i // tiles_per_batch, 0, 0))
    idx3 = dest.reshape(n_tiles, 1, tm * TOP_K)
    return pl.pallas_call(
        _combine_kernel,
        grid=(n_tiles,),
        in_specs=[pl.BlockSpec((1, 1, tm * TOP_K), lambda i: (i, 0, 0), memory_space=pltpu.SMEM),
                  pl.BlockSpec((1, 1, tm * TOP_K), lambda i: (jnp.minimum(i + 1, n_tiles - 1), 0, 0),
                               memory_space=pltpu.SMEM),
                  pl.BlockSpec((tm, d), lambda i: (i, 0)),
                  pl.BlockSpec((tm, TOP_K), lambda i: (i, 0)),
                  pl.BlockSpec((1, mod_rows, d), mod_map),
                  pl.BlockSpec((1, d), lambda i: (0, 0)),
                  pl.BlockSpec(memory_space=pl.ANY)],
        out_specs=pl.BlockSpec((tm, d), lambda i: (i, 0)),
        out_shape=jax.ShapeDtypeStruct((r, d), _F32),
        scratch_shapes=[pltpu.VMEM((2, TOP_K, tm, d), _F32), pltpu.SemaphoreType.DMA((2,))],
        compiler_params=_params(("arbitrary",)),
        name="combine",
    )(idx3, idx3, x1, gk, gate, g_final.reshape(1, d), ys)


def _route(logits, n_exp, tm):
    t = logits.shape[0]
    top_v, top_e = lax.top_k(logits, TOP_K)
    gate = jax.nn.softmax(top_v, axis=-1)
    m = t * TOP_K
    n_blocks = -(-m // tm) + n_exp
    rows = n_blocks * tm
    i32 = jnp.int32
    flat_e = top_e.reshape(-1).astype(i32)
    sorted_e, order = lax.sort((flat_e, jnp.arange(m, dtype=i32)), num_keys=1)
    counts = jnp.sum((flat_e[:, None] == jnp.arange(n_exp, dtype=i32)[None, :]).astype(i32), axis=0)
    padded = ((counts + tm - 1) // tm) * tm
    cum_end = jnp.cumsum(padded)
    pstart = cum_end - padded
    cstart = jnp.cumsum(counts) - counts
    n_valid = (cum_end[-1] // tm).astype(i32).reshape(1)
    blk_start = jnp.arange(n_blocks, dtype=i32) * tm
    block_e = jnp.minimum(jnp.searchsorted(cum_end, blk_start, side='right'), n_exp - 1).astype(i32)
    last_e = block_e[jnp.maximum(n_valid[0] - 1, 0)]
    block_e = jnp.where(jnp.arange(n_blocks) < n_valid[0], block_e, last_e)
    row = jnp.arange(rows, dtype=i32)
    e_row = jnp.repeat(block_e, tm)
    off = row - pstart[e_row]
    is_slot = (off < counts[e_row]) & (row < cum_end[-1])
    src = jnp.clip(cstart[e_row] + off, 0, m - 1)
    tok_sorted = jnp.where(is_slot, order[src] // TOP_K, 0).astype(i32)
    dest_sorted = pstart[sorted_e] + jnp.arange(m, dtype=i32) - cstart[sorted_e]
    _, dest = lax.sort((order, dest_sorted.astype(i32)), num_keys=1)
    return gate, dest.reshape(t, TOP_K), tok_sorted, block_e, n_valid


def _pick_tile(n, pref):
    t = min(n, pref)
    while n % t:
        t //= 2
    return t


def kernel(x_prompt, x_sample, cache_k, cache_v, state_conv, c_prompt, c_sample, g_mix, g_ffn, w_ada, b_ada, w_in, lambda_q1, lambda_k1, lambda_q2, lambda_k2, subln_g, conv_w, w_o, w_router, b_router, w_gu, b_gu, w_down, b_down, g_final):
    depth = g_mix.shape[0]
    assert depth == 1
    bp, sp, d = x_prompt.shape
    bs, ts, _ = x_sample.shape
    past = cache_k.shape[2]
    aw = d // 2
    hw = aw // N_HEADS
    hd = hw // 2
    n_exp = w_router.shape[-1]
    layer = 0

    n_c = bp + bs
    c_rows = -(-n_c // 8) * 8
    c_all = jnp.concatenate([c_prompt, c_sample, jnp.zeros((c_rows - n_c, d), _F32)], axis=0)
    ada = _ada(c_all, w_ada[layer], b_ada[layer])
    shift1, scale1, gate1, shift2, scale2, gate2 = [ada[:, i * d:(i + 1) * d] for i in range(6)]

    lam_init = 0.8 - 0.6 * math.exp(-0.3 * layer)
    lam = (jnp.exp(jnp.sum(lambda_q1[layer] * lambda_k1[layer]))
           - jnp.exp(jnp.sum(lambda_q2[layer] * lambda_k2[layer])) + lam_init)
    slopes = 2.0 ** (-8.0 * np.arange(1, N_HEADS + 1) / N_HEADS)
    lam1 = lam.reshape(1).astype(_F32)
    scal_s = jnp.concatenate([lam1, jnp.asarray(slopes, _F32)])
    scal_p = jnp.concatenate([lam1, jnp.asarray(slopes * LOG2E, _F32), jnp.asarray(1.0 / (slopes * LOG2E), _F32)])
    out_scale = 1.0 - lam_init
    q_scale = hd ** -0.5

    w_in_bf = w_in[layer].astype(_BF16)
    w_o_bf = w_o[layer].astype(_BF16)
    w_router_pad = jnp.zeros((d, ROUTER_LANES), _F32).at[:, :n_exp].set(w_router[layer])
    b_router_pad = jnp.zeros((1, ROUTER_LANES), _F32).at[0, :n_exp].set(b_router[layer])

    rp = bp * sp
    tm_p = _pick_tile(sp, 512)
    tpb = sp // tm_p
    xp2 = x_prompt.reshape(rp, d)
    mod_p = lambda a: a[:bp].reshape(bp, 1, d)
    qp, kp, vp, kbp, vtp, cop, csp = _inproj(
        xp2, mod_p(shift1), mod_p(scale1), g_mix[layer], w_in_bf, conv_w[layer],
        jnp.zeros((1, 2, aw), _F32), tm=tm_p, seg=tm_p, tiles_per_batch=tpb, use_state=False,
        q_scale=q_scale * LOG2E)
    tq = _pick_tile(sp, 512)
    ap = _attn_prompt(scal_p, qp, kbp, vtp, subln_g[layer].reshape(hw, 1), batch=bp, seq=sp, tq=tq,
                      out_scale=out_scale)
    rs = bs * ts
    x1p, h2_all, lg_all = _outproj(ap, cop, xp2, mod_p(gate1), mod_p(shift2), mod_p(scale2), g_ffn[layer],
                                   w_o_bf, w_router_pad, b_router_pad, tm=tm_p, tiles_per_batch=tpb,
                                   per_row=False, rows_total=rp + rs, row_off=0)

    mod_s = lambda a: jnp.repeat(a[bp:bp + bs], ts, axis=0).reshape(1, rs, d)
    xs2 = x_sample.reshape(rs, d)
    qs, ks, vs, _, _, cos, css = _inproj(
        xs2, mod_s(shift1), mod_s(scale1), g_mix[layer], w_in_bf, conv_w[layer],
        state_conv[layer], tm=rs, seg=ts, tiles_per_batch=1, use_state=True, q_scale=q_scale)
    a_s = _attn_sample(scal_s, qs, ks, vs, cache_k[layer].reshape(bs, past, aw),
                       cache_v[layer].reshape(bs, past, aw), subln_g[layer].reshape(1, hw),
                       batch=bs, t=ts, out_scale=out_scale)
    x1s, h2_all, lg_all = _outproj(a_s, cos, xs2, mod_s(gate1), mod_s(shift2), mod_s(scale2), g_ffn[layer],
                                   w_o_bf, w_router_pad, b_router_pad, tm=rs, tiles_per_batch=1,
                                   per_row=True, rows_total=rp + rs, row_off=rp, into=(h2_all, lg_all))

    tm_e = 512
    gate, dest, tok_sorted, block_e, n_valid = _route(lg_all[:, :n_exp], n_exp, tm_e)
    xs_sorted = _dispatch(n_valid, tok_sorted, h2_all, tm=tm_e)
    d_ff = w_down.shape[2]
    ys = _moe(block_e, n_valid, xs_sorted, w_gu[layer], b_gu[layer], w_down[layer], b_down[layer],
              tm=tm_e, tn=_pick_tile(d_ff, 1024))

    tm_c = _pick_tile(sp, 128)
    y_p = _combine(x1p, ys, dest[:rp], gate[:rp], mod_p(gate2), g_final, tm=tm_c,
                   tiles_per_batch=sp // tm_c, per_row=False)
    tm_cs = _pick_tile(rs, 128)
    gate2_s = mod_s(gate2).reshape(rs // tm_cs, tm_cs, d)
    y_s = _combine(x1s, ys, dest[rp:], gate[rp:], gate2_s, g_final, tm=tm_cs, tiles_per_batch=1,
                   per_row=True)

    return (y_p.reshape(bp, sp, d), y_s.reshape(bs, ts, d),
            kp.reshape(1, bp, sp, N_HEADS, 2, hd), vp.reshape(1, bp, sp, N_HEADS, hw),
            csp.reshape(1, bp, 2, aw),
            ks.reshape(1, bs, ts, N_HEADS, 2, hd), vs.reshape(1, bs, ts, N_HEADS, hw),
            css.reshape(1, bs, 2, aw))
```

```python
import functools
import math

import jax
import jax.numpy as jnp
import numpy as np
from jax import lax
from jax.experimental import pallas as pl
from jax.experimental.pallas import tpu as pltpu

_F32 = jnp.float32
_BF16 = jnp.bfloat16

CHUNK = 64
N_HEADS = 8
TOP_K = 4
NORM_EPS = 1e-6
SWIGLU_ALPHA = 1.702
SWIGLU_LIMIT = 7.0
MASKED_SCORE = -1e30
ROUTER_LANES = 128

V7X_VMEM_LIMIT_BYTES = 56 * 1024 * 1024


def _params(semantics, vmem_bytes=V7X_VMEM_LIMIT_BYTES):
    return pltpu.CompilerParams(dimension_semantics=semantics, vmem_limit_bytes=vmem_bytes)


def _rms(x):
    return x * lax.rsqrt(jnp.mean(x * x, axis=-1, keepdims=True) + NORM_EPS)


def _ada_kernel(c_ref, w_ref, b_ref, o_ref):
    c = c_ref[...]
    s = (c * jax.nn.sigmoid(c)).astype(_BF16)
    o_ref[...] = jnp.dot(s, w_ref[...].astype(_BF16), preferred_element_type=_F32) + b_ref[...]


def _ada(c_all, w_ada, b_ada, tn=1024):
    rows, d = c_all.shape
    n = w_ada.shape[1]
    return pl.pallas_call(
        _ada_kernel,
        grid=(n // tn,),
        in_specs=[pl.BlockSpec((rows, d), lambda j: (0, 0)),
                  pl.BlockSpec((d, tn), lambda j: (0, j)),
                  pl.BlockSpec((1, tn), lambda j: (0, j))],
        out_specs=pl.BlockSpec((rows, tn), lambda j: (0, j)),
        out_shape=jax.ShapeDtypeStruct((rows, n), _F32),
        compiler_params=_params(("arbitrary",)),
        name="ada",
    )(c_all, w_ada, b_ada.reshape(1, n))


def _inproj_kernel(x_ref, shift_ref, scale_ref, g_ref, w_ref, cw_ref, st_ref,
                   q_ref, k_ref, v_ref, kb_ref, vt_ref, co_ref, cs_ref,
                   ug_scr, carry_scr, *, width, seg, tiles_per_batch, use_state, q_scale):
    i = pl.program_id(0)
    tm = x_ref.shape[0]
    h = _rms(x_ref[...]) * g_ref[...]
    h = (h * (1.0 + scale_ref[0]) + shift_ref[0]).astype(_BF16)

    def proj(j):
        return jnp.dot(h, w_ref[:, j * width:(j + 1) * width], preferred_element_type=_F32)

    q_ref[...] = (proj(0) * q_scale).astype(_BF16)
    zk = proj(1)
    k_ref[...] = zk
    kb_ref[...] = zk.astype(_BF16)
    zv = proj(2)
    v_ref[...] = zv
    vt_ref[...] = zv.T.astype(_BF16)

    ug = proj(3) * proj(5)
    gb = proj(4)
    w0 = cw_ref[0:1, :]
    w1 = cw_ref[1:2, :]
    w2 = cw_ref[2:3, :]
    if not use_state:
        @pl.when((i % tiles_per_batch) == 0)
        def _():
            carry_scr[...] = jnp.zeros_like(carry_scr)

    for s in range(tm // seg):
        lo = s * seg
        bnd = st_ref[s] if use_state else carry_scr[...]
        ug_s = ug[lo:lo + seg]
        ug_scr[6:8, :] = bnd
        ug_scr[8:8 + seg, :] = ug_s
        y = w2 * ug_s + w1 * ug_scr[7:7 + seg, :] + w0 * ug_scr[6:6 + seg, :]
        co_ref[lo:lo + seg, :] = (gb[lo:lo + seg] * y).astype(_BF16)
        last2 = ug_scr[6 + seg:8 + seg, :]
        cs_ref[s] = last2
        carry_scr[...] = last2


def _inproj(x2d, shift, scale, g_mix, w_in_bf, conv_w, state, *, tm, seg, tiles_per_batch,
            use_state, q_scale):
    r, d = x2d.shape
    width = w_in_bf.shape[1] // 6
    n_tiles = r // tm
    n_seg = tm // seg
    mod_rows = shift.shape[1]
    if use_state:
        mod_map = lambda i: (i, 0, 0)
        st_spec = pl.BlockSpec((n_seg, 2, width), lambda i: (i, 0, 0))
        cs_spec = pl.BlockSpec((n_seg, 2, width), lambda i: (i, 0, 0))
        n_state = n_tiles * n_seg
    else:
        mod_map = lambda i: (i // tiles_per_batch, 0, 0)
        st_spec = pl.BlockSpec((1, 2, width), lambda i: (0, 0, 0))
        cs_spec = pl.BlockSpec((1, 2, width), lambda i: (i // tiles_per_batch, 0, 0))
        n_state = n_tiles // tiles_per_batch
    kern = functools.partial(_inproj_kernel, width=width, seg=seg, tiles_per_batch=tiles_per_batch,
                             use_state=use_state, q_scale=q_scale)
    row_spec = lambda: pl.BlockSpec((tm, width), lambda i: (i, 0))
    return pl.pallas_call(
        kern,
        grid=(n_tiles,),
        in_specs=[pl.BlockSpec((tm, d), lambda i: (i, 0)),
                  pl.BlockSpec((1, mod_rows, d), mod_map),
                  pl.BlockSpec((1, mod_rows, d), mod_map),
                  pl.BlockSpec((1, d), lambda i: (0, 0)),
                  pl.BlockSpec((d, 6 * width), lambda i: (0, 0), pipeline_mode=pl.Buffered(1)),
                  pl.BlockSpec((3, width), lambda i: (0, 0)),
                  st_spec],
        out_specs=[row_spec(), row_spec(), row_spec(), row_spec(),
                   pl.BlockSpec((width, tm), lambda i: (0, i)),
                   row_spec(), cs_spec],
        out_shape=[jax.ShapeDtypeStruct((r, width), _BF16),
                   jax.ShapeDtypeStruct((r, width), _F32),
                   jax.ShapeDtypeStruct((r, width), _F32),
                   jax.ShapeDtypeStruct((r, width), _BF16),
                   jax.ShapeDtypeStruct((width, r), _BF16),
                   jax.ShapeDtypeStruct((r, width), _BF16),
                   jax.ShapeDtypeStruct((n_state, 2, width), _F32)],
        scratch_shapes=[pltpu.VMEM((seg + 8, width), _F32), pltpu.VMEM((2, width), _F32)],
        compiler_params=_params(("arbitrary",)),
        name="inproj",
    )(x2d, shift, scale, g_mix.reshape(1, d), w_in_bf, conv_w, state)


LOG2E = math.log2(math.e)
SKIP_MARGIN = 160.0
FAST_MARGIN = 60.0
N_AUG = 12


def _split3(v):
    t1 = v.astype(_BF16).astype(_F32)
    r = v - t1
    t2 = r.astype(_BF16).astype(_F32)
    t3 = (r - t2).astype(_BF16).astype(_F32)
    return t1, t2, t3


def _aug_group_term(idx):
    group = (idx >= 3).astype(jnp.int32) + (idx >= 6).astype(jnp.int32) + (idx >= 9).astype(jnp.int32)
    return group, idx - 3 * group


def _attn_prompt_kernel(scal_ref, q_ref, k_ref, vt_ref, g_ref, o_ref,
                        acc_scr, ml_scr, kn_scr, dbias_scr, obias_scr, aug_scr, *, tq, hd, out_scale):
    hh = pl.program_id(1)
    qi = pl.program_id(2)
    lam = scal_ref[0]
    slope2 = scal_ref[1 + hh]
    inv_slope2 = scal_ref[1 + N_HEADS + hh]
    tk = tq
    hw = 2 * hd
    n_kv_total = k_ref.shape[0] // tk

    @pl.when(qi == 0)
    def _():
        lane = lax.broadcasted_iota(jnp.int32, (tk, hw), 1)

        def chunk(c, mx):
            kk = k_ref[pl.ds(pl.multiple_of(c * tk, tk), tk), :].astype(_F32)
            sq = kk * kk
            n0 = jnp.sum(jnp.where(lane < hd, sq, 0.0), axis=1, keepdims=True)
            n1 = jnp.sum(jnp.where(lane >= hd, sq, 0.0), axis=1, keepdims=True)
            return jnp.maximum(mx[0], n0), jnp.maximum(mx[1], n1)

        zero = jnp.zeros((tk, 1), _F32)
        n0, n1 = lax.fori_loop(0, n_kv_total, chunk, (zero, zero))
        kn0 = jnp.sqrt(jnp.max(n0, axis=0, keepdims=True))
        kn1 = jnp.sqrt(jnp.max(n1, axis=0, keepdims=True))
        col = lax.broadcasted_iota(jnp.int32, (1, 2 * tq), 1)
        kn_scr[...] = jnp.where(col < tq, kn0, kn1) * 1.001

        jj = lax.broadcasted_iota(jnp.int32, (tk, tq), 0)
        ii = lax.broadcasted_iota(jnp.int32, (tk, tq), 1)
        jf = jj.astype(_F32)
        iif = ii.astype(_F32)
        visible = (jj // CHUNK) <= (ii // CHUNK)
        dbias = jnp.where(visible, slope2 * iif - slope2 * jnp.abs(iif - jf), MASKED_SCORE)
        dbias_scr[...] = jnp.concatenate([dbias, dbias], axis=1)
        obias = slope2 * jf
        obias_scr[...] = jnp.concatenate([obias, obias], axis=1)

        lc = lane & (hd - 1)
        group, _ = _aug_group_term(lc)
        jrow = lax.broadcasted_iota(jnp.int32, (tk, hw), 0)
        val = jnp.where(group == 0, jrow >> 4, jnp.where(group == 1, jrow & 15, jnp.where(group == 2, 0, 1)))
        aug_scr[...] = jnp.where(lc < N_AUG, val, 0).astype(_F32).astype(_BF16)

    qt = q_ref[...].astype(_F32).T
    qtb = qt.astype(_BF16)
    row2 = lax.broadcasted_iota(jnp.int32, (hw, 2 * tq), 0)
    col2 = lax.broadcasted_iota(jnp.int32, (hw, 2 * tq), 1)
    rhs_bd = jnp.where((row2 < hd) == (col2 < tq), jnp.concatenate([qtb, qtb], axis=1), 0).astype(_BF16)

    def kv_tiles(kv):
        start = pl.multiple_of(kv * tk, tk)
        return k_ref[pl.ds(start, tk), :], vt_ref[:, pl.ds(start, tk)]

    k, vt = kv_tiles(qi)
    s = jnp.dot(k, rhs_bd, preferred_element_type=_F32) + dbias_scr[...]
    m0 = jnp.max(s, axis=0, keepdims=True)
    p = jnp.exp2(s - m0)
    l0 = jnp.sum(p, axis=0, keepdims=True)
    acc_scr[...] = jnp.dot(vt, p.astype(_BF16), preferred_element_type=_F32)

    qsq = qt * qt
    qn = jnp.concatenate([jnp.sqrt(jnp.sum(qsq[:hd], axis=0, keepdims=True)),
                          jnp.sqrt(jnp.sum(qsq[hd:], axis=0, keepdims=True))], axis=1)
    gap = jnp.max(qn * kn_scr[...] - m0, axis=1, keepdims=True)
    reach = ((gap + SKIP_MARGIN) * inv_slope2 - 1.0) * (1.0 / tk)
    n_off = jnp.clip(jnp.floor(reach) + 1.0, 0.0, qi.astype(_F32)).astype(jnp.int32)[0, 0]
    fast = (gap <= FAST_MARGIN).astype(jnp.int32)[0, 0] == 1
    kv_lo = qi - n_off

    @pl.when(fast)
    def _():
        lane_lo = lax.broadcasted_iota(jnp.int32, (tk, hw), 1) < hd
        rowq = lax.broadcasted_iota(jnp.int32, (hw, tq), 0)

        def make_rhs(base, m_half, q_rows):
            local = rowq - base
            group, term = _aug_group_term(local)
            val = jnp.where(group == 0, slope2 * 16.0,
                            jnp.where(group == 1, slope2, jnp.where(group == 2, -slope2 * tk, -m_half)))
            t1, t2, t3 = _split3(val)
            v = jnp.where(term == 0, t1, jnp.where(term == 1, t2, t3))
            bias_rows = jnp.where((local >= 0) & (local < N_AUG), v, 0.0)
            return jnp.where(q_rows, qt, bias_rows).astype(_BF16)

        rhs0 = make_rhs(hd, m0[:, :tq], rowq < hd)
        rhs1 = make_rhs(0, m0[:, tq:], rowq >= hd)
        lcol = lax.broadcasted_iota(jnp.int32, (1, hw), 1) & (hd - 1)
        delta_cols = ((lcol >= 6) & (lcol < 9)).astype(_F32)

        def tile(kv):
            k, vt = kv_tiles(kv)
            delta = (qi - kv).astype(_F32)
            aug = aug_scr[...] + (delta * delta_cols).astype(_BF16)
            s = jnp.concatenate(
                [jnp.dot(jnp.where(lane_lo, k, aug), rhs0, preferred_element_type=_F32),
                 jnp.dot(jnp.where(lane_lo, aug, k), rhs1, preferred_element_type=_F32)], axis=1)
            p = jnp.exp2(s)
            return (jnp.sum(p.reshape(tk // 8, 8, 2 * tq), axis=0),
                    jnp.dot(vt, p.astype(_BF16), preferred_element_type=_F32))

        def pair(j, l8):
            kv = kv_lo + 2 * j
            la, pva = tile(kv)
            lb, pvb = tile(kv + 1)
            acc_scr[...] = acc_scr[...] + (pva + pvb)
            return l8 + (la + lb)

        l8 = lax.fori_loop(0, lax.shift_right_logical(n_off, 1), pair, jnp.zeros((8, 2 * tq), _F32))
        ml_scr[...] = l0 + jnp.sum(l8, axis=0, keepdims=True)

        @pl.when((n_off & 1) == 1)
        def _():
            la, pva = tile(qi - 1)
            acc_scr[...] = acc_scr[...] + pva
            ml_scr[...] = ml_scr[...] + jnp.sum(la, axis=0, keepdims=True)

    @pl.when(jnp.logical_not(fast))
    def _():
        def body(kv, carry):
            m, l = carry
            c = -slope2 * ((qi - kv) * tk).astype(_F32)
            k, vt = kv_tiles(kv)
            s = jnp.dot(k, rhs_bd, preferred_element_type=_F32) + obias_scr[...]
            m_new = jnp.maximum(m, jnp.max(s, axis=0, keepdims=True) + c)
            p = jnp.exp2(s - (m_new - c))
            alpha = jnp.exp2(m - m_new)
            l = alpha * l + jnp.sum(p, axis=0, keepdims=True)
            acc_scr[...] = acc_scr[...] * alpha + jnp.dot(vt, p.astype(_BF16), preferred_element_type=_F32)
            return m_new, l

        _, l = lax.fori_loop(kv_lo, qi, body, (m0, l0))
        ml_scr[...] = l

    acc = acc_scr[...] / ml_scr[...]
    o = acc[:, :tq] - lam * acc[:, tq:]
    o = o * lax.rsqrt(jnp.mean(o * o, axis=0, keepdims=True) + NORM_EPS)
    o = o * g_ref[...] * out_scale
    o_ref[...] = o.T.astype(_BF16)


def _attn_prompt(scal, q, kb, vt, subln_col, *, batch, seq, tq, out_scale):
    width = q.shape[1]
    hw = width // N_HEADS
    nq = seq // tq
    kern = functools.partial(_attn_prompt_kernel, tq=tq, hd=hw // 2, out_scale=out_scale)
    return pl.pallas_call(
        kern,
        grid=(batch, N_HEADS, nq),
        in_specs=[pl.BlockSpec(memory_space=pltpu.SMEM),
                  pl.BlockSpec((tq, hw), lambda b, h, i: (b * nq + i, h)),
                  pl.BlockSpec((seq, hw), lambda b, h, i: (b, h)),
                  pl.BlockSpec((hw, seq), lambda b, h, i: (h, b)),
                  pl.BlockSpec((hw, 1), lambda b, h, i: (0, 0))],
        out_specs=pl.BlockSpec((tq, hw), lambda b, h, i: (b * nq + i, h)),
        out_shape=jax.ShapeDtypeStruct((batch * seq, width), _BF16),
        scratch_shapes=[pltpu.VMEM((hw, 2 * tq), _F32),
                        pltpu.VMEM((1, 2 * tq), _F32),
                        pltpu.VMEM((1, 2 * tq), _F32),
                        pltpu.VMEM((tq, 2 * tq), _F32),
                        pltpu.VMEM((tq, 2 * tq), _F32),
                        pltpu.VMEM((tq, hw), _BF16)],
        compiler_params=_params(("arbitrary", "arbitrary", "arbitrary")),
        name="attn_prompt",
    )(scal, q, kb, vt, subln_col)


def _attn_sample_kernel(scal_ref, q_ref, kn_ref, vn_ref, kc_ref, vc_ref, g_ref, o_ref, *, hd, out_scale):
    hh = pl.program_id(1)
    lam = scal_ref[0]
    slope = scal_ref[1 + hh]
    t = q_ref.shape[0]
    past = kc_ref.shape[1]
    q = q_ref[...]
    kc = kc_ref[0].astype(_BF16)
    kn = kn_ref[...].astype(_BF16)
    vc = vc_ref[0].astype(_BF16)
    vn = vn_ref[...].astype(_BF16)
    qpos = lax.broadcasted_iota(jnp.int32, (t, past), 0).astype(_F32) + float(past)
    kpos = lax.broadcasted_iota(jnp.int32, (t, past), 1).astype(_F32)
    bias_c = -slope * jnp.abs(qpos - kpos)
    tq_i = lax.broadcasted_iota(jnp.int32, (t, t), 0).astype(_F32)
    tk_i = lax.broadcasted_iota(jnp.int32, (t, t), 1).astype(_F32)
    bias_n = -slope * jnp.abs(tq_i - tk_i)
    nt = (((1,), (1,)), ((), ()))
    outs = []
    for j in range(2):
        sl = slice(j * hd, (j + 1) * hd)
        sc = lax.dot_general(q[:, sl], kc[:, sl], nt, preferred_element_type=_F32) + bias_c
        sn = lax.dot_general(q[:, sl], kn[:, sl], nt, preferred_element_type=_F32) + bias_n
        m = jnp.maximum(jnp.max(sc, axis=-1, keepdims=True), jnp.max(sn, axis=-1, keepdims=True))
        pc = jnp.exp(sc - m)
        pn = jnp.exp(sn - m)
        l = jnp.sum(pc, axis=-1, keepdims=True) + jnp.sum(pn, axis=-1, keepdims=True)
        o = (jnp.dot(pc.astype(_BF16), vc, preferred_element_type=_F32)
             + jnp.dot(pn.astype(_BF16), vn, preferred_element_type=_F32))
        outs.append(o / l)
    o = outs[0] - lam * outs[1]
    o = _rms(o) * g_ref[...] * out_scale
    o_ref[...] = o.astype(_BF16)


def _attn_sample(scal, q, k_new, v_new, cache_k, cache_v, subln_row, *, batch, t, out_scale):
    width = q.shape[1]
    hw = width // N_HEADS
    past = cache_k.shape[1]
    kern = functools.partial(_attn_sample_kernel, hd=hw // 2, out_scale=out_scale)
    new_spec = lambda: pl.BlockSpec((t, hw), lambda b, h: (b, h))
    cache_spec = lambda: pl.BlockSpec((1, past, hw), lambda b, h: (b, 0, h))
    return pl.pallas_call(
        kern,
        grid=(batch, N_HEADS),
        in_specs=[pl.BlockSpec(memory_space=pltpu.SMEM), new_spec(), new_spec(), new_spec(),
                  cache_spec(), cache_spec(), pl.BlockSpec((1, hw), lambda b, h: (0, 0))],
        out_specs=new_spec(),
        out_shape=jax.ShapeDtypeStruct((batch * t, width), _BF16),
        compiler_params=_params(("arbitrary", "arbitrary")),
        name="attn_sample",
    )(scal, q, k_new, v_new, cache_k, cache_v, subln_row)


def _outproj_kernel(a_ref, c_ref, x_ref, gate_ref, shift_ref, scale_ref, g_ref, wo_ref, wrh_ref, wrl_ref,
                    br_ref, *rest, half, n_tiles):
    x1_ref, h2_ref, lg_ref = rest[-3:]
    i = pl.program_id(0)

    @pl.when(i < n_tiles)
    def _():
        mix = (jnp.dot(a_ref[...], wo_ref[0:half, :], preferred_element_type=_F32)
               + jnp.dot(c_ref[...], wo_ref[half:2 * half, :], preferred_element_type=_F32))
        x1 = x_ref[...] + gate_ref[0] * mix
        x1_ref[...] = x1
        h2 = _rms(x1) * g_ref[...]
        h2 = h2 * (1.0 + scale_ref[0]) + shift_ref[0]
        hi = h2.astype(_BF16)
        dh = h2.shape[1] // 2
        bits = lax.bitcast_convert_type(hi.astype(_F32), jnp.uint32)
        h2_ref[...] = bits[:, :dh] | lax.shift_right_logical(bits[:, dh:], jnp.uint32(16))
        lo = (h2 - hi.astype(_F32)).astype(_BF16)
        lg_ref[...] = (jnp.dot(hi, wrh_ref[...], preferred_element_type=_F32)
                       + jnp.dot(lo, wrh_ref[...], preferred_element_type=_F32)
                       + jnp.dot(hi, wrl_ref[...], preferred_element_type=_F32)) + br_ref[...]

    @pl.when(i >= n_tiles)
    def _():
        h2_ref[...] = jnp.zeros_like(h2_ref)
        lg_ref[...] = jnp.zeros_like(lg_ref)


def _outproj(attn, conv, x2d, gate, shift, scale, g_ffn, w_o_bf, w_router_pad, b_router_pad, *, tm,
             tiles_per_batch, per_row, rows_total, row_off, into=None):
    w_router_hi = w_router_pad.astype(_BF16)
    w_router_lo = (w_router_pad - w_router_hi.astype(_F32)).astype(_BF16)
    r, d = x2d.shape
    half = attn.shape[1]
    mod_rows = gate.shape[1]
    blk_off = row_off // tm
    assert row_off % tm == 0
    n_tiles = r // tm
    n_fill = (rows_total - row_off - r) // tm if into is None else 0
    last = n_tiles - 1
    row_map = lambda i: (jnp.minimum(i, last), 0)
    if per_row:
        mod_map = lambda i: (jnp.minimum(i, last), 0, 0)
    else:
        mod_map = lambda i: (jnp.minimum(i, last) // tiles_per_batch, 0, 0)
    mod_spec = lambda: pl.BlockSpec((1, mod_rows, d), mod_map)
    kern = functools.partial(_outproj_kernel, half=half, n_tiles=n_tiles)
    in_specs = [pl.BlockSpec((tm, half), row_map),
                pl.BlockSpec((tm, half), row_map),
                pl.BlockSpec((tm, d), row_map),
                mod_spec(), mod_spec(), mod_spec(),
                pl.BlockSpec((1, d), lambda i: (0, 0)),
                pl.BlockSpec((2 * half, d), lambda i: (0, 0), pipeline_mode=pl.Buffered(1)),
                pl.BlockSpec((d, ROUTER_LANES), lambda i: (0, 0)),
                pl.BlockSpec((d, ROUTER_LANES), lambda i: (0, 0)),
                pl.BlockSpec((1, ROUTER_LANES), lambda i: (0, 0))]
    args = [attn, conv, x2d, gate, shift, scale, g_ffn.reshape(1, d), w_o_bf, w_router_hi, w_router_lo,
            b_router_pad]
    aliases = {}
    if into is not None:
        aliases = {len(args): 1, len(args) + 1: 2}
        in_specs += [pl.BlockSpec(memory_space=pl.ANY), pl.BlockSpec(memory_space=pl.ANY)]
        args += list(into)
    return pl.pallas_call(
        kern,
        grid=(n_tiles + n_fill,),
        in_specs=in_specs,
        out_specs=[pl.BlockSpec((tm, d), row_map),
                   pl.BlockSpec((tm, d // 2), lambda i: (blk_off + i, 0)),
                   pl.BlockSpec((tm, ROUTER_LANES), lambda i: (blk_off + i, 0))],
        out_shape=[jax.ShapeDtypeStruct((r, d), _F32),
                   jax.ShapeDtypeStruct((rows_total, d // 2), jnp.uint32),
                   jax.ShapeDtypeStruct((rows_total, ROUTER_LANES), _F32)],
        input_output_aliases=aliases,
        compiler_params=_params(("arbitrary",)),
        name="outproj",
    )(*args)


def _row_gather_start(src_hbm, idx_ref, n_rows, dst_of_row, sem):
    for r in range(n_rows):
        pltpu.make_async_copy(src_hbm.at[pl.ds(idx_ref[0, 0, r], 1), :], dst_of_row(r), sem).start(
            priority=r % 2)


def _dispatch_kernel(nv_ref, idx_ref, idx_next_ref, h_hbm, o_ref, buf, sem):
    i = pl.program_id(0)
    nv = nv_ref[0]
    tm = o_ref.shape[0]
    slot = i % 2

    def start(idx_blk_ref, s):
        _row_gather_start(h_hbm, idx_blk_ref, tm, lambda r: buf.at[s, pl.ds(r, 1), :], sem.at[s])

    @pl.when(i == 0)
    def _():
        start(idx_ref, 0)

    @pl.when(i + 1 < nv)
    def _():
        start(idx_next_ref, 1 - slot)

    @pl.when(i < nv)
    def _():
        pltpu.make_async_copy(h_hbm.at[pl.ds(0, tm), :], buf.at[slot], sem.at[slot]).wait()
        u = buf[slot]
        dh = u.shape[1]
        o_ref[:, :dh] = lax.bitcast_convert_type(u & jnp.uint32(0xFFFF0000), _F32).astype(_BF16)
        o_ref[:, dh:] = lax.bitcast_convert_type(lax.shift_left(u, jnp.uint32(16)), _F32).astype(_BF16)

    @pl.when(i >= nv)
    def _():
        o_ref[...] = jnp.zeros_like(o_ref)


def _dispatch(n_valid, tok_sorted, h2_all, *, tm):
    rows = tok_sorted.shape[0]
    n_blocks = rows // tm
    dh = h2_all.shape[1]
    d = 2 * dh
    idx3 = tok_sorted.reshape(n_blocks, 1, tm)
    return pl.pallas_call(
        _dispatch_kernel,
        grid_spec=pltpu.PrefetchScalarGridSpec(
            num_scalar_prefetch=1,
            grid=(n_blocks,),
            in_specs=[pl.BlockSpec((1, 1, tm), lambda i, nv: (i, 0, 0), memory_space=pltpu.SMEM),
                      pl.BlockSpec((1, 1, tm), lambda i, nv: (jnp.minimum(i + 1, n_blocks - 1), 0, 0),
                                   memory_space=pltpu.SMEM),
                      pl.BlockSpec(memory_space=pl.ANY)],
            out_specs=pl.BlockSpec((tm, d), lambda i, nv: (i, 0)),
            scratch_shapes=[pltpu.VMEM((2, tm, dh), jnp.uint32), pltpu.SemaphoreType.DMA((2,))]),
        out_shape=jax.ShapeDtypeStruct((rows, d), _BF16),
        compiler_params=_params(("arbitrary",)),
        name="dispatch",
    )(n_valid, idx3, idx3, h2_all)


def _expert_changed(be_ref, i):
    return jnp.logical_or(i == 0, be_ref[i] != be_ref[jnp.maximum(i - 1, 0)])


def _moe_up_kernel(be_ref, nv_ref, x_ref, wg_ref, wu_ref, bg_ref, bu_ref, act_ref, wg_scr, wu_scr):
    i = pl.program_id(1)
    valid = i < nv_ref[0]

    @pl.when(jnp.logical_and(valid, _expert_changed(be_ref, i)))
    def _():
        wg_scr[...] = wg_ref[0].astype(_BF16)
        wu_scr[...] = wu_ref[0].astype(_BF16)

    @pl.when(valid)
    def _():
        x = x_ref[...]
        g = jnp.dot(x, wg_scr[...], preferred_element_type=_F32) + bg_ref[0]
        u = jnp.dot(x, wu_scr[...], preferred_element_type=_F32) + bu_ref[0]
        g = jnp.minimum(g, SWIGLU_LIMIT)
        u = jnp.clip(u, -SWIGLU_LIMIT, SWIGLU_LIMIT)
        act_ref[...] = ((u + 1.0) * g * jax.nn.sigmoid(SWIGLU_ALPHA * g)).astype(_BF16)

    @pl.when(jnp.logical_not(valid))
    def _():
        act_ref[...] = jnp.zeros_like(act_ref)


def _moe_down_kernel(be_ref, nv_ref, a_ref, wd_ref, bd_ref, y_ref, wd_scr):
    i = pl.program_id(1)
    valid = i < nv_ref[0]

    @pl.when(jnp.logical_and(valid, _expert_changed(be_ref, i)))
    def _():
        wd_scr[...] = wd_ref[0].astype(_BF16)

    @pl.when(valid)
    def _():
        y_ref[...] = jnp.dot(a_ref[...], wd_scr[...], preferred_element_type=_F32) + bd_ref[0]

    @pl.when(jnp.logical_not(valid))
    def _():
        y_ref[...] = jnp.zeros_like(y_ref)


def _moe(block_e, n_valid, xs, w_gu, b_gu, w_down, b_down, *, tm, tn):
    rows, d = xs.shape
    n_exp, _, two_ff = w_gu.shape
    d_ff = two_ff // 2
    n_blocks = rows // tm
    n_up = d_ff // tn
    b_gu3 = b_gu.reshape(n_exp, 1, two_ff)
    act = pl.pallas_call(
        _moe_up_kernel,
        grid_spec=pltpu.PrefetchScalarGridSpec(
            num_scalar_prefetch=2,
            grid=(n_up, n_blocks),
            in_specs=[pl.BlockSpec((tm, d), lambda n, i, be, nv: (i, 0)),
                      pl.BlockSpec((1, d, tn), lambda n, i, be, nv: (be[i], 0, n)),
                      pl.BlockSpec((1, d, tn), lambda n, i, be, nv: (be[i], 0, n_up + n)),
                      pl.BlockSpec((1, 1, tn), lambda n, i, be, nv: (be[i], 0, n)),
                      pl.BlockSpec((1, 1, tn), lambda n, i, be, nv: (be[i], 0, n_up + n))],
            out_specs=pl.BlockSpec((tm, tn), lambda n, i, be, nv: (i, n)),
            scratch_shapes=[pltpu.VMEM((d, tn), _BF16), pltpu.VMEM((d, tn), _BF16)]),
        out_shape=jax.ShapeDtypeStruct((rows, d_ff), _BF16),
        compiler_params=_params(("arbitrary", "arbitrary")),
        name="moe_up",
    )(block_e, n_valid, xs, w_gu, w_gu, b_gu3, b_gu3)
    n_down = d // tn
    return pl.pallas_call(
        _moe_down_kernel,
        grid_spec=pltpu.PrefetchScalarGridSpec(
            num_scalar_prefetch=2,
            grid=(n_down, n_blocks),
            in_specs=[pl.BlockSpec((tm, d_ff), lambda n, i, be, nv: (i, 0)),
                      pl.BlockSpec((1, d_ff, tn), lambda n, i, be, nv: (be[i], 0, n)),
                      pl.BlockSpec((1, 1, tn), lambda n, i, be, nv: (be[i], 0, n))],
            out_specs=pl.BlockSpec((tm, tn), lambda n, i, be, nv: (i, n)),
            scratch_shapes=[pltpu.VMEM((d_ff, tn), _BF16)]),
        out_shape=jax.ShapeDtypeStruct((rows, d), _F32),
        compiler_params=_params(("arbitrary", "arbitrary")),
        name="moe_down",
    )(block_e, n_valid, act, w_down, b_down.reshape(n_exp, 1, d))


def _combine_kernel(idx_ref, idx_next_ref, x1_ref, gk_ref, gate_ref, g_ref, ys_hbm, y_ref, buf, sem):
    i = pl.program_id(0)
    n = pl.num_programs(0)
    tm, d = x1_ref.shape
    slot = i % 2

    def start(idx_blk_ref, s):
        _row_gather_start(ys_hbm, idx_blk_ref, tm * TOP_K,
                          lambda r: buf.at[s, r % TOP_K, pl.ds(r // TOP_K, 1), :], sem.at[s])

    @pl.when(i == 0)
    def _():
        start(idx_ref, 0)

    @pl.when(i + 1 < n)
    def _():
        start(idx_next_ref, 1 - slot)

    for k in range(TOP_K):
        pltpu.make_async_copy(ys_hbm.at[pl.ds(0, tm), :], buf.at[slot, k], sem.at[slot]).wait()
    gk = gk_ref[...]
    ff = gk[:, 0:1] * buf[slot, 0]
    for k in range(1, TOP_K):
        ff = ff + gk[:, k:k + 1] * buf[slot, k]
    x2 = x1_ref[...] + gate_ref[0] * ff
    y_ref[...] = _rms(x2) * g_ref[...]


def _combine(x1, ys, dest, gk, gate, g_final, *, tm, tiles_per_batch, per_row):
    r, d = x1.shape
    n_tiles = r // tm
    mod_rows = gate.shape[1]
    mod_map = (lambda i: (i, 0, 0)) if per_row else (lambda i: (i // tiles_per_batch, 0, 0))
    idx3 = dest.reshape(n_tiles, 1, tm * TOP_K)
    return pl.pallas_call(
        _combine_kernel,
        grid=(n_tiles,),
        in_specs=[pl.BlockSpec((1, 1, tm * TOP_K), lambda i: (i, 0, 0), memory_space=pltpu.SMEM),
                  pl.BlockSpec((1, 1, tm * TOP_K), lambda i: (jnp.minimum(i + 1, n_tiles - 1), 0, 0),
                               memory_space=pltpu.SMEM),
                  pl.BlockSpec((tm, d), lambda i: (i, 0)),
                  pl.BlockSpec((tm, TOP_K), lambda i: (i, 0)),
                  pl.BlockSpec((1, mod_rows, d), mod_map),
                  pl.BlockSpec((1, d), lambda i: (0, 0)),
                  pl.BlockSpec(memory_space=pl.ANY)],
        out_specs=pl.BlockSpec((tm, d), lambda i: (i, 0)),
        out_shape=jax.ShapeDtypeStruct((r, d), _F32),
        scratch_shapes=[pltpu.VMEM((2, TOP_K, tm, d), _F32), pltpu.SemaphoreType.DMA((2,))],
        compiler_params=_params(("arbitrary",)),
        name="combine",
    )(idx3, idx3, x1, gk, gate, g_final.reshape(1, d), ys)


def _route(logits, n_exp, tm):
    t = logits.shape[0]
    top_v, top_e = lax.top_k(logits, TOP_K)
    gate = jax.nn.softmax(top_v, axis=-1)
    m = t * TOP_K
    n_blocks = -(-m // tm) + n_exp
    rows = n_blocks * tm
    i32 = jnp.int32
    flat_e = top_e.reshape(-1).astype(i32)
    onehot = (flat_e[:, None] == jnp.arange(n_exp, dtype=i32)[None, :]).astype(i32)
    csum = jnp.cumsum(onehot, axis=0)
    rank = jnp.sum(csum * onehot, axis=1) - 1
    counts = csum[-1]
    padded = ((counts + tm - 1) // tm) * tm
    cum_end = jnp.cumsum(padded)
    pstart = cum_end - padded
    cstart = jnp.cumsum(counts) - counts
    n_valid = (cum_end[-1] // tm).astype(i32).reshape(1)
    blk_start = jnp.arange(n_blocks, dtype=i32) * tm
    n_before = jnp.sum((cum_end[None, :] <= blk_start[:, None]).astype(i32), axis=1)
    block_e = jnp.minimum(n_before, n_exp - 1)
    last_e = jnp.sum(jnp.where(jnp.arange(n_blocks) == n_valid[0] - 1, block_e, 0))
    block_e = jnp.where(jnp.arange(n_blocks) < n_valid[0], block_e, last_e).astype(i32)
    _, order = lax.sort((flat_e, jnp.arange(m, dtype=i32)), num_keys=2)
    row = jnp.arange(rows, dtype=i32)
    e_row = jnp.repeat(block_e, tm)
    off = row - pstart[e_row]
    is_slot = (off < counts[e_row]) & (row < cum_end[-1])
    src = jnp.clip(cstart[e_row] + off, 0, m - 1)
    tok_sorted = jnp.where(is_slot, order[src] // TOP_K, 0).astype(i32)
    dest = (jnp.sum(onehot * pstart[None, :], axis=1) + rank).astype(i32)
    return gate, dest.reshape(t, TOP_K), tok_sorted, block_e, n_valid


def _pick_tile(n, pref):
    t = min(n, pref)
    while n % t:
        t //= 2
    return t


def kernel(x_prompt, x_sample, cache_k, cache_v, state_conv, c_prompt, c_sample, g_mix, g_ffn, w_ada, b_ada, w_in, lambda_q1, lambda_k1, lambda_q2, lambda_k2, subln_g, conv_w, w_o, w_router, b_router, w_gu, b_gu, w_down, b_down, g_final):
    depth = g_mix.shape[0]
    assert depth == 1
    bp, sp, d = x_prompt.shape
    bs, ts, _ = x_sample.shape
    past = cache_k.shape[2]
    aw = d // 2
    hw = aw // N_HEADS
    hd = hw // 2
    n_exp = w_router.shape[-1]
    layer = 0

    n_c = bp + bs
    c_rows = -(-n_c // 8) * 8
    c_all = jnp.concatenate([c_prompt, c_sample, jnp.zeros((c_rows - n_c, d), _F32)], axis=0)
    ada = _ada(c_all, w_ada[layer], b_ada[layer])
    shift1, scale1, gate1, shift2, scale2, gate2 = [ada[:, i * d:(i + 1) * d] for i in range(6)]

    lam_init = 0.8 - 0.6 * math.exp(-0.3 * layer)
    lam = (jnp.exp(jnp.sum(lambda_q1[layer] * lambda_k1[layer]))
           - jnp.exp(jnp.sum(lambda_q2[layer] * lambda_k2[layer])) + lam_init)
    slopes = 2.0 ** (-8.0 * np.arange(1, N_HEADS + 1) / N_HEADS)
    lam1 = lam.reshape(1).astype(_F32)
    scal_s = jnp.concatenate([lam1, jnp.asarray(slopes, _F32)])
    scal_p = jnp.concatenate([lam1, jnp.asarray(slopes * LOG2E, _F32), jnp.asarray(1.0 / (slopes * LOG2E), _F32)])
    out_scale = 1.0 - lam_init
    q_scale = hd ** -0.5

    w_in_bf = w_in[layer].astype(_BF16)
    w_o_bf = w_o[layer].astype(_BF16)
    w_router_pad = jnp.zeros((d, ROUTER_LANES), _F32).at[:, :n_exp].set(w_router[layer])
    b_router_pad = jnp.zeros((1, ROUTER_LANES), _F32).at[0, :n_exp].set(b_router[layer])

    rp = bp * sp
    tm_p = _pick_tile(sp, 512)
    tpb = sp // tm_p
    xp2 = x_prompt.reshape(rp, d)
    mod_p = lambda a: a[:bp].reshape(bp, 1, d)
    qp, kp, vp, kbp, vtp, cop, csp = _inproj(
        xp2, mod_p(shift1), mod_p(scale1), g_mix[layer], w_in_bf, conv_w[layer],
        jnp.zeros((1, 2, aw), _F32), tm=tm_p, seg=tm_p, tiles_per_batch=tpb, use_state=False,
        q_scale=q_scale * LOG2E)
    tq = _pick_tile(sp, 512)
    ap = _attn_prompt(scal_p, qp, kbp, vtp, subln_g[layer].reshape(hw, 1), batch=bp, seq=sp, tq=tq,
                      out_scale=out_scale)
    rs = bs * ts
    x1p, h2_all, lg_all = _outproj(ap, cop, xp2, mod_p(gate1), mod_p(shift2), mod_p(scale2), g_ffn[layer],
                                   w_o_bf, w_router_pad, b_router_pad, tm=tm_p, tiles_per_batch=tpb,
                                   per_row=False, rows_total=rp + rs, row_off=0)

    mod_s = lambda a: jnp.repeat(a[bp:bp + bs], ts, axis=0).reshape(1, rs, d)
    xs2 = x_sample.reshape(rs, d)
    qs, ks, vs, _, _, cos, css = _inproj(
        xs2, mod_s(shift1), mod_s(scale1), g_mix[layer], w_in_bf, conv_w[layer],
        state_conv[layer], tm=rs, seg=ts, tiles_per_batch=1, use_state=True, q_scale=q_scale)
    a_s = _attn_sample(scal_s, qs, ks, vs, cache_k[layer].reshape(bs, past, aw),
                       cache_v[layer].reshape(bs, past, aw), subln_g[layer].reshape(1, hw),
                       batch=bs, t=ts, out_scale=out_scale)
    x1s, h2_all, lg_all = _outproj(a_s, cos, xs2, mod_s(gate1), mod_s(shift2), mod_s(scale2), g_ffn[layer],
                                   w_o_bf, w_router_pad, b_router_pad, tm=rs, tiles_per_batch=1,
                                   per_row=True, rows_total=rp + rs, row_off=rp, into=(h2_all, lg_all))

    tm_e = 512
    gate, dest, tok_sorted, block_e, n_valid = _route(lg_all[:, :n_exp], n_exp, tm_e)
    xs_sorted = _dispatch(n_valid, tok_sorted, h2_all, tm=tm_e)
    d_ff = w_down.shape[2]
    ys = _moe(block_e, n_valid, xs_sorted, w_gu[layer], b_gu[layer], w_down[layer], b_down[layer],
              tm=tm_e, tn=_pick_tile(d_ff, 1024))

    tm_c = _pick_tile(sp, 128)
    y_p = _combine(x1p, ys, dest[:rp], gate[:rp], mod_p(gate2), g_final, tm=tm_c,
                   tiles_per_batch=sp // tm_c, per_row=False)
    tm_cs = _pick_tile(rs, 128)
    gate2_s = mod_s(gate2).reshape(rs // tm_cs, tm_cs, d)
    y_s = _combine(x1s, ys, dest[rp:], gate[rp:], gate2_s, g_final, tm=tm_cs, tiles_per_batch=1,
                   per_row=True)

    return (y_p.reshape(bp, sp, d), y_s.reshape(bs, ts, d),
            kp.reshape(1, bp, sp, N_HEADS, 2, hd), vp.reshape(1, bp, sp, N_HEADS, hw),
            csp.reshape(1, bp, 2, aw),
            ks.reshape(1, bs, ts, N_HEADS, 2, hd), vs.reshape(1, bs, ts, N_HEADS, hw),
            css.reshape(1, bs, 2, aw))
```

```python
import functools
import math

import jax
import jax.numpy as jnp
import numpy as np
from jax import lax
from jax.experimental import pallas as pl
from jax.experimental.pallas import tpu as pltpu

_F32 = jnp.float32
_BF16 = jnp.bfloat16

CHUNK = 64
N_HEADS = 8
TOP_K = 4
NORM_EPS = 1e-6
SWIGLU_ALPHA = 1.702
SWIGLU_LIMIT = 7.0
MASKED_SCORE = -1e30
ROUTER_LANES = 128

V7X_VMEM_LIMIT_BYTES = 56 * 1024 * 1024


def _params(semantics, vmem_bytes=V7X_VMEM_LIMIT_BYTES):
    return pltpu.CompilerParams(dimension_semantics=semantics, vmem_limit_bytes=vmem_bytes)


def _rms(x):
    return x * lax.rsqrt(jnp.mean(x * x, axis=-1, keepdims=True) + NORM_EPS)


def _ada_kernel(c_ref, w_ref, b_ref, o_ref):
    c = c_ref[...]
    s = (c * jax.nn.sigmoid(c)).astype(_BF16)
    o_ref[...] = jnp.dot(s, w_ref[...].astype(_BF16), preferred_element_type=_F32) + b_ref[...]


def _ada(c_all, w_ada, b_ada, tn=1024):
    rows, d = c_all.shape
    n = w_ada.shape[1]
    return pl.pallas_call(
        _ada_kernel,
        grid=(n // tn,),
        in_specs=[pl.BlockSpec((rows, d), lambda j: (0, 0)),
                  pl.BlockSpec((d, tn), lambda j: (0, j)),
                  pl.BlockSpec((1, tn), lambda j: (0, j))],
        out_specs=pl.BlockSpec((rows, tn), lambda j: (0, j)),
        out_shape=jax.ShapeDtypeStruct((rows, n), _F32),
        compiler_params=_params(("arbitrary",)),
        name="ada",
    )(c_all, w_ada, b_ada.reshape(1, n))


def _inproj_kernel(x_ref, shift_ref, scale_ref, g_ref, w_ref, cw_ref, st_ref,
                   q_ref, k_ref, v_ref, kb_ref, vt_ref, co_ref, cs_ref,
                   ug_scr, carry_scr, *, width, seg, tiles_per_batch, use_state, q_scale):
    i = pl.program_id(0)
    tm = x_ref.shape[0]
    h = _rms(x_ref[...]) * g_ref[...]
    h = (h * (1.0 + scale_ref[0]) + shift_ref[0]).astype(_BF16)

    def proj(j):
        return jnp.dot(h, w_ref[:, j * width:(j + 1) * width], preferred_element_type=_F32)

    q_ref[...] = (proj(0) * q_scale).astype(_BF16)
    zk = proj(1)
    k_ref[...] = zk
    kb_ref[...] = zk.astype(_BF16)
    zv = proj(2)
    v_ref[...] = zv
    vt_ref[...] = zv.T.astype(_BF16)

    ug = proj(3) * proj(5)
    gb = proj(4)
    w0 = cw_ref[0:1, :]
    w1 = cw_ref[1:2, :]
    w2 = cw_ref[2:3, :]
    if not use_state:
        @pl.when((i % tiles_per_batch) == 0)
        def _():
            carry_scr[...] = jnp.zeros_like(carry_scr)

    for s in range(tm // seg):
        lo = s * seg
        bnd = st_ref[s] if use_state else carry_scr[...]
        ug_s = ug[lo:lo + seg]
        ug_scr[6:8, :] = bnd
        ug_scr[8:8 + seg, :] = ug_s
        y = w2 * ug_s + w1 * ug_scr[7:7 + seg, :] + w0 * ug_scr[6:6 + seg, :]
        co_ref[lo:lo + seg, :] = (gb[lo:lo + seg] * y).astype(_BF16)
        last2 = ug_scr[6 + seg:8 + seg, :]
        cs_ref[s] = last2
        carry_scr[...] = last2


def _inproj(x2d, shift, scale, g_mix, w_in_bf, conv_w, state, *, tm, seg, tiles_per_batch,
            use_state, q_scale):
    r, d = x2d.shape
    width = w_in_bf.shape[1] // 6
    n_tiles = r // tm
    n_seg = tm // seg
    mod_rows = shift.shape[1]
    if use_state:
        mod_map = lambda i: (i, 0, 0)
        st_spec = pl.BlockSpec((n_seg, 2, width), lambda i: (i, 0, 0))
        cs_spec = pl.BlockSpec((n_seg, 2, width), lambda i: (i, 0, 0))
        n_state = n_tiles * n_seg
    else:
        mod_map = lambda i: (i // tiles_per_batch, 0, 0)
        st_spec = pl.BlockSpec((1, 2, width), lambda i: (0, 0, 0))
        cs_spec = pl.BlockSpec((1, 2, width), lambda i: (i // tiles_per_batch, 0, 0))
        n_state = n_tiles // tiles_per_batch
    kern = functools.partial(_inproj_kernel, width=width, seg=seg, tiles_per_batch=tiles_per_batch,
                             use_state=use_state, q_scale=q_scale)
    row_spec = lambda: pl.BlockSpec((tm, width), lambda i: (i, 0))
    return pl.pallas_call(
        kern,
        grid=(n_tiles,),
        in_specs=[pl.BlockSpec((tm, d), lambda i: (i, 0)),
                  pl.BlockSpec((1, mod_rows, d), mod_map),
                  pl.BlockSpec((1, mod_rows, d), mod_map),
                  pl.BlockSpec((1, d), lambda i: (0, 0)),
                  pl.BlockSpec((d, 6 * width), lambda i: (0, 0), pipeline_mode=pl.Buffered(1)),
                  pl.BlockSpec((3, width), lambda i: (0, 0)),
                  st_spec],
        out_specs=[row_spec(), row_spec(), row_spec(), row_spec(),
                   pl.BlockSpec((width, tm), lambda i: (0, i)),
                   row_spec(), cs_spec],
        out_shape=[jax.ShapeDtypeStruct((r, width), _BF16),
                   jax.ShapeDtypeStruct((r, width), _F32),
                   jax.ShapeDtypeStruct((r, width), _F32),
                   jax.ShapeDtypeStruct((r, width), _BF16),
                   jax.ShapeDtypeStruct((width, r), _BF16),
                   jax.ShapeDtypeStruct((r, width), _BF16),
                   jax.ShapeDtypeStruct((n_state, 2, width), _F32)],
        scratch_shapes=[pltpu.VMEM((seg + 8, width), _F32), pltpu.VMEM((2, width), _F32)],
        compiler_params=_params(("arbitrary",)),
        name="inproj",
    )(x2d, shift, scale, g_mix.reshape(1, d), w_in_bf, conv_w, state)


LOG2E = math.log2(math.e)
SKIP_MARGIN = 160.0
FAST_MARGIN = 60.0
N_AUG = 12


def _split3(v):
    t1 = v.astype(_BF16).astype(_F32)
    r = v - t1
    t2 = r.astype(_BF16).astype(_F32)
    t3 = (r - t2).astype(_BF16).astype(_F32)
    return t1, t2, t3


def _aug_group_term(idx):
    group = (idx >= 3).astype(jnp.int32) + (idx >= 6).astype(jnp.int32) + (idx >= 9).astype(jnp.int32)
    return group, idx - 3 * group


def _attn_prompt_kernel(scal_ref, q_ref, k_ref, vt_ref, g_ref, o_ref,
                        acc_scr, ml_scr, kn_scr, dbias_scr, obias_scr, aug_scr, *, tq, hd, out_scale):
    hh = pl.program_id(1)
    qi = pl.program_id(2)
    lam = scal_ref[0]
    slope2 = scal_ref[1 + hh]
    inv_slope2 = scal_ref[1 + N_HEADS + hh]
    tk = tq
    hw = 2 * hd
    n_kv_total = k_ref.shape[0] // tk

    @pl.when(qi == 0)
    def _():
        lane = lax.broadcasted_iota(jnp.int32, (tk, hw), 1)

        def chunk(c, mx):
            kk = k_ref[pl.ds(pl.multiple_of(c * tk, tk), tk), :].astype(_F32)
            sq = kk * kk
            n0 = jnp.sum(jnp.where(lane < hd, sq, 0.0), axis=1, keepdims=True)
            n1 = jnp.sum(jnp.where(lane >= hd, sq, 0.0), axis=1, keepdims=True)
            return jnp.maximum(mx[0], n0), jnp.maximum(mx[1], n1)

        zero = jnp.zeros((tk, 1), _F32)
        n0, n1 = lax.fori_loop(0, n_kv_total, chunk, (zero, zero))
        kn0 = jnp.sqrt(jnp.max(n0, axis=0, keepdims=True))
        kn1 = jnp.sqrt(jnp.max(n1, axis=0, keepdims=True))
        col = lax.broadcasted_iota(jnp.int32, (1, 2 * tq), 1)
        kn_scr[...] = jnp.where(col < tq, kn0, kn1) * 1.001

        jj = lax.broadcasted_iota(jnp.int32, (tk, tq), 0)
        ii = lax.broadcasted_iota(jnp.int32, (tk, tq), 1)
        jf = jj.astype(_F32)
        iif = ii.astype(_F32)
        visible = (jj // CHUNK) <= (ii // CHUNK)
        dbias = jnp.where(visible, slope2 * iif - slope2 * jnp.abs(iif - jf), MASKED_SCORE)
        dbias_scr[...] = jnp.concatenate([dbias, dbias], axis=1)
        obias = slope2 * jf
        obias_scr[...] = jnp.concatenate([obias, obias], axis=1)

        lc = lane & (hd - 1)
        group, _ = _aug_group_term(lc)
        jrow = lax.broadcasted_iota(jnp.int32, (tk, hw), 0)
        val = jnp.where(group == 0, jrow >> 4, jnp.where(group == 1, jrow & 15, jnp.where(group == 2, 0, 1)))
        aug_scr[...] = jnp.where(lc < N_AUG, val, 0).astype(_F32).astype(_BF16)

    qt = q_ref[...].astype(_F32).T
    qtb = qt.astype(_BF16)
    row2 = lax.broadcasted_iota(jnp.int32, (hw, 2 * tq), 0)
    col2 = lax.broadcasted_iota(jnp.int32, (hw, 2 * tq), 1)
    rhs_bd = jnp.where((row2 < hd) == (col2 < tq), jnp.concatenate([qtb, qtb], axis=1), 0).astype(_BF16)

    def kv_tiles(kv):
        start = pl.multiple_of(kv * tk, tk)
        return k_ref[pl.ds(start, tk), :], vt_ref[:, pl.ds(start, tk)]

    k, vt = kv_tiles(qi)
    s = jnp.dot(k, rhs_bd, preferred_element_type=_F32) + dbias_scr[...]
    m0 = jnp.max(s, axis=0, keepdims=True)
    p = jnp.exp2(s - m0)
    l0 = jnp.sum(p, axis=0, keepdims=True)
    acc_scr[...] = jnp.dot(vt, p.astype(_BF16), preferred_element_type=_F32)

    qsq = qt * qt
    qn = jnp.concatenate([jnp.sqrt(jnp.sum(qsq[:hd], axis=0, keepdims=True)),
                          jnp.sqrt(jnp.sum(qsq[hd:], axis=0, keepdims=True))], axis=1)
    gap = jnp.max(qn * kn_scr[...] - m0, axis=1, keepdims=True)
    reach = ((gap + SKIP_MARGIN) * inv_slope2 - 1.0) * (1.0 / tk)
    n_off = jnp.clip(jnp.floor(reach) + 1.0, 0.0, qi.astype(_F32)).astype(jnp.int32)[0, 0]
    fast = (gap <= FAST_MARGIN).astype(jnp.int32)[0, 0] == 1
    kv_lo = qi - n_off

    @pl.when(fast)
    def _():
        lane_lo = lax.broadcasted_iota(jnp.int32, (tk, hw), 1) < hd
        slab_rows = 16
        local = lax.broadcasted_iota(jnp.int32, (slab_rows, tq), 0)
        group, term = _aug_group_term(local)

        def bias_slab(m_half):
            val = jnp.where(group == 0, slope2 * 16.0,
                            jnp.where(group == 1, slope2, jnp.where(group == 2, -slope2 * tk, -m_half)))
            t1, t2, t3 = _split3(val)
            v = jnp.where(term == 0, t1, jnp.where(term == 1, t2, t3))
            return jnp.where(local < N_AUG, v, 0.0)

        pad = jnp.zeros((hd - slab_rows, tq), _F32)
        rhs0 = jnp.concatenate([qt[:hd], bias_slab(m0[:, :tq]), pad], axis=0).astype(_BF16)
        rhs1 = jnp.concatenate([bias_slab(m0[:, tq:]), pad, qt[hd:]], axis=0).astype(_BF16)
        lcol = lax.broadcasted_iota(jnp.int32, (1, hw), 1) & (hd - 1)
        delta_cols = ((lcol >= 6) & (lcol < 9)).astype(_F32)

        def tile(kv):
            k, vt = kv_tiles(kv)
            delta = (qi - kv).astype(_F32)
            aug = aug_scr[...] + (delta * delta_cols).astype(_BF16)
            s = jnp.concatenate(
                [jnp.dot(jnp.where(lane_lo, k, aug), rhs0, preferred_element_type=_F32),
                 jnp.dot(jnp.where(lane_lo, aug, k), rhs1, preferred_element_type=_F32)], axis=1)
            p = jnp.exp2(s)
            return (jnp.sum(p.reshape(tk // 8, 8, 2 * tq), axis=0),
                    jnp.dot(vt, p.astype(_BF16), preferred_element_type=_F32))

        def pair(j, l8):
            kv = kv_lo + 2 * j
            la, pva = tile(kv)
            lb, pvb = tile(kv + 1)
            acc_scr[...] = acc_scr[...] + (pva + pvb)
            return l8 + (la + lb)

        l8 = lax.fori_loop(0, lax.shift_right_logical(n_off, 1), pair, jnp.zeros((8, 2 * tq), _F32))
        ml_scr[...] = l0 + jnp.sum(l8, axis=0, keepdims=True)

        @pl.when((n_off & 1) == 1)
        def _():
            la, pva = tile(qi - 1)
            acc_scr[...] = acc_scr[...] + pva
            ml_scr[...] = ml_scr[...] + jnp.sum(la, axis=0, keepdims=True)

    @pl.when(jnp.logical_not(fast))
    def _():
        def body(kv, carry):
            m, l = carry
            c = -slope2 * ((qi - kv) * tk).astype(_F32)
            k, vt = kv_tiles(kv)
            s = jnp.dot(k, rhs_bd, preferred_element_type=_F32) + obias_scr[...]
            m_new = jnp.maximum(m, jnp.max(s, axis=0, keepdims=True) + c)
            p = jnp.exp2(s - (m_new - c))
            alpha = jnp.exp2(m - m_new)
            l = alpha * l + jnp.sum(p, axis=0, keepdims=True)
            acc_scr[...] = acc_scr[...] * alpha + jnp.dot(vt, p.astype(_BF16), preferred_element_type=_F32)
            return m_new, l

        _, l = lax.fori_loop(kv_lo, qi, body, (m0, l0))
        ml_scr[...] = l

    acc = acc_scr[...] / ml_scr[...]
    o = acc[:, :tq] - lam * acc[:, tq:]
    o = o * lax.rsqrt(jnp.mean(o * o, axis=0, keepdims=True) + NORM_EPS)
    o = o * g_ref[...] * out_scale
    o_ref[...] = o.T.astype(_BF16)


def _attn_prompt(scal, q, kb, vt, subln_col, *, batch, seq, tq, out_scale):
    width = q.shape[1]
    hw = width // N_HEADS
    nq = seq // tq
    kern = functools.partial(_attn_prompt_kernel, tq=tq, hd=hw // 2, out_scale=out_scale)
    return pl.pallas_call(
        kern,
        grid=(batch, N_HEADS, nq),
        in_specs=[pl.BlockSpec(memory_space=pltpu.SMEM),
                  pl.BlockSpec((tq, hw), lambda b, h, i: (b * nq + i, h)),
                  pl.BlockSpec((seq, hw), lambda b, h, i: (b, h)),
                  pl.BlockSpec((hw, seq), lambda b, h, i: (h, b)),
                  pl.BlockSpec((hw, 1), lambda b, h, i: (0, 0))],
        out_specs=pl.BlockSpec((tq, hw), lambda b, h, i: (b * nq + i, h)),
        out_shape=jax.ShapeDtypeStruct((batch * seq, width), _BF16),
        scratch_shapes=[pltpu.VMEM((hw, 2 * tq), _F32),
                        pltpu.VMEM((1, 2 * tq), _F32),
                        pltpu.VMEM((1, 2 * tq), _F32),
                        pltpu.VMEM((tq, 2 * tq), _F32),
                        pltpu.VMEM((tq, 2 * tq), _F32),
                        pltpu.VMEM((tq, hw), _BF16)],
        compiler_params=_params(("arbitrary", "arbitrary", "arbitrary")),
        name="attn_prompt",
    )(scal, q, kb, vt, subln_col)


def _attn_sample_kernel(scal_ref, q_ref, kn_ref, vn_ref, kc_ref, vc_ref, g_ref, o_ref, *, hd, out_scale):
    hh = pl.program_id(1)
    lam = scal_ref[0]
    slope = scal_ref[1 + hh]
    t = q_ref.shape[0]
    past = kc_ref.shape[1]
    q = q_ref[...]
    kc = kc_ref[0].astype(_BF16)
    kn = kn_ref[...].astype(_BF16)
    vc = vc_ref[0].astype(_BF16)
    vn = vn_ref[...].astype(_BF16)
    qpos = lax.broadcasted_iota(jnp.int32, (t, past), 0).astype(_F32) + float(past)
    kpos = lax.broadcasted_iota(jnp.int32, (t, past), 1).astype(_F32)
    bias_c = -slope * jnp.abs(qpos - kpos)
    tq_i = lax.broadcasted_iota(jnp.int32, (t, t), 0).astype(_F32)
    tk_i = lax.broadcasted_iota(jnp.int32, (t, t), 1).astype(_F32)
    bias_n = -slope * jnp.abs(tq_i - tk_i)
    nt = (((1,), (1,)), ((), ()))
    outs = []
    for j in range(2):
        sl = slice(j * hd, (j + 1) * hd)
        sc = lax.dot_general(q[:, sl], kc[:, sl], nt, preferred_element_type=_F32) + bias_c
        sn = lax.dot_general(q[:, sl], kn[:, sl], nt, preferred_element_type=_F32) + bias_n
        m = jnp.maximum(jnp.max(sc, axis=-1, keepdims=True), jnp.max(sn, axis=-1, keepdims=True))
        pc = jnp.exp(sc - m)
        pn = jnp.exp(sn - m)
        l = jnp.sum(pc, axis=-1, keepdims=True) + jnp.sum(pn, axis=-1, keepdims=True)
        o = (jnp.dot(pc.astype(_BF16), vc, preferred_element_type=_F32)
             + jnp.dot(pn.astype(_BF16), vn, preferred_element_type=_F32))
        outs.append(o / l)
    o = outs[0] - lam * outs[1]
    o = _rms(o) * g_ref[...] * out_scale
    o_ref[...] = o.astype(_BF16)


def _attn_sample(scal, q, k_new, v_new, cache_k, cache_v, subln_row, *, batch, t, out_scale):
    width = q.shape[1]
    hw = width // N_HEADS
    past = cache_k.shape[1]
    kern = functools.partial(_attn_sample_kernel, hd=hw // 2, out_scale=out_scale)
    new_spec = lambda: pl.BlockSpec((t, hw), lambda b, h: (b, h))
    cache_spec = lambda: pl.BlockSpec((1, past, hw), lambda b, h: (b, 0, h))
    return pl.pallas_call(
        kern,
        grid=(batch, N_HEADS),
        in_specs=[pl.BlockSpec(memory_space=pltpu.SMEM), new_spec(), new_spec(), new_spec(),
                  cache_spec(), cache_spec(), pl.BlockSpec((1, hw), lambda b, h: (0, 0))],
        out_specs=new_spec(),
        out_shape=jax.ShapeDtypeStruct((batch * t, width), _BF16),
        compiler_params=_params(("arbitrary", "arbitrary")),
        name="attn_sample",
    )(scal, q, k_new, v_new, cache_k, cache_v, subln_row)


def _outproj_kernel(a_ref, c_ref, x_ref, gate_ref, shift_ref, scale_ref, g_ref, wo_ref, wrh_ref, wrl_ref,
                    br_ref, *rest, half, n_tiles):
    x1_ref, h2_ref, lg_ref = rest[-3:]
    i = pl.program_id(0)

    @pl.when(i < n_tiles)
    def _():
        mix = (jnp.dot(a_ref[...], wo_ref[0:half, :], preferred_element_type=_F32)
               + jnp.dot(c_ref[...], wo_ref[half:2 * half, :], preferred_element_type=_F32))
        x1 = x_ref[...] + gate_ref[0] * mix
        x1_ref[...] = x1
        h2 = _rms(x1) * g_ref[...]
        h2 = h2 * (1.0 + scale_ref[0]) + shift_ref[0]
        hi = h2.astype(_BF16)
        dh = h2.shape[1] // 2
        bits = lax.bitcast_convert_type(hi.astype(_F32), jnp.uint32)
        h2_ref[...] = bits[:, :dh] | lax.shift_right_logical(bits[:, dh:], jnp.uint32(16))
        lo = (h2 - hi.astype(_F32)).astype(_BF16)
        lg_ref[...] = (jnp.dot(hi, wrh_ref[...], preferred_element_type=_F32)
                       + jnp.dot(lo, wrh_ref[...], preferred_element_type=_F32)
                       + jnp.dot(hi, wrl_ref[...], preferred_element_type=_F32)) + br_ref[...]

    @pl.when(i >= n_tiles)
    def _():
        h2_ref[...] = jnp.zeros_like(h2_ref)
        lg_ref[...] = jnp.zeros_like(lg_ref)


def _outproj(attn, conv, x2d, gate, shift, scale, g_ffn, w_o_bf, w_router_pad, b_router_pad, *, tm,
             tiles_per_batch, per_row, rows_total, row_off, into=None):
    w_router_hi = w_router_pad.astype(_BF16)
    w_router_lo = (w_router_pad - w_router_hi.astype(_F32)).astype(_BF16)
    r, d = x2d.shape
    half = attn.shape[1]
    mod_rows = gate.shape[1]
    blk_off = row_off // tm
    assert row_off % tm == 0
    n_tiles = r // tm
    n_fill = (rows_total - row_off - r) // tm if into is None else 0
    last = n_tiles - 1
    row_map = lambda i: (jnp.minimum(i, last), 0)
    if per_row:
        mod_map = lambda i: (jnp.minimum(i, last), 0, 0)
    else:
        mod_map = lambda i: (jnp.minimum(i, last) // tiles_per_batch, 0, 0)
    mod_spec = lambda: pl.BlockSpec((1, mod_rows, d), mod_map)
    kern = functools.partial(_outproj_kernel, half=half, n_tiles=n_tiles)
    in_specs = [pl.BlockSpec((tm, half), row_map),
                pl.BlockSpec((tm, half), row_map),
                pl.BlockSpec((tm, d), row_map),
                mod_spec(), mod_spec(), mod_spec(),
                pl.BlockSpec((1, d), lambda i: (0, 0)),
                pl.BlockSpec((2 * half, d), lambda i: (0, 0), pipeline_mode=pl.Buffered(1)),
                pl.BlockSpec((d, ROUTER_LANES), lambda i: (0, 0)),
                pl.BlockSpec((d, ROUTER_LANES), lambda i: (0, 0)),
                pl.BlockSpec((1, ROUTER_LANES), lambda i: (0, 0))]
    args = [attn, conv, x2d, gate, shift, scale, g_ffn.reshape(1, d), w_o_bf, w_router_hi, w_router_lo,
            b_router_pad]
    aliases = {}
    if into is not None:
        aliases = {len(args): 1, len(args) + 1: 2}
        in_specs += [pl.BlockSpec(memory_space=pl.ANY), pl.BlockSpec(memory_space=pl.ANY)]
        args += list(into)
    return pl.pallas_call(
        kern,
        grid=(n_tiles + n_fill,),
        in_specs=in_specs,
        out_specs=[pl.BlockSpec((tm, d), row_map),
                   pl.BlockSpec((tm, d // 2), lambda i: (blk_off + i, 0)),
                   pl.BlockSpec((tm, ROUTER_LANES), lambda i: (blk_off + i, 0))],
        out_shape=[jax.ShapeDtypeStruct((r, d), _F32),
                   jax.ShapeDtypeStruct((rows_total, d // 2), jnp.uint32),
                   jax.ShapeDtypeStruct((rows_total, ROUTER_LANES), _F32)],
        input_output_aliases=aliases,
        compiler_params=_params(("arbitrary",)),
        name="outproj",
    )(*args)


def _row_gather_start(src_hbm, idx_ref, n_rows, dst_of_row, sem):
    for r in range(n_rows):
        pltpu.make_async_copy(src_hbm.at[pl.ds(idx_ref[0, 0, r], 1), :], dst_of_row(r), sem).start(
            priority=r % 2)


def _dispatch_kernel(nv_ref, idx_ref, idx_next_ref, h_hbm, o_ref, buf, sem):
    i = pl.program_id(0)
    nv = nv_ref[0]
    tm = o_ref.shape[0]
    slot = i % 2

    def start(idx_blk_ref, s):
        _row_gather_start(h_hbm, idx_blk_ref, tm, lambda r: buf.at[s, pl.ds(r, 1), :], sem.at[s])

    @pl.when(i == 0)
    def _():
        start(idx_ref, 0)

    @pl.when(i + 1 < nv)
    def _():
        start(idx_next_ref, 1 - slot)

    @pl.when(i < nv)
    def _():
        pltpu.make_async_copy(h_hbm.at[pl.ds(0, tm), :], buf.at[slot], sem.at[slot]).wait()
        u = buf[slot]
        dh = u.shape[1]
        o_ref[:, :dh] = lax.bitcast_convert_type(u & jnp.uint32(0xFFFF0000), _F32).astype(_BF16)
        o_ref[:, dh:] = lax.bitcast_convert_type(lax.shift_left(u, jnp.uint32(16)), _F32).astype(_BF16)

    @pl.when(i >= nv)
    def _():
        o_ref[...] = jnp.zeros_like(o_ref)


def _dispatch(n_valid, tok_sorted, h2_all, *, tm):
    rows = tok_sorted.shape[0]
    n_blocks = rows // tm
    dh = h2_all.shape[1]
    d = 2 * dh
    idx3 = tok_sorted.reshape(n_blocks, 1, tm)
    return pl.pallas_call(
        _dispatch_kernel,
        grid_spec=pltpu.PrefetchScalarGridSpec(
            num_scalar_prefetch=1,
            grid=(n_blocks,),
            in_specs=[pl.BlockSpec((1, 1, tm), lambda i, nv: (i, 0, 0), memory_space=pltpu.SMEM),
                      pl.BlockSpec((1, 1, tm), lambda i, nv: (jnp.minimum(i + 1, n_blocks - 1), 0, 0),
                                   memory_space=pltpu.SMEM),
                      pl.BlockSpec(memory_space=pl.ANY)],
            out_specs=pl.BlockSpec((tm, d), lambda i, nv: (i, 0)),
            scratch_shapes=[pltpu.VMEM((2, tm, dh), jnp.uint32), pltpu.SemaphoreType.DMA((2,))]),
        out_shape=jax.ShapeDtypeStruct((rows, d), _BF16),
        compiler_params=_params(("arbitrary",)),
        name="dispatch",
    )(n_valid, idx3, idx3, h2_all)


def _expert_changed(be_ref, i):
    return jnp.logical_or(i == 0, be_ref[i] != be_ref[jnp.maximum(i - 1, 0)])


def _moe_up_kernel(be_ref, nv_ref, x_ref, wg_ref, wu_ref, bg_ref, bu_ref, act_ref, wg_scr, wu_scr):
    i = pl.program_id(1)
    valid = i < nv_ref[0]

    @pl.when(jnp.logical_and(valid, _expert_changed(be_ref, i)))
    def _():
        wg_scr[...] = wg_ref[0].astype(_BF16)
        wu_scr[...] = wu_ref[0].astype(_BF16)

    @pl.when(valid)
    def _():
        x = x_ref[...]
        g = jnp.dot(x, wg_scr[...], preferred_element_type=_F32) + bg_ref[0]
        u = jnp.dot(x, wu_scr[...], preferred_element_type=_F32) + bu_ref[0]
        g = jnp.minimum(g, SWIGLU_LIMIT)
        u = jnp.clip(u, -SWIGLU_LIMIT, SWIGLU_LIMIT)
        act_ref[...] = ((u + 1.0) * g * jax.nn.sigmoid(SWIGLU_ALPHA * g)).astype(_BF16)

    @pl.when(jnp.logical_not(valid))
    def _():
        act_ref[...] = jnp.zeros_like(act_ref)


def _moe_down_kernel(be_ref, nv_ref, a_ref, wd_ref, bd_ref, y_ref, wd_scr):
    i = pl.program_id(1)
    valid = i < nv_ref[0]

    @pl.when(jnp.logical_and(valid, _expert_changed(be_ref, i)))
    def _():
        wd_scr[...] = wd_ref[0].astype(_BF16)

    @pl.when(valid)
    def _():
        y_ref[...] = jnp.dot(a_ref[...], wd_scr[...], preferred_element_type=_F32) + bd_ref[0]

    @pl.when(jnp.logical_not(valid))
    def _():
        y_ref[...] = jnp.zeros_like(y_ref)


def _moe(block_e, n_valid, xs, w_gu, b_gu, w_down, b_down, *, tm, tn, tn_down):
    rows, d = xs.shape
    n_exp, _, two_ff = w_gu.shape
    d_ff = two_ff // 2
    n_blocks = rows // tm
    n_up = d_ff // tn
    b_gu3 = b_gu.reshape(n_exp, 1, two_ff)
    act = pl.pallas_call(
        _moe_up_kernel,
        grid_spec=pltpu.PrefetchScalarGridSpec(
            num_scalar_prefetch=2,
            grid=(n_up, n_blocks),
            in_specs=[pl.BlockSpec((tm, d), lambda n, i, be, nv: (i, 0)),
                      pl.BlockSpec((1, d, tn), lambda n, i, be, nv: (be[i], 0, n)),
                      pl.BlockSpec((1, d, tn), lambda n, i, be, nv: (be[i], 0, n_up + n)),
                      pl.BlockSpec((1, 1, tn), lambda n, i, be, nv: (be[i], 0, n)),
                      pl.BlockSpec((1, 1, tn), lambda n, i, be, nv: (be[i], 0, n_up + n))],
            out_specs=pl.BlockSpec((tm, tn), lambda n, i, be, nv: (i, n)),
            scratch_shapes=[pltpu.VMEM((d, tn), _BF16), pltpu.VMEM((d, tn), _BF16)]),
        out_shape=jax.ShapeDtypeStruct((rows, d_ff), _BF16),
        compiler_params=_params(("arbitrary", "arbitrary")),
        name="moe_up",
    )(block_e, n_valid, xs, w_gu, w_gu, b_gu3, b_gu3)
    n_down = d // tn_down
    return pl.pallas_call(
        _moe_down_kernel,
        grid_spec=pltpu.PrefetchScalarGridSpec(
            num_scalar_prefetch=2,
            grid=(n_down, n_blocks),
            in_specs=[pl.BlockSpec((tm, d_ff), lambda n, i, be, nv: (i, 0)),
                      pl.BlockSpec((1, d_ff, tn_down), lambda n, i, be, nv: (be[i], 0, n)),
                      pl.BlockSpec((1, 1, tn_down), lambda n, i, be, nv: (be[i], 0, n))],
            out_specs=pl.BlockSpec((tm, tn_down), lambda n, i, be, nv: (i, n)),
            scratch_shapes=[pltpu.VMEM((d_ff, tn_down), _BF16)]),
        out_shape=jax.ShapeDtypeStruct((rows, d), _F32),
        compiler_params=_params(("arbitrary", "arbitrary")),
        name="moe_down",
    )(block_e, n_valid, act, w_down, b_down.reshape(n_exp, 1, d))


def _combine_kernel(idx_ref, idx_next_ref, x1_ref, gk_ref, gate_ref, g_ref, ys_hbm, y_ref, buf, sem):
    i = pl.program_id(0)
    n = pl.num_programs(0)
    tm, d = x1_ref.shape
    slot = i % 2

    def start(idx_blk_ref, s):
        _row_gather_start(ys_hbm, idx_blk_ref, tm * TOP_K,
                          lambda r: buf.at[s, r % TOP_K, pl.ds(r // TOP_K, 1), :], sem.at[s])

    @pl.when(i == 0)
    def _():
        start(idx_ref, 0)

    @pl.when(i + 1 < n)
    def _():
        start(idx_next_ref, 1 - slot)

    for k in range(TOP_K):
        pltpu.make_async_copy(ys_hbm.at[pl.ds(0, tm), :], buf.at[slot, k], sem.at[slot]).wait()
    gk = gk_ref[...]
    ff = gk[:, 0:1] * buf[slot, 0]
    for k in range(1, TOP_K):
        ff = ff + gk[:, k:k + 1] * buf[slot, k]
    x2 = x1_ref[...] + gate_ref[0] * ff
    y_ref[...] = _rms(x2) * g_ref[...]


def _combine(x1, ys, dest, gk, gate, g_final, *, tm, tiles_per_batch, per_row):
    r, d = x1.shape
    n_tiles = r // tm
    mod_rows = gate.shape[1]
    mod_map = (lambda i: (i, 0, 0)) if per_row else (lambda i: (i // tiles_per_batch, 0, 0))
    idx3 = dest.reshape(n_tiles, 1, tm * TOP_K)
    return pl.pallas_call(
        _combine_kernel,
        grid=(n_tiles,),
        in_specs=[pl.BlockSpec((1, 1, tm * TOP_K), lambda i: (i, 0, 0), memory_space=pltpu.SMEM),
                  pl.BlockSpec((1, 1, tm * TOP_K), lambda i: (jnp.minimum(i + 1, n_tiles - 1), 0, 0),
                               memory_space=pltpu.SMEM),
                  pl.BlockSpec((tm, d), lambda i: (i, 0)),
                  pl.BlockSpec((tm, TOP_K), lambda i: (i, 0)),
                  pl.BlockSpec((1, mod_rows, d), mod_map),
                  pl.BlockSpec((1, d), lambda i: (0, 0)),
                  pl.BlockSpec(memory_space=pl.ANY)],
        out_specs=pl.BlockSpec((tm, d), lambda i: (i, 0)),
        out_shape=jax.ShapeDtypeStruct((r, d), _F32),
        scratch_shapes=[pltpu.VMEM((2, TOP_K, tm, d), _F32), pltpu.SemaphoreType.DMA((2,))],
        compiler_params=_params(("arbitrary",)),
        name="combine",
    )(idx3, idx3, x1, gk, gate, g_final.reshape(1, d), ys)


def _route(logits, n_exp, tm):
    t = logits.shape[0]
    top_v, top_e = lax.top_k(logits, TOP_K)
    gate = jax.nn.softmax(top_v, axis=-1)
    m = t * TOP_K
    n_blocks = -(-m // tm) + n_exp
    rows = n_blocks * tm
    i32 = jnp.int32
    flat_e = top_e.reshape(-1).astype(i32)
    onehot = (flat_e[:, None] == jnp.arange(n_exp, dtype=i32)[None, :]).astype(i32)
    csum = jnp.cumsum(onehot, axis=0)
    rank = jnp.sum(csum * onehot, axis=1) - 1
    counts = csum[-1]
    padded = ((counts + tm - 1) // tm) * tm
    cum_end = jnp.cumsum(padded)
    pstart = cum_end - padded
    cstart = jnp.cumsum(counts) - counts
    n_valid = (cum_end[-1] // tm).astype(i32).reshape(1)
    blk_start = jnp.arange(n_blocks, dtype=i32) * tm
    n_before = jnp.sum((cum_end[None, :] <= blk_start[:, None]).astype(i32), axis=1)
    block_e = jnp.minimum(n_before, n_exp - 1)
    last_e = jnp.sum(jnp.where(jnp.arange(n_blocks) == n_valid[0] - 1, block_e, 0))
    block_e = jnp.where(jnp.arange(n_blocks) < n_valid[0], block_e, last_e).astype(i32)
    assert n_exp * m < 2 ** 31
    order = lax.sort(flat_e * m + jnp.arange(m, dtype=i32)) % m
    row = jnp.arange(rows, dtype=i32)
    e_row = jnp.repeat(block_e, tm)
    off = row - pstart[e_row]
    is_slot = (off < counts[e_row]) & (row < cum_end[-1])
    src = jnp.clip(cstart[e_row] + off, 0, m - 1)
    tok_sorted = jnp.where(is_slot, order[src] // TOP_K, 0).astype(i32)
    dest = (jnp.sum(onehot * pstart[None, :], axis=1) + rank).astype(i32)
    return gate, dest.reshape(t, TOP_K), tok_sorted, block_e, n_valid


def _pick_tile(n, pref):
    t = min(n, pref)
    while n % t:
        t //= 2
    return t


def kernel(x_prompt, x_sample, cache_k, cache_v, state_conv, c_prompt, c_sample, g_mix, g_ffn, w_ada, b_ada, w_in, lambda_q1, lambda_k1, lambda_q2, lambda_k2, subln_g, conv_w, w_o, w_router, b_router, w_gu, b_gu, w_down, b_down, g_final):
    depth = g_mix.shape[0]
    assert depth == 1
    bp, sp, d = x_prompt.shape
    bs, ts, _ = x_sample.shape
    past = cache_k.shape[2]
    aw = d // 2
    hw = aw // N_HEADS
    hd = hw // 2
    n_exp = w_router.shape[-1]
    layer = 0

    n_c = bp + bs
    c_rows = -(-n_c // 8) * 8
    c_all = jnp.concatenate([c_prompt, c_sample, jnp.zeros((c_rows - n_c, d), _F32)], axis=0)
    ada = _ada(c_all, w_ada[layer], b_ada[layer])
    shift1, scale1, gate1, shift2, scale2, gate2 = [ada[:, i * d:(i + 1) * d] for i in range(6)]

    lam_init = 0.8 - 0.6 * math.exp(-0.3 * layer)
    lam = (jnp.exp(jnp.sum(lambda_q1[layer] * lambda_k1[layer]))
           - jnp.exp(jnp.sum(lambda_q2[layer] * lambda_k2[layer])) + lam_init)
    slopes = 2.0 ** (-8.0 * np.arange(1, N_HEADS + 1) / N_HEADS)
    lam1 = lam.reshape(1).astype(_F32)
    scal_s = jnp.concatenate([lam1, jnp.asarray(slopes, _F32)])
    scal_p = jnp.concatenate([lam1, jnp.asarray(slopes * LOG2E, _F32), jnp.asarray(1.0 / (slopes * LOG2E), _F32)])
    out_scale = 1.0 - lam_init
    q_scale = hd ** -0.5

    w_in_bf = w_in[layer].astype(_BF16)
    w_o_bf = w_o[layer].astype(_BF16)
    w_router_pad = jnp.zeros((d, ROUTER_LANES), _F32).at[:, :n_exp].set(w_router[layer])
    b_router_pad = jnp.zeros((1, ROUTER_LANES), _F32).at[0, :n_exp].set(b_router[layer])

    rp = bp * sp
    tm_p = _pick_tile(sp, 512)
    tpb = sp // tm_p
    xp2 = x_prompt.reshape(rp, d)
    mod_p = lambda a: a[:bp].reshape(bp, 1, d)
    qp, kp, vp, kbp, vtp, cop, csp = _inproj(
        xp2, mod_p(shift1), mod_p(scale1), g_mix[layer], w_in_bf, conv_w[layer],
        jnp.zeros((1, 2, aw), _F32), tm=tm_p, seg=tm_p, tiles_per_batch=tpb, use_state=False,
        q_scale=q_scale * LOG2E)
    tq = _pick_tile(sp, 512)
    ap = _attn_prompt(scal_p, qp, kbp, vtp, subln_g[layer].reshape(hw, 1), batch=bp, seq=sp, tq=tq,
                      out_scale=out_scale)
    rs = bs * ts
    x1p, h2_all, lg_all = _outproj(ap, cop, xp2, mod_p(gate1), mod_p(shift2), mod_p(scale2), g_ffn[layer],
                                   w_o_bf, w_router_pad, b_router_pad, tm=tm_p, tiles_per_batch=tpb,
                                   per_row=False, rows_total=rp + rs, row_off=0)

    mod_s = lambda a: jnp.repeat(a[bp:bp + bs], ts, axis=0).reshape(1, rs, d)
    xs2 = x_sample.reshape(rs, d)
    qs, ks, vs, _, _, cos, css = _inproj(
        xs2, mod_s(shift1), mod_s(scale1), g_mix[layer], w_in_bf, conv_w[layer],
        state_conv[layer], tm=rs, seg=ts, tiles_per_batch=1, use_state=True, q_scale=q_scale)
    a_s = _attn_sample(scal_s, qs, ks, vs, cache_k[layer].reshape(bs, past, aw),
                       cache_v[layer].reshape(bs, past, aw), subln_g[layer].reshape(1, hw),
                       batch=bs, t=ts, out_scale=out_scale)
    x1s, h2_all, lg_all = _outproj(a_s, cos, xs2, mod_s(gate1), mod_s(shift2), mod_s(scale2), g_ffn[layer],
                                   w_o_bf, w_router_pad, b_router_pad, tm=rs, tiles_per_batch=1,
                                   per_row=True, rows_total=rp + rs, row_off=rp, into=(h2_all, lg_all))

    tm_e = 512
    gate, dest, tok_sorted, block_e, n_valid = _route(lg_all[:, :n_exp], n_exp, tm_e)
    xs_sorted = _dispatch(n_valid, tok_sorted, h2_all, tm=tm_e)
    d_ff = w_down.shape[2]
    ys = _moe(block_e, n_valid, xs_sorted, w_gu[layer], b_gu[layer], w_down[layer], b_down[layer],
              tm=tm_e, tn=_pick_tile(d_ff, 1024), tn_down=_pick_tile(d, 2048))

    tm_c = _pick_tile(sp, 128)
    y_p = _combine(x1p, ys, dest[:rp], gate[:rp], mod_p(gate2), g_final, tm=tm_c,
                   tiles_per_batch=sp // tm_c, per_row=False)
    tm_cs = _pick_tile(rs, 128)
    gate2_s = mod_s(gate2).reshape(rs // tm_cs, tm_cs, d)
    y_s = _combine(x1s, ys, dest[rp:], gate[rp:], gate2_s, g_final, tm=tm_cs, tiles_per_batch=1,
                   per_row=True)

    return (y_p.reshape(bp, sp, d), y_s.reshape(bs, ts, d),
            kp.reshape(1, bp, sp, N_HEADS, 2, hd), vp.reshape(1, bp, sp, N_HEADS, hw),
            csp.reshape(1, bp, 2, aw),
            ks.reshape(1, bs, ts, N_HEADS, 2, hd), vs.reshape(1, bs, ts, N_HEADS, hw),
            css.reshape(1, bs, 2, aw))
```

```python
import functools
import math

import jax
import jax.numpy as jnp
import numpy as np
from jax import lax
from jax.experimental import pallas as pl
from jax.experimental.pallas import tpu as pltpu

_F32 = jnp.float32
_BF16 = jnp.bfloat16

CHUNK = 64
N_HEADS = 8
TOP_K = 4
NORM_EPS = 1e-6
SWIGLU_ALPHA = 1.702
SWIGLU_LIMIT = 7.0
MASKED_SCORE = -1e30
ROUTER_LANES = 128

ROW_TILE = 512
COMBINE_TILE = 128
ADA_COL_TILE = 1024
EXPERT_UP_COL_TILE = 1024
EXPERT_DOWN_COL_TILE = 2048

V7X_VMEM_LIMIT_BYTES = 56 * 1024 * 1024


def _params(semantics, vmem_bytes=V7X_VMEM_LIMIT_BYTES):
    return pltpu.CompilerParams(dimension_semantics=semantics, vmem_limit_bytes=vmem_bytes)


def _rms(x):
    return x * lax.rsqrt(jnp.mean(x * x, axis=-1, keepdims=True) + NORM_EPS)


def _ada_kernel(c_ref, w_ref, b_ref, o_ref):
    c = c_ref[...]
    s = (c * jax.nn.sigmoid(c)).astype(_BF16)
    o_ref[...] = jnp.dot(s, w_ref[...].astype(_BF16), preferred_element_type=_F32) + b_ref[...]


def _ada(c_all, w_ada, b_ada, tn=ADA_COL_TILE):
    rows, d = c_all.shape
    n = w_ada.shape[1]
    return pl.pallas_call(
        _ada_kernel,
        grid=(n // tn,),
        in_specs=[pl.BlockSpec((rows, d), lambda j: (0, 0)),
                  pl.BlockSpec((d, tn), lambda j: (0, j)),
                  pl.BlockSpec((1, tn), lambda j: (0, j))],
        out_specs=pl.BlockSpec((rows, tn), lambda j: (0, j)),
        out_shape=jax.ShapeDtypeStruct((rows, n), _F32),
        compiler_params=_params(("arbitrary",)),
        name="ada",
    )(c_all, w_ada, b_ada.reshape(1, n))


def _inproj_kernel(x_ref, shift_ref, scale_ref, g_ref, w_ref, cw_ref, st_ref,
                   q_ref, k_ref, v_ref, kb_ref, vt_ref, co_ref, cs_ref,
                   ug_scr, carry_scr, *, width, seg, tiles_per_batch, use_state, q_scale, k_transposed):
    i = pl.program_id(0)
    tm = x_ref.shape[0]
    h = _rms(x_ref[...]) * g_ref[...]
    h = (h * (1.0 + scale_ref[0]) + shift_ref[0]).astype(_BF16)

    def proj(j):
        return jnp.dot(h, w_ref[:, j * width:(j + 1) * width], preferred_element_type=_F32)

    q_ref[...] = (proj(0) * q_scale).astype(_BF16)
    zk = proj(1)
    if k_transposed:
        k_ref[0] = zk.T
    else:
        k_ref[...] = zk
    kb_ref[...] = zk.astype(_BF16)
    zv = proj(2)
    v_ref[...] = zv
    vt_ref[...] = zv.T.astype(_BF16)

    ug = proj(3) * proj(5)
    gb = proj(4)
    w0 = cw_ref[0:1, :]
    w1 = cw_ref[1:2, :]
    w2 = cw_ref[2:3, :]
    if not use_state:
        @pl.when((i % tiles_per_batch) == 0)
        def _():
            carry_scr[...] = jnp.zeros_like(carry_scr)

    for s in range(tm // seg):
        lo = s * seg
        bnd = st_ref[s] if use_state else carry_scr[...]
        ug_s = ug[lo:lo + seg]
        ug_scr[6:8, :] = bnd
        ug_scr[8:8 + seg, :] = ug_s
        y = w2 * ug_s + w1 * ug_scr[7:7 + seg, :] + w0 * ug_scr[6:6 + seg, :]
        co_ref[lo:lo + seg, :] = (gb[lo:lo + seg] * y).astype(_BF16)
        last2 = ug_scr[6 + seg:8 + seg, :]
        cs_ref[s] = last2
        carry_scr[...] = last2


def _inproj(x2d, shift, scale, g_mix, w_in_bf, conv_w, state, *, tm, seg, tiles_per_batch,
            use_state, q_scale, k_transposed):
    r, d = x2d.shape
    width = w_in_bf.shape[1] // 6
    n_tiles = r // tm
    n_seg = tm // seg
    mod_rows = shift.shape[1]
    if use_state:
        mod_map = lambda i: (i, 0, 0)
        st_spec = pl.BlockSpec((n_seg, 2, width), lambda i: (i, 0, 0))
        cs_spec = pl.BlockSpec((n_seg, 2, width), lambda i: (i, 0, 0))
        n_state = n_tiles * n_seg
    else:
        mod_map = lambda i: (i // tiles_per_batch, 0, 0)
        st_spec = pl.BlockSpec((1, 2, width), lambda i: (0, 0, 0))
        cs_spec = pl.BlockSpec((1, 2, width), lambda i: (i // tiles_per_batch, 0, 0))
        n_state = n_tiles // tiles_per_batch
    kern = functools.partial(_inproj_kernel, width=width, seg=seg, tiles_per_batch=tiles_per_batch,
                             use_state=use_state, q_scale=q_scale, k_transposed=k_transposed)
    row_spec = lambda: pl.BlockSpec((tm, width), lambda i: (i, 0))
    if k_transposed:
        k_spec = pl.BlockSpec((1, width, tm), lambda i: (i // tiles_per_batch, 0, i % tiles_per_batch))
        k_shape = jax.ShapeDtypeStruct((n_tiles // tiles_per_batch, width, tiles_per_batch * tm), _F32)
    else:
        k_spec = row_spec()
        k_shape = jax.ShapeDtypeStruct((r, width), _F32)
    return pl.pallas_call(
        kern,
        grid=(n_tiles,),
        in_specs=[pl.BlockSpec((tm, d), lambda i: (i, 0)),
                  pl.BlockSpec((1, mod_rows, d), mod_map),
                  pl.BlockSpec((1, mod_rows, d), mod_map),
                  pl.BlockSpec((1, d), lambda i: (0, 0)),
                  pl.BlockSpec((d, 6 * width), lambda i: (0, 0), pipeline_mode=pl.Buffered(1)),
                  pl.BlockSpec((3, width), lambda i: (0, 0)),
                  st_spec],
        out_specs=[row_spec(), k_spec, row_spec(), row_spec(),
                   pl.BlockSpec((width, tm), lambda i: (0, i)),
                   row_spec(), cs_spec],
        out_shape=[jax.ShapeDtypeStruct((r, width), _BF16),
                   k_shape,
                   jax.ShapeDtypeStruct((r, width), _F32),
                   jax.ShapeDtypeStruct((r, width), _BF16),
                   jax.ShapeDtypeStruct((width, r), _BF16),
                   jax.ShapeDtypeStruct((r, width), _BF16),
                   jax.ShapeDtypeStruct((n_state, 2, width), _F32)],
        scratch_shapes=[pltpu.VMEM((seg + 8, width), _F32), pltpu.VMEM((2, width), _F32)],
        compiler_params=_params(("arbitrary",)),
        name="inproj",
    )(x2d, shift, scale, g_mix.reshape(1, d), w_in_bf, conv_w, state)


LOG2E = math.log2(math.e)
SKIP_MARGIN = 160.0
FAST_MARGIN = 60.0
N_AUG = 12


def _split3(v):
    t1 = v.astype(_BF16).astype(_F32)
    r = v - t1
    t2 = r.astype(_BF16).astype(_F32)
    t3 = (r - t2).astype(_BF16).astype(_F32)
    return t1, t2, t3


def _aug_group_term(idx):
    group = (idx >= 3).astype(jnp.int32) + (idx >= 6).astype(jnp.int32) + (idx >= 9).astype(jnp.int32)
    return group, idx - 3 * group


def _attn_prompt_kernel(scal_ref, q_ref, k_ref, vt_ref, g_ref, o_ref,
                        acc_scr, ml_scr, kn_scr, dbias_scr, obias_scr, aug_scr, *, tq, hd, out_scale):
    hh = pl.program_id(1)
    qi = pl.program_id(2)
    lam = scal_ref[0]
    slope2 = scal_ref[1 + hh]
    inv_slope2 = scal_ref[1 + N_HEADS + hh]
    tk = tq
    hw = 2 * hd
    n_kv_total = k_ref.shape[0] // tk

    @pl.when(qi == 0)
    def _():
        lane = lax.broadcasted_iota(jnp.int32, (tk, hw), 1)

        def chunk(c, mx):
            kk = k_ref[pl.ds(pl.multiple_of(c * tk, tk), tk), :].astype(_F32)
            sq = kk * kk
            n0 = jnp.sum(jnp.where(lane < hd, sq, 0.0), axis=1, keepdims=True)
            n1 = jnp.sum(jnp.where(lane >= hd, sq, 0.0), axis=1, keepdims=True)
            return jnp.maximum(mx[0], n0), jnp.maximum(mx[1], n1)

        zero = jnp.zeros((tk, 1), _F32)
        n0, n1 = lax.fori_loop(0, n_kv_total, chunk, (zero, zero))
        kn0 = jnp.sqrt(jnp.max(n0, axis=0, keepdims=True))
        kn1 = jnp.sqrt(jnp.max(n1, axis=0, keepdims=True))
        col = lax.broadcasted_iota(jnp.int32, (1, 2 * tq), 1)
        kn_scr[...] = jnp.where(col < tq, kn0, kn1) * 1.001

        jj = lax.broadcasted_iota(jnp.int32, (tk, tq), 0)
        ii = lax.broadcasted_iota(jnp.int32, (tk, tq), 1)
        jf = jj.astype(_F32)
        iif = ii.astype(_F32)
        visible = (jj // CHUNK) <= (ii // CHUNK)
        dbias = jnp.where(visible, slope2 * iif - slope2 * jnp.abs(iif - jf), MASKED_SCORE)
        dbias_scr[...] = jnp.concatenate([dbias, dbias], axis=1)
        obias = slope2 * jf
        obias_scr[...] = jnp.concatenate([obias, obias], axis=1)

        lc = lane & (hd - 1)
        group, _ = _aug_group_term(lc)
        jrow = lax.broadcasted_iota(jnp.int32, (tk, hw), 0)
        val = jnp.where(group == 0, jrow >> 4, jnp.where(group == 1, jrow & 15, jnp.where(group == 2, 0, 1)))
        aug_scr[...] = jnp.where(lc < N_AUG, val, 0).astype(_F32).astype(_BF16)

    qt = q_ref[...].astype(_F32).T
    qtb = qt.astype(_BF16)
    row2 = lax.broadcasted_iota(jnp.int32, (hw, 2 * tq), 0)
    col2 = lax.broadcasted_iota(jnp.int32, (hw, 2 * tq), 1)
    rhs_bd = jnp.where((row2 < hd) == (col2 < tq), jnp.concatenate([qtb, qtb], axis=1), 0).astype(_BF16)

    def kv_tiles(kv):
        start = pl.multiple_of(kv * tk, tk)
        return k_ref[pl.ds(start, tk), :], vt_ref[:, pl.ds(start, tk)]

    def halves_sum(a):
        return jnp.concatenate([jnp.sum(a[:hd], axis=0, keepdims=True),
                                jnp.sum(a[hd:], axis=0, keepdims=True)], axis=1)

    k_diag, _ = kv_tiles(qi)
    self_dot = halves_sum(qt * k_diag.astype(_F32).T)
    pos = lax.broadcasted_iota(jnp.int32, (1, tq), 1).astype(_F32)
    m_ref = self_dot + slope2 * jnp.concatenate([pos, pos], axis=1)

    bound = jnp.sqrt(halves_sum(qt * qt)) * kn_scr[...]
    gap_skip = jnp.max(bound - m_ref, axis=1, keepdims=True)
    gap_fast = jnp.max(bound - self_dot, axis=1, keepdims=True)
    reach = ((gap_skip + SKIP_MARGIN) * inv_slope2 - 1.0) * (1.0 / tk)
    n_off = jnp.clip(jnp.floor(reach) + 1.0, 0.0, qi.astype(_F32)).astype(jnp.int32)[0, 0]
    fast = (gap_fast <= FAST_MARGIN).astype(jnp.int32)[0, 0] == 1
    kv_lo = qi - n_off

    def diag_scores():
        k, vt = kv_tiles(qi)
        return jnp.dot(k, rhs_bd, preferred_element_type=_F32) + dbias_scr[...], vt

    @pl.when(fast)
    def _():
        lane_lo = lax.broadcasted_iota(jnp.int32, (tk, hw), 1) < hd
        slab_rows = 16
        local = lax.broadcasted_iota(jnp.int32, (slab_rows, tq), 0)
        group, term = _aug_group_term(local)

        def bias_slab(m_half):
            val = jnp.where(group == 0, slope2 * 16.0,
                            jnp.where(group == 1, slope2, jnp.where(group == 2, -slope2 * tk, -m_half)))
            t1, t2, t3 = _split3(val)
            v = jnp.where(term == 0, t1, jnp.where(term == 1, t2, t3))
            return jnp.where(local < N_AUG, v, 0.0)

        s, vt = diag_scores()
        p = jnp.exp2(s - m_ref)
        l8_diag = jnp.sum(p.reshape(tk // 8, 8, 2 * tq), axis=0)
        acc_scr[...] = jnp.dot(vt, p.astype(_BF16), preferred_element_type=_F32)

        pad = jnp.zeros((hd - slab_rows, tq), _F32)
        rhs0 = jnp.concatenate([qt[:hd], bias_slab(m_ref[:, :tq]), pad], axis=0).astype(_BF16)
        rhs1 = jnp.concatenate([bias_slab(m_ref[:, tq:]), pad, qt[hd:]], axis=0).astype(_BF16)
        lcol = lax.broadcasted_iota(jnp.int32, (1, hw), 1) & (hd - 1)
        delta_cols = ((lcol >= 6) & (lcol < 9)).astype(_F32)

        def tile(kv):
            k, vt = kv_tiles(kv)
            delta = (qi - kv).astype(_F32)
            aug = aug_scr[...] + (delta * delta_cols).astype(_BF16)
            s = jnp.concatenate(
                [jnp.dot(jnp.where(lane_lo, k, aug), rhs0, preferred_element_type=_F32),
                 jnp.dot(jnp.where(lane_lo, aug, k), rhs1, preferred_element_type=_F32)], axis=1)
            p = jnp.exp2(s)
            return (jnp.sum(p.reshape(tk // 8, 8, 2 * tq), axis=0),
                    jnp.dot(vt, p.astype(_BF16), preferred_element_type=_F32))

        def pair(j, l8):
            kv = kv_lo + 2 * j
            la, pva = tile(kv)
            lb, pvb = tile(kv + 1)
            acc_scr[...] = acc_scr[...] + (pva + pvb)
            return l8 + (la + lb)

        l8 = lax.fori_loop(0, lax.shift_right_logical(n_off, 1), pair, l8_diag)
        ml_scr[...] = jnp.sum(l8, axis=0, keepdims=True)

        @pl.when((n_off & 1) == 1)
        def _():
            la, pva = tile(qi - 1)
            acc_scr[...] = acc_scr[...] + pva
            ml_scr[...] = ml_scr[...] + jnp.sum(la, axis=0, keepdims=True)

    @pl.when(jnp.logical_not(fast))
    def _():
        s, vt = diag_scores()
        m0 = jnp.max(s, axis=0, keepdims=True)
        p = jnp.exp2(s - m0)
        l0 = jnp.sum(p, axis=0, keepdims=True)
        acc_scr[...] = jnp.dot(vt, p.astype(_BF16), preferred_element_type=_F32)

        def body(kv, carry):
            m, l = carry
            c = -slope2 * ((qi - kv) * tk).astype(_F32)
            k, vt = kv_tiles(kv)
            s = jnp.dot(k, rhs_bd, preferred_element_type=_F32) + obias_scr[...]
            m_new = jnp.maximum(m, jnp.max(s, axis=0, keepdims=True) + c)
            p = jnp.exp2(s - (m_new - c))
            alpha = jnp.exp2(m - m_new)
            l = alpha * l + jnp.sum(p, axis=0, keepdims=True)
            acc_scr[...] = acc_scr[...] * alpha + jnp.dot(vt, p.astype(_BF16), preferred_element_type=_F32)
            return m_new, l

        _, l = lax.fori_loop(kv_lo, qi, body, (m0, l0))
        ml_scr[...] = l

    acc = acc_scr[...] / ml_scr[...]
    o = acc[:, :tq] - lam * acc[:, tq:]
    o = o * lax.rsqrt(jnp.mean(o * o, axis=0, keepdims=True) + NORM_EPS)
    o = o * g_ref[...] * out_scale
    o_ref[...] = o.T.astype(_BF16)


def _attn_prompt(scal, q, kb, vt, subln_col, *, batch, seq, tq, out_scale):
    width = q.shape[1]
    hw = width // N_HEADS
    nq = seq // tq
    kern = functools.partial(_attn_prompt_kernel, tq=tq, hd=hw // 2, out_scale=out_scale)
    return pl.pallas_call(
        kern,
        grid=(batch, N_HEADS, nq),
        in_specs=[pl.BlockSpec(memory_space=pltpu.SMEM),
                  pl.BlockSpec((tq, hw), lambda b, h, i: (b * nq + i, h)),
                  pl.BlockSpec((seq, hw), lambda b, h, i: (b, h)),
                  pl.BlockSpec((hw, seq), lambda b, h, i: (h, b)),
                  pl.BlockSpec((hw, 1), lambda b, h, i: (0, 0))],
        out_specs=pl.BlockSpec((tq, hw), lambda b, h, i: (b * nq + i, h)),
        out_shape=jax.ShapeDtypeStruct((batch * seq, width), _BF16),
        scratch_shapes=[pltpu.VMEM((hw, 2 * tq), _F32),
                        pltpu.VMEM((1, 2 * tq), _F32),
                        pltpu.VMEM((1, 2 * tq), _F32),
                        pltpu.VMEM((tq, 2 * tq), _F32),
                        pltpu.VMEM((tq, 2 * tq), _F32),
                        pltpu.VMEM((tq, hw), _BF16)],
        compiler_params=_params(("arbitrary", "arbitrary", "arbitrary")),
        name="attn_prompt",
    )(scal, q, kb, vt, subln_col)


def _attn_sample_kernel(scal_ref, q_ref, kn_ref, vn_ref, kc_ref, vc_ref, g_ref, o_ref, *, hd, out_scale):
    hh = pl.program_id(1)
    lam = scal_ref[0]
    slope = scal_ref[1 + hh]
    t = q_ref.shape[0]
    past = kc_ref.shape[1]
    q = q_ref[...]
    kc = kc_ref[0].astype(_BF16)
    kn = kn_ref[...].astype(_BF16)
    vc = vc_ref[0].astype(_BF16)
    vn = vn_ref[...].astype(_BF16)
    qpos = lax.broadcasted_iota(jnp.int32, (t, past), 0).astype(_F32) + float(past)
    kpos = lax.broadcasted_iota(jnp.int32, (t, past), 1).astype(_F32)
    bias_c = -slope * jnp.abs(qpos - kpos)
    tq_i = lax.broadcasted_iota(jnp.int32, (t, t), 0).astype(_F32)
    tk_i = lax.broadcasted_iota(jnp.int32, (t, t), 1).astype(_F32)
    bias_n = -slope * jnp.abs(tq_i - tk_i)
    nt = (((1,), (1,)), ((), ()))
    outs = []
    for j in range(2):
        sl = slice(j * hd, (j + 1) * hd)
        sc = lax.dot_general(q[:, sl], kc[:, sl], nt, preferred_element_type=_F32) + bias_c
        sn = lax.dot_general(q[:, sl], kn[:, sl], nt, preferred_element_type=_F32) + bias_n
        m = jnp.maximum(jnp.max(sc, axis=-1, keepdims=True), jnp.max(sn, axis=-1, keepdims=True))
        pc = jnp.exp(sc - m)
        pn = jnp.exp(sn - m)
        l = jnp.sum(pc, axis=-1, keepdims=True) + jnp.sum(pn, axis=-1, keepdims=True)
        o = (jnp.dot(pc.astype(_BF16), vc, preferred_element_type=_F32)
             + jnp.dot(pn.astype(_BF16), vn, preferred_element_type=_F32))
        outs.append(o / l)
    o = outs[0] - lam * outs[1]
    o = _rms(o) * g_ref[...] * out_scale
    o_ref[...] = o.astype(_BF16)


def _attn_sample(scal, q, k_new, v_new, cache_k, cache_v, subln_row, *, batch, t, out_scale):
    width = q.shape[1]
    hw = width // N_HEADS
    past = cache_k.shape[1]
    kern = functools.partial(_attn_sample_kernel, hd=hw // 2, out_scale=out_scale)
    new_spec = lambda: pl.BlockSpec((t, hw), lambda b, h: (b, h))
    cache_spec = lambda: pl.BlockSpec((1, past, hw), lambda b, h: (b, 0, h))
    return pl.pallas_call(
        kern,
        grid=(batch, N_HEADS),
        in_specs=[pl.BlockSpec(memory_space=pltpu.SMEM), new_spec(), new_spec(), new_spec(),
                  cache_spec(), cache_spec(), pl.BlockSpec((1, hw), lambda b, h: (0, 0))],
        out_specs=new_spec(),
        out_shape=jax.ShapeDtypeStruct((batch * t, width), _BF16),
        compiler_params=_params(("arbitrary", "arbitrary")),
        name="attn_sample",
    )(scal, q, k_new, v_new, cache_k, cache_v, subln_row)


def _outproj_kernel(a_ref, c_ref, x_ref, gate_ref, shift_ref, scale_ref, g_ref, wo_ref, wrh_ref, wrl_ref,
                    br_ref, *rest, half, n_tiles):
    x1_ref, h2_ref, lg_ref = rest[-3:]
    i = pl.program_id(0)

    @pl.when(i < n_tiles)
    def _():
        mix = (jnp.dot(a_ref[...], wo_ref[0:half, :], preferred_element_type=_F32)
               + jnp.dot(c_ref[...], wo_ref[half:2 * half, :], preferred_element_type=_F32))
        x1 = x_ref[...] + gate_ref[0] * mix
        x1_ref[...] = x1
        h2 = _rms(x1) * g_ref[...]
        h2 = h2 * (1.0 + scale_ref[0]) + shift_ref[0]
        hi = h2.astype(_BF16)
        dh = h2.shape[1] // 2
        bits = lax.bitcast_convert_type(hi.astype(_F32), jnp.uint32)
        h2_ref[...] = bits[:, :dh] | lax.shift_right_logical(bits[:, dh:], jnp.uint32(16))
        lo = (h2 - hi.astype(_F32)).astype(_BF16)
        lg_ref[...] = (jnp.dot(hi, wrh_ref[...], preferred_element_type=_F32)
                       + jnp.dot(lo, wrh_ref[...], preferred_element_type=_F32)
                       + jnp.dot(hi, wrl_ref[...], preferred_element_type=_F32)) + br_ref[...]

    @pl.when(i >= n_tiles)
    def _():
        h2_ref[...] = jnp.zeros_like(h2_ref)
        lg_ref[...] = jnp.zeros_like(lg_ref)


def _outproj(attn, conv, x2d, gate, shift, scale, g_ffn, w_o_bf, w_router_pad, b_router_pad, *, tm,
             tiles_per_batch, per_row, rows_total, row_off, into=None):
    w_router_hi = w_router_pad.astype(_BF16)
    w_router_lo = (w_router_pad - w_router_hi.astype(_F32)).astype(_BF16)
    r, d = x2d.shape
    half = attn.shape[1]
    mod_rows = gate.shape[1]
    blk_off = row_off // tm
    assert row_off % tm == 0
    n_tiles = r // tm
    n_fill = (rows_total - row_off - r) // tm if into is None else 0
    last = n_tiles - 1
    row_map = lambda i: (jnp.minimum(i, last), 0)
    if per_row:
        mod_map = lambda i: (jnp.minimum(i, last), 0, 0)
    else:
        mod_map = lambda i: (jnp.minimum(i, last) // tiles_per_batch, 0, 0)
    mod_spec = lambda: pl.BlockSpec((1, mod_rows, d), mod_map)
    kern = functools.partial(_outproj_kernel, half=half, n_tiles=n_tiles)
    in_specs = [pl.BlockSpec((tm, half), row_map),
                pl.BlockSpec((tm, half), row_map),
                pl.BlockSpec((tm, d), row_map),
                mod_spec(), mod_spec(), mod_spec(),
                pl.BlockSpec((1, d), lambda i: (0, 0)),
                pl.BlockSpec((2 * half, d), lambda i: (0, 0), pipeline_mode=pl.Buffered(1)),
                pl.BlockSpec((d, ROUTER_LANES), lambda i: (0, 0)),
                pl.BlockSpec((d, ROUTER_LANES), lambda i: (0, 0)),
                pl.BlockSpec((1, ROUTER_LANES), lambda i: (0, 0))]
    args = [attn, conv, x2d, gate, shift, scale, g_ffn.reshape(1, d), w_o_bf, w_router_hi, w_router_lo,
            b_router_pad]
    aliases = {}
    if into is not None:
        aliases = {len(args): 1, len(args) + 1: 2}
        in_specs += [pl.BlockSpec(memory_space=pl.ANY), pl.BlockSpec(memory_space=pl.ANY)]
        args += list(into)
    return pl.pallas_call(
        kern,
        grid=(n_tiles + n_fill,),
        in_specs=in_specs,
        out_specs=[pl.BlockSpec((tm, d), row_map),
                   pl.BlockSpec((tm, d // 2), lambda i: (blk_off + i, 0)),
                   pl.BlockSpec((tm, ROUTER_LANES), lambda i: (blk_off + i, 0))],
        out_shape=[jax.ShapeDtypeStruct((r, d), _F32),
                   jax.ShapeDtypeStruct((rows_total, d // 2), jnp.uint32),
                   jax.ShapeDtypeStruct((rows_total, ROUTER_LANES), _F32)],
        input_output_aliases=aliases,
        compiler_params=_params(("arbitrary",)),
        name="outproj",
    )(*args)


def _row_gather_start(src_hbm, idx_ref, n_rows, dst_of_row, sem):
    for r in range(n_rows):
        pltpu.make_async_copy(src_hbm.at[pl.ds(idx_ref[0, 0, r], 1), :], dst_of_row(r), sem).start(
            priority=r % 2)


def _dispatch_kernel(nv_ref, idx_ref, idx_next_ref, h_hbm, o_ref, buf, sem):
    i = pl.program_id(0)
    nv = nv_ref[0]
    tm = o_ref.shape[0]
    slot = i % 2

    def start(idx_blk_ref, s):
        _row_gather_start(h_hbm, idx_blk_ref, tm, lambda r: buf.at[s, pl.ds(r, 1), :], sem.at[s])

    @pl.when(i == 0)
    def _():
        start(idx_ref, 0)

    @pl.when(i + 1 < nv)
    def _():
        start(idx_next_ref, 1 - slot)

    @pl.when(i < nv)
    def _():
        pltpu.make_async_copy(h_hbm.at[pl.ds(0, tm), :], buf.at[slot], sem.at[slot]).wait()
        u = buf[slot]
        dh = u.shape[1]
        o_ref[:, :dh] = lax.bitcast_convert_type(u & jnp.uint32(0xFFFF0000), _F32).astype(_BF16)
        o_ref[:, dh:] = lax.bitcast_convert_type(lax.shift_left(u, jnp.uint32(16)), _F32).astype(_BF16)

    @pl.when(i >= nv)
    def _():
        o_ref[...] = jnp.zeros_like(o_ref)


def _dispatch(n_valid, tok_sorted, h2_all, *, tm):
    rows = tok_sorted.shape[0]
    n_blocks = rows // tm
    dh = h2_all.shape[1]
    d = 2 * dh
    idx3 = tok_sorted.reshape(n_blocks, 1, tm)
    return pl.pallas_call(
        _dispatch_kernel,
        grid_spec=pltpu.PrefetchScalarGridSpec(
            num_scalar_prefetch=1,
            grid=(n_blocks,),
            in_specs=[pl.BlockSpec((1, 1, tm), lambda i, nv: (i, 0, 0), memory_space=pltpu.SMEM),
                      pl.BlockSpec((1, 1, tm), lambda i, nv: (jnp.minimum(i + 1, n_blocks - 1), 0, 0),
                                   memory_space=pltpu.SMEM),
                      pl.BlockSpec(memory_space=pl.ANY)],
            out_specs=pl.BlockSpec((tm, d), lambda i, nv: (i, 0)),
            scratch_shapes=[pltpu.VMEM((2, tm, dh), jnp.uint32), pltpu.SemaphoreType.DMA((2,))]),
        out_shape=jax.ShapeDtypeStruct((rows, d), _BF16),
        compiler_params=_params(("arbitrary",)),
        name="dispatch",
    )(n_valid, idx3, idx3, h2_all)


def _expert_changed(be_ref, i):
    return jnp.logical_or(i == 0, be_ref[i] != be_ref[jnp.maximum(i - 1, 0)])


def _moe_up_kernel(be_ref, nv_ref, x_ref, wg_ref, wu_ref, bg_ref, bu_ref, act_ref, wg_scr, wu_scr):
    i = pl.program_id(1)
    valid = i < nv_ref[0]

    @pl.when(jnp.logical_and(valid, _expert_changed(be_ref, i)))
    def _():
        wg_scr[...] = wg_ref[0].astype(_BF16)
        wu_scr[...] = wu_ref[0].astype(_BF16)

    @pl.when(valid)
    def _():
        x = x_ref[...]
        g = jnp.dot(x, wg_scr[...], preferred_element_type=_F32) + bg_ref[0]
        u = jnp.dot(x, wu_scr[...], preferred_element_type=_F32) + bu_ref[0]
        g = jnp.minimum(g, SWIGLU_LIMIT)
        u = jnp.clip(u, -SWIGLU_LIMIT, SWIGLU_LIMIT)
        act_ref[...] = ((u + 1.0) * g * jax.nn.sigmoid(SWIGLU_ALPHA * g)).astype(_BF16)

    @pl.when(jnp.logical_not(valid))
    def _():
        act_ref[...] = jnp.zeros_like(act_ref)


def _moe_down_kernel(be_ref, nv_ref, a_ref, wd_ref, bd_ref, y_ref, wd_scr):
    i = pl.program_id(1)
    valid = i < nv_ref[0]

    @pl.when(jnp.logical_and(valid, _expert_changed(be_ref, i)))
    def _():
        wd_scr[...] = wd_ref[0].astype(_BF16)

    @pl.when(valid)
    def _():
        y_ref[...] = jnp.dot(a_ref[...], wd_scr[...], preferred_element_type=_F32) + bd_ref[0]

    @pl.when(jnp.logical_not(valid))
    def _():
        y_ref[...] = jnp.zeros_like(y_ref)


def _moe(block_e, n_valid, xs, w_gu, b_gu, w_down, b_down, *, tm, tn, tn_down):
    rows, d = xs.shape
    n_exp, _, two_ff = w_gu.shape
    d_ff = two_ff // 2
    n_blocks = rows // tm
    n_up = d_ff // tn
    b_gu3 = b_gu.reshape(n_exp, 1, two_ff)
    act = pl.pallas_call(
        _moe_up_kernel,
        grid_spec=pltpu.PrefetchScalarGridSpec(
            num_scalar_prefetch=2,
            grid=(n_up, n_blocks),
            in_specs=[pl.BlockSpec((tm, d), lambda n, i, be, nv: (i, 0)),
                      pl.BlockSpec((1, d, tn), lambda n, i, be, nv: (be[i], 0, n)),
                      pl.BlockSpec((1, d, tn), lambda n, i, be, nv: (be[i], 0, n_up + n)),
                      pl.BlockSpec((1, 1, tn), lambda n, i, be, nv: (be[i], 0, n)),
                      pl.BlockSpec((1, 1, tn), lambda n, i, be, nv: (be[i], 0, n_up + n))],
            out_specs=pl.BlockSpec((tm, tn), lambda n, i, be, nv: (i, n)),
            scratch_shapes=[pltpu.VMEM((d, tn), _BF16), pltpu.VMEM((d, tn), _BF16)]),
        out_shape=jax.ShapeDtypeStruct((rows, d_ff), _BF16),
        compiler_params=_params(("arbitrary", "arbitrary")),
        name="moe_up",
    )(block_e, n_valid, xs, w_gu, w_gu, b_gu3, b_gu3)
    n_down = d // tn_down
    return pl.pallas_call(
        _moe_down_kernel,
        grid_spec=pltpu.PrefetchScalarGridSpec(
            num_scalar_prefetch=2,
            grid=(n_down, n_blocks),
            in_specs=[pl.BlockSpec((tm, d_ff), lambda n, i, be, nv: (i, 0)),
                      pl.BlockSpec((1, d_ff, tn_down), lambda n, i, be, nv: (be[i], 0, n)),
                      pl.BlockSpec((1, 1, tn_down), lambda n, i, be, nv: (be[i], 0, n))],
            out_specs=pl.BlockSpec((tm, tn_down), lambda n, i, be, nv: (i, n)),
            scratch_shapes=[pltpu.VMEM((d_ff, tn_down), _BF16)]),
        out_shape=jax.ShapeDtypeStruct((rows, d), _F32),
        compiler_params=_params(("arbitrary", "arbitrary")),
        name="moe_down",
    )(block_e, n_valid, act, w_down, b_down.reshape(n_exp, 1, d))


def _combine_kernel(idx_ref, idx_next_ref, x1_ref, gk_ref, gate_ref, g_ref, ys_hbm, y_ref, buf, sem):
    i = pl.program_id(0)
    n = pl.num_programs(0)
    tm, d = x1_ref.shape
    slot = i % 2

    def start(idx_blk_ref, s):
        _row_gather_start(ys_hbm, idx_blk_ref, tm * TOP_K,
                          lambda r: buf.at[s, r % TOP_K, pl.ds(r // TOP_K, 1), :], sem.at[s])

    @pl.when(i == 0)
    def _():
        start(idx_ref, 0)

    @pl.when(i + 1 < n)
    def _():
        start(idx_next_ref, 1 - slot)

    for k in range(TOP_K):
        pltpu.make_async_copy(ys_hbm.at[pl.ds(0, tm), :], buf.at[slot, k], sem.at[slot]).wait()
    gk = gk_ref[...]
    ff = gk[:, 0:1] * buf[slot, 0]
    for k in range(1, TOP_K):
        ff = ff + gk[:, k:k + 1] * buf[slot, k]
    x2 = x1_ref[...] + gate_ref[0] * ff
    y_ref[...] = _rms(x2) * g_ref[...]


def _combine(x1, ys, dest, gk, gate, g_final, *, tm, tiles_per_batch, per_row):
    r, d = x1.shape
    n_tiles = r // tm
    mod_rows = gate.shape[1]
    mod_map = (lambda i: (i, 0, 0)) if per_row else (lambda i: (i // tiles_per_batch, 0, 0))
    idx3 = dest.reshape(n_tiles, 1, tm * TOP_K)
    return pl.pallas_call(
        _combine_kernel,
        grid=(n_tiles,),
        in_specs=[pl.BlockSpec((1, 1, tm * TOP_K), lambda i: (i, 0, 0), memory_space=pltpu.SMEM),
                  pl.BlockSpec((1, 1, tm * TOP_K), lambda i: (jnp.minimum(i + 1, n_tiles - 1), 0, 0),
                               memory_space=pltpu.SMEM),
                  pl.BlockSpec((tm, d), lambda i: (i, 0)),
                  pl.BlockSpec((tm, TOP_K), lambda i: (i, 0)),
                  pl.BlockSpec((1, mod_rows, d), mod_map),
                  pl.BlockSpec((1, d), lambda i: (0, 0)),
                  pl.BlockSpec(memory_space=pl.ANY)],
        out_specs=pl.BlockSpec((tm, d), lambda i: (i, 0)),
        out_shape=jax.ShapeDtypeStruct((r, d), _F32),
        scratch_shapes=[pltpu.VMEM((2, TOP_K, tm, d), _F32), pltpu.SemaphoreType.DMA((2,))],
        compiler_params=_params(("arbitrary",)),
        name="combine",
    )(idx3, idx3, x1, gk, gate, g_final.reshape(1, d), ys)


def _route(logits, n_exp, tm):
    t = logits.shape[0]
    top_v, top_e = lax.top_k(logits, TOP_K)
    gate = jax.nn.softmax(top_v, axis=-1)
    m = t * TOP_K
    n_blocks = -(-m // tm) + n_exp
    rows = n_blocks * tm
    i32 = jnp.int32
    flat_e = top_e.reshape(-1).astype(i32)
    onehot = (flat_e[:, None] == jnp.arange(n_exp, dtype=i32)[None, :]).astype(i32)
    csum = jnp.cumsum(onehot, axis=0)
    rank = jnp.sum(csum * onehot, axis=1) - 1
    counts = csum[-1]
    padded = ((counts + tm - 1) // tm) * tm
    cum_end = jnp.cumsum(padded)
    pstart = cum_end - padded
    cstart = jnp.cumsum(counts) - counts
    n_valid = (cum_end[-1] // tm).astype(i32).reshape(1)
    blk_start = jnp.arange(n_blocks, dtype=i32) * tm
    n_before = jnp.sum((cum_end[None, :] <= blk_start[:, None]).astype(i32), axis=1)
    block_e = jnp.minimum(n_before, n_exp - 1)
    last_e = jnp.sum(jnp.where(jnp.arange(n_blocks) == n_valid[0] - 1, block_e, 0))
    block_e = jnp.where(jnp.arange(n_blocks) < n_valid[0], block_e, last_e).astype(i32)
    assert n_exp * m < 2 ** 31
    order = lax.sort(flat_e * m + jnp.arange(m, dtype=i32)) % m
    row = jnp.arange(rows, dtype=i32)
    e_row = jnp.repeat(block_e, tm)
    off = row - pstart[e_row]
    is_slot = (off < counts[e_row]) & (row < cum_end[-1])
    src = jnp.clip(cstart[e_row] + off, 0, m - 1)
    tok_sorted = jnp.where(is_slot, order[src] // TOP_K, 0).astype(i32)
    dest = (jnp.sum(onehot * pstart[None, :], axis=1) + rank).astype(i32)
    return gate, dest.reshape(t, TOP_K), tok_sorted, block_e, n_valid


def _pick_tile(n, pref):
    t = min(n, pref)
    while n % t:
        t //= 2
    return t


def kernel(x_prompt, x_sample, cache_k, cache_v, state_conv, c_prompt, c_sample, g_mix, g_ffn, w_ada, b_ada, w_in, lambda_q1, lambda_k1, lambda_q2, lambda_k2, subln_g, conv_w, w_o, w_router, b_router, w_gu, b_gu, w_down, b_down, g_final):
    depth = g_mix.shape[0]
    assert depth == 1
    bp, sp, d = x_prompt.shape
    bs, ts, _ = x_sample.shape
    past = cache_k.shape[2]
    aw = d // 2
    hw = aw // N_HEADS
    hd = hw // 2
    n_exp = w_router.shape[-1]
    layer = 0

    n_c = bp + bs
    c_rows = -(-n_c // 8) * 8
    c_all = jnp.concatenate([c_prompt, c_sample, jnp.zeros((c_rows - n_c, d), _F32)], axis=0)
    ada = _ada(c_all, w_ada[layer], b_ada[layer])
    shift1, scale1, gate1, shift2, scale2, gate2 = [ada[:, i * d:(i + 1) * d] for i in range(6)]

    lam_init = 0.8 - 0.6 * math.exp(-0.3 * layer)
    lam = (jnp.exp(jnp.sum(lambda_q1[layer] * lambda_k1[layer]))
           - jnp.exp(jnp.sum(lambda_q2[layer] * lambda_k2[layer])) + lam_init)
    slopes = 2.0 ** (-8.0 * np.arange(1, N_HEADS + 1) / N_HEADS)
    lam1 = lam.reshape(1).astype(_F32)
    scal_s = jnp.concatenate([lam1, jnp.asarray(slopes, _F32)])
    scal_p = jnp.concatenate([lam1, jnp.asarray(slopes * LOG2E, _F32), jnp.asarray(1.0 / (slopes * LOG2E), _F32)])
    out_scale = 1.0 - lam_init
    q_scale = hd ** -0.5

    w_in_bf = w_in[layer].astype(_BF16)
    w_o_bf = w_o[layer].astype(_BF16)
    w_router_pad = jnp.zeros((d, ROUTER_LANES), _F32).at[:, :n_exp].set(w_router[layer])
    b_router_pad = jnp.zeros((1, ROUTER_LANES), _F32).at[0, :n_exp].set(b_router[layer])

    rp = bp * sp
    tm_p = _pick_tile(sp, ROW_TILE)
    tpb = sp // tm_p
    xp2 = x_prompt.reshape(rp, d)
    mod_p = lambda a: a[:bp].reshape(bp, 1, d)
    qp, kp, vp, kbp, vtp, cop, csp = _inproj(
        xp2, mod_p(shift1), mod_p(scale1), g_mix[layer], w_in_bf, conv_w[layer],
        jnp.zeros((1, 2, aw), _F32), tm=tm_p, seg=tm_p, tiles_per_batch=tpb, use_state=False,
        q_scale=q_scale * LOG2E, k_transposed=True)
    tq = _pick_tile(sp, ROW_TILE)
    ap = _attn_prompt(scal_p, qp, kbp, vtp, subln_g[layer].reshape(hw, 1), batch=bp, seq=sp, tq=tq,
                      out_scale=out_scale)
    rs = bs * ts
    x1p, h2_all, lg_all = _outproj(ap, cop, xp2, mod_p(gate1), mod_p(shift2), mod_p(scale2), g_ffn[layer],
                                   w_o_bf, w_router_pad, b_router_pad, tm=tm_p, tiles_per_batch=tpb,
                                   per_row=False, rows_total=rp + rs, row_off=0)

    mod_s = lambda a: jnp.repeat(a[bp:bp + bs], ts, axis=0).reshape(1, rs, d)
    xs2 = x_sample.reshape(rs, d)
    qs, ks, vs, kbs, _, cos, css = _inproj(
        xs2, mod_s(shift1), mod_s(scale1), g_mix[layer], w_in_bf, conv_w[layer],
        state_conv[layer], tm=rs, seg=ts, tiles_per_batch=1, use_state=True, q_scale=q_scale,
        k_transposed=False)
    a_s = _attn_sample(scal_s, qs, kbs, vs, cache_k[layer].reshape(bs, past, aw),
                       cache_v[layer].reshape(bs, past, aw), subln_g[layer].reshape(1, hw),
                       batch=bs, t=ts, out_scale=out_scale)
    x1s, h2_all, lg_all = _outproj(a_s, cos, xs2, mod_s(gate1), mod_s(shift2), mod_s(scale2), g_ffn[layer],
                                   w_o_bf, w_router_pad, b_router_pad, tm=rs, tiles_per_batch=1,
                                   per_row=True, rows_total=rp + rs, row_off=rp, into=(h2_all, lg_all))

    tm_e = ROW_TILE
    gate, dest, tok_sorted, block_e, n_valid = _route(lg_all[:, :n_exp], n_exp, tm_e)
    xs_sorted = _dispatch(n_valid, tok_sorted, h2_all, tm=tm_e)
    d_ff = w_down.shape[2]
    ys = _moe(block_e, n_valid, xs_sorted, w_gu[layer], b_gu[layer], w_down[layer], b_down[layer],
              tm=tm_e, tn=_pick_tile(d_ff, EXPERT_UP_COL_TILE), tn_down=_pick_tile(d, EXPERT_DOWN_COL_TILE))

    tm_c = _pick_tile(sp, COMBINE_TILE)
    y_p = _combine(x1p, ys, dest[:rp], gate[:rp], mod_p(gate2), g_final, tm=tm_c,
                   tiles_per_batch=sp // tm_c, per_row=False)
    tm_cs = _pick_tile(rs, COMBINE_TILE)
    gate2_s = mod_s(gate2).reshape(rs // tm_cs, tm_cs, d)
    y_s = _combine(x1s, ys, dest[rp:], gate[rp:], gate2_s, g_final, tm=tm_cs, tiles_per_batch=1,
                   per_row=True)

    return (y_p.reshape(bp, sp, d), y_s.reshape(bs, ts, d),
            jnp.transpose(kp.reshape(bp, N_HEADS, 2, hd, sp), (0, 4, 1, 2, 3))[None],
            vp.reshape(1, bp, sp, N_HEADS, hw),
            csp.reshape(1, bp, 2, aw),
            ks.reshape(1, bs, ts, N_HEADS, 2, hd), vs.reshape(1, bs, ts, N_HEADS, hw),
            css.reshape(1, bs, 2, aw))
```
